```python
import math
import jax, jax.numpy as jnp
from jax import lax
import numpy as np

D_MODEL = 2048
BATCH = 8
SEQ = 4096
DEPTH = 4

GRID_W = 64
CTX_LEN = 256
HEAD_DIM = 128
H_A = 4
WIN_H = 8
WIN_W = 16
H_B = 8
KV_B = 2
H_C = 4
DC = HEAD_DIM // 2
Q_BLOCK = 128
ROPE_THETA = 10000.0
EPS = 1e-6
A_W = H_A * HEAD_DIM
B_QW = H_B * HEAD_DIM
B_KW = KV_B * HEAD_DIM
C_W = H_C * HEAD_DIM
GATE_W = 3 * D_MODEL
SPLITS = (A_W, A_W, A_W, A_W, B_QW, B_KW, B_KW, B_QW, C_W, C_W, C_W, C_W, GATE_W)
N_IN = sum(SPLITS)

kernel_name = "hybrid_natten_gqa_diffattn_prefix_dit"


def _rmsnorm(x, w):
    xf = x.astype(jnp.float32)
    y = xf * lax.rsqrt(jnp.mean(xf * xf, axis=-1, keepdims=True) + EPS)
    return y.astype(x.dtype) * w


def _axial_rope_tables(length, rot_dim):
    pos = jnp.arange(length)
    row = (pos // GRID_W).astype(jnp.float32)
    col = (pos % GRID_W).astype(jnp.float32)
    n = rot_dim // 4
    inv_freq = ROPE_THETA ** (-jnp.arange(n, dtype=jnp.float32) / n)
    ang = jnp.concatenate([row[:, None] * inv_freq, col[:, None] * inv_freq], axis=-1)
    return jnp.cos(ang), jnp.sin(ang)


def _apply_rope(x, cos, sin):
    half = x.shape[-1] // 2
    shape = (cos.shape[0],) + (1,) * (x.ndim - 3) + (half,)
    co = cos.reshape(shape)
    si = sin.reshape(shape)
    xf = x.astype(jnp.float32)
    x1, x2 = xf[..., :half], xf[..., half:]
    return jnp.concatenate([x1 * co - x2 * si, x1 * si + x2 * co], axis=-1).astype(x.dtype)


def _split_cols(p):
    idx = []
    acc = 0
    for s in SPLITS[:-1]:
        acc += s
        idx.append(acc)
    return jnp.split(p, idx, axis=-1)


def _prep(p, q_norm_w, k_norm_w, rope_b, rope_c):
    b, length = p.shape[0], p.shape[1]
    aq, ak, av, az, bq, bk, bv, bz, cq, ck, cv, cz, gates = _split_cols(p)
    aq = aq.reshape(b, length, H_A, HEAD_DIM)
    ak = ak.reshape(b, length, H_A, HEAD_DIM)
    av = av.reshape(b, length, H_A, HEAD_DIM)
    bq = _rmsnorm(bq.reshape(b, length, H_B, HEAD_DIM), q_norm_w)
    bk = _rmsnorm(bk.reshape(b, length, KV_B, HEAD_DIM), k_norm_w)
    bv = bv.reshape(b, length, KV_B, HEAD_DIM)
    cq = cq.reshape(b, length, H_C, 2, DC)
    ck = ck.reshape(b, length, H_C, 2, DC)
    cv = cv.reshape(b, length, H_C, HEAD_DIM)
    if rope_b is not None:
        bq = _apply_rope(bq, *rope_b)
        bk = _apply_rope(bk, *rope_b)
        cq = _apply_rope(cq, *rope_c)
        ck = _apply_rope(ck, *rope_c)
    return aq, ak, av, az, bq, bk, bv, bz, cq, ck, cv, cz, gates


def _blocked_map(fn, q):
    b, lq = q.shape[0], q.shape[1]
    nb = lq // Q_BLOCK
    qb = jnp.moveaxis(q.reshape((b, nb, Q_BLOCK) + q.shape[2:]), 1, 0)
    o = lax.map(fn, qb)
    return jnp.moveaxis(o, 0, 1).reshape((b, lq) + o.shape[3:])


def _gqa_attend(q, k, v):
    scale = q.shape[-1] ** -0.5

    def block(qi):
        s = jnp.einsum("bqgrd,bkgd->bgrqk", qi, k).astype(jnp.float32) * scale
        p = jax.nn.softmax(s, axis=-1).astype(v.dtype)
        return jnp.einsum("bgrqk,bkgd->bqgrd", p, v)

    o = _blocked_map(block, q)
    return o.reshape(o.shape[0], o.shape[1], -1)


def _diff_attend(q, k, v, lam, subln_w, lam_init):
    scale = q.shape[-1] ** -0.5

    def block(qi):
        s = jnp.einsum("bqhmd,bkhmd->bhmqk", qi, k).astype(jnp.float32) * scale
        p = jax.nn.softmax(s, axis=-1)
        a = (p[:, :, 0] - lam * p[:, :, 1]).astype(v.dtype)
        return jnp.einsum("bhqk,bkhd->bqhd", a, v)

    o = _blocked_map(block, q)
    o = _rmsnorm(o, subln_w) * (1.0 - lam_init)
    return o.reshape(o.shape[0], o.shape[1], -1)


def _neighbourhood_attend(q, k, v, k_ctx, v_ctx, rpb):
    b, length, h, d = q.shape
    rows = length // GRID_W
    kh = min(WIN_H, rows)
    scale = d ** -0.5
    r = jnp.arange(rows)
    row_start = jnp.clip(r - kh // 2, 0, rows - kh)
    band = row_start[:, None] + jnp.arange(kh)
    cidx = jnp.arange(GRID_W)
    col_start = jnp.clip(cidx - WIN_W // 2, 0, GRID_W - WIN_W)
    in_win = (cidx[None, :] >= col_start[:, None]) & (cidx[None, :] < col_start[:, None] + WIN_W)
    dr = band - r[:, None] + (WIN_H - 1)
    dc = jnp.clip(cidx[None, :] - cidx[:, None], -(WIN_W - 1), WIN_W - 1) + (WIN_W - 1)
    bias = rpb[:, dr[:, :, None, None], dc[None, None]]
    bias = jnp.transpose(bias, (0, 1, 3, 2, 4))
    qg = q.reshape(b, rows, GRID_W, h, d)
    kb = k.reshape(b, rows, GRID_W, h, d)[:, band]
    vb = v.reshape(b, rows, GRID_W, h, d)[:, band]
    s_win = jnp.einsum("brqhd,brjkhd->bhrqjk", qg, kb).astype(jnp.float32) * scale + bias
    s_win = jnp.where(in_win[:, None, :], s_win, -jnp.inf)
    s_ctx = jnp.einsum("brqhd,bchd->bhrqc", qg, k_ctx).astype(jnp.float32) * scale
    nwin = kh * GRID_W
    s = jnp.concatenate([s_win.reshape(b, h, rows, GRID_W, nwin), s_ctx], axis=-1)
    p = jax.nn.softmax(s, axis=-1).astype(v.dtype)
    p_win = p[..., :nwin].reshape(b, h, rows, GRID_W, kh, GRID_W)
    p_ctx = p[..., nwin:]
    o = jnp.einsum("bhrqjk,brjkhd->brqhd", p_win, vb) + jnp.einsum("bhrqc,bchd->brqhd", p_ctx, v_ctx)
    return o.reshape(b, length, h * d)


def _gated_merge(o_a, z_a, o_b, z_b, o_c, z_c, gates, b_gate, w_bo_a, w_bo_b, w_bo_c, w_out):
    g_a, g_b, g_c = jnp.split(jax.nn.sigmoid(gates + b_gate), 3, axis=-1)
    y = (g_a * ((o_a * jax.nn.silu(z_a)) @ w_bo_a)
         + g_b * ((o_b * jax.nn.silu(z_b)) @ w_bo_b)
         + g_c * ((o_c * jax.nn.silu(z_c)) @ w_bo_c))
    return y @ w_out


def _hybrid_layer(x, xc, c, c_ctx, layer_idx, rope_b, rope_c, update_ctx, norm_w, w_ada, b_ada, w_in, b_gate,
                  rpb, q_norm_w, k_norm_w, lam_q1, lam_k1, lam_q2, lam_k2, subln_w, w_bo_a, w_bo_b, w_bo_c, w_out):
    b, length = x.shape[0], x.shape[1]
    shift, scale, gate = jnp.split(jax.nn.silu(c) @ w_ada + b_ada, 3, axis=-1)
    shift_c, scale_c, gate_c = jnp.split(jax.nn.silu(c_ctx) @ w_ada + b_ada, 3, axis=-1)
    h = _rmsnorm(x, norm_w) * (1.0 + scale[:, None]) + shift[:, None]
    hc = _rmsnorm(xc, norm_w) * (1.0 + scale_c) + shift_c
    aq, ak, av, az, bq, bk, bv, bz, cq, ck, cv, cz, gl = _prep(h @ w_in, q_norm_w, k_norm_w, rope_b, rope_c)
    aqc, akc, avc, azc, bqc, bkc, bvc, bzc, cqc, ckc, cvc, czc, glc = _prep(hc @ w_in, q_norm_w, k_norm_w, None, None)
    lam_init = 0.8 - 0.6 * math.exp(-0.3 * layer_idx)
    lam = (jnp.exp(jnp.sum(lam_q1.astype(jnp.float32) * lam_k1.astype(jnp.float32)))
           - jnp.exp(jnp.sum(lam_q2.astype(jnp.float32) * lam_k2.astype(jnp.float32))) + lam_init)
    rep = H_B // KV_B
    o_a = _neighbourhood_attend(aq, ak, av, akc, avc, rpb)
    o_b = _gqa_attend(bq.reshape(b, length, KV_B, rep, HEAD_DIM),
                      jnp.concatenate([bkc, bk], axis=1), jnp.concatenate([bvc, bv], axis=1))
    o_c = _diff_attend(cq, jnp.concatenate([ckc, ck], axis=1), jnp.concatenate([cvc, cv], axis=1),
                       lam, subln_w, lam_init)
    out = _gated_merge(o_a, az, o_b, bz, o_c, cz, gl, b_gate, w_bo_a, w_bo_b, w_bo_c, w_out)
    x = x + gate[:, None] * out
    if update_ctx:
        lc = xc.shape[1]
        oc_a = _gqa_attend(aqc[:, :, :, None], akc, avc)
        oc_b = _gqa_attend(bqc.reshape(b, lc, KV_B, rep, HEAD_DIM), bkc, bvc)
        oc_c = _diff_attend(cqc, ckc, cvc, lam, subln_w, lam_init)
        out_c = _gated_merge(oc_a, azc, oc_b, bzc, oc_c, czc, glc, b_gate, w_bo_a, w_bo_b, w_bo_c, w_out)
        xc = xc + gate_c * out_c
    return x, xc


def setup_inputs(seed: int = 0) -> dict:
    key = jax.random.key(seed)
    ks = jax.random.split(key, 24)
    f32 = jnp.float32
    sd = D_MODEL ** -0.5

    def nrm(k, shape, s):
        return jax.random.normal(k, shape, f32) * s

    return {
        "x": nrm(ks[0], (BATCH, SEQ, D_MODEL), 1.0),
        "c": nrm(ks[1], (BATCH, D_MODEL), 1.0),
        "ctx": nrm(ks[2], (BATCH, CTX_LEN, D_MODEL), 1.0),
        "c_ctx": nrm(ks[3], (D_MODEL,), 1.0),
        "norm_w": 1.0 + nrm(ks[4], (DEPTH, D_MODEL), 0.02),
        "w_ada": nrm(ks[5], (DEPTH, D_MODEL, 3 * D_MODEL), 0.5 * sd),
        "b_ada": nrm(ks[6], (DEPTH, 3 * D_MODEL), 0.01),
        "w_in": nrm(ks[7], (DEPTH, D_MODEL, N_IN), sd),
        "b_gate": nrm(ks[8], (DEPTH, GATE_W), 0.01),
        "rpb": nrm(ks[9], (DEPTH, H_A, 2 * WIN_H - 1, 2 * WIN_W - 1), 0.02),
        "q_norm_w": 1.0 + nrm(ks[10], (DEPTH, HEAD_DIM), 0.02),
        "k_norm_w": 1.0 + nrm(ks[11], (DEPTH, HEAD_DIM), 0.02),
        "lam_q1": nrm(ks[12], (DEPTH, DC), 0.1),
        "lam_k1": nrm(ks[13], (DEPTH, DC), 0.1),
        "lam_q2": nrm(ks[14], (DEPTH, DC), 0.1),
        "lam_k2": nrm(ks[15], (DEPTH, DC), 0.1),
        "subln_w": 1.0 + nrm(ks[16], (DEPTH, HEAD_DIM), 0.02),
        "w_bo_a": nrm(ks[17], (DEPTH, A_W, D_MODEL), A_W ** -0.5),
        "w_bo_b": nrm(ks[18], (DEPTH, B_QW, D_MODEL), B_QW ** -0.5),
        "w_bo_c": nrm(ks[19], (DEPTH, C_W, D_MODEL), C_W ** -0.5),
        "w_out": nrm(ks[20], (DEPTH, D_MODEL, D_MODEL), sd),
        "final_norm_w": 1.0 + nrm(ks[21], (D_MODEL,), 0.02),
    }


def reference(x, c, ctx, c_ctx, norm_w, w_ada, b_ada, w_in, b_gate, rpb, q_norm_w, k_norm_w,
              lam_q1, lam_k1, lam_q2, lam_k2, subln_w, w_bo_a, w_bo_b, w_bo_c, w_out, final_norm_w):
    length = x.shape[1]
    rope_b = _axial_rope_tables(length, HEAD_DIM)
    rope_c = _axial_rope_tables(length, DC)
    xl, xc = x, ctx
    for l in range(DEPTH):
        xl, xc = _hybrid_layer(xl, xc, c, c_ctx, l, rope_b, rope_c, l < DEPTH - 1,
                               norm_w[l], w_ada[l], b_ada[l], w_in[l], b_gate[l], rpb[l],
                               q_norm_w[l], k_norm_w[l], lam_q1[l], lam_k1[l], lam_q2[l], lam_k2[l],
                               subln_w[l], w_bo_a[l], w_bo_b[l], w_bo_c[l], w_out[l])
    return _rmsnorm(xl, final_norm_w)
```

```python
import functools

import numpy as np
import jax
import jax.numpy as jnp
from jax import lax
from jax.experimental import pallas as pl
from jax.experimental.pallas import tpu as pltpu

F32 = jnp.float32
BF16 = jnp.bfloat16

GRID_W = 64
HEAD_DIM = 128
H_A = 4
WIN_H = 8
WIN_W = 16
H_B = 8
KV_B = 2
H_C = 4
DC = HEAD_DIM // 2
ROPE_THETA = 10000.0
EPS = 1e-6
NEG = -1e30

A_W = H_A * HEAD_DIM
B_QW = H_B * HEAD_DIM
B_KW = KV_B * HEAD_DIM
C_W = H_C * HEAD_DIM

OFF_AQ, OFF_AK, OFF_AV, OFF_AZ = 0, A_W, 2 * A_W, 3 * A_W
OFF_BQ = 4 * A_W
OFF_BK = OFF_BQ + B_QW
OFF_BV = OFF_BK + B_KW
OFF_BZ = OFF_BV + B_KW
OFF_CQ = OFF_BZ + B_QW
OFF_CK = OFF_CQ + C_W
OFF_CV = OFF_CK + C_W
OFF_CZ = OFF_CV + C_W
P_W = OFF_CZ + C_W

TN = 512
NP_TILES = P_W // TN

NBR_R = 4
NBR_KR = 12

VMEM_LIMIT = 56 * 1024 * 1024


def _params(n_axes):
    return pltpu.CompilerParams(dimension_semantics=("arbitrary",) * n_axes, vmem_limit_bytes=VMEM_LIMIT)


def _sigmoid(x):
    return 1.0 / (1.0 + jnp.exp(-x))


def _dot(a, b):
    return jnp.dot(a, b, preferred_element_type=F32)


def _dot_nt(a, b):
    return lax.dot_general(a, b, (((1,), (1,)), ((), ())), preferred_element_type=F32)


def _head_rmsnorm(y, w):
    return y * lax.rsqrt(jnp.mean(y * y, axis=-1, keepdims=True) + EPS) * w


def _ada_kernel(c_ref, w_ref, b_ref, o_ref):
    a = c_ref[...]
    a = a * _sigmoid(a)
    o_ref[0] = _dot(a.astype(BF16), w_ref[0].astype(BF16)) + b_ref[0]


def _ada_call(cc, w_ada, b_ada):
    depth, d, n3 = w_ada.shape
    rows = cc.shape[0]
    tn = 1536
    return pl.pallas_call(
        _ada_kernel,
        grid=(depth, n3 // tn),
        in_specs=[
            pl.BlockSpec((rows, d), lambda l, j: (0, 0)),
            pl.BlockSpec((1, d, tn), lambda l, j: (l, 0, j)),
            pl.BlockSpec((1, 1, tn), lambda l, j: (l, 0, j)),
        ],
        out_specs=pl.BlockSpec((1, rows, tn), lambda l, j: (l, 0, j)),
        out_shape=jax.ShapeDtypeStruct((depth, rows, n3), F32),
        compiler_params=_params(2),
        name="ada",
    )(cc, w_ada, b_ada.reshape(depth, 1, n3))


def _norm_mod_kernel(x_ref, w_ref, mod_ref, o_ref):
    x = x_ref[0]
    y = x * lax.rsqrt(jnp.mean(x * x, axis=-1, keepdims=True) + EPS)
    y = y * w_ref[...]
    shift = mod_ref[0, 0:1, :]
    scale = mod_ref[0, 1:2, :]
    o_ref[0] = (y * (1.0 + scale) + shift).astype(o_ref.dtype)


def _norm_kernel(x_ref, w_ref, o_ref):
    x = x_ref[0]
    y = x * lax.rsqrt(jnp.mean(x * x, axis=-1, keepdims=True) + EPS)
    o_ref[0] = (y * w_ref[...]).astype(o_ref.dtype)


def _row_tile(n, cap):
    t = min(n, cap)
    assert n % t == 0
    return t


def _norm_call(x, w, mod, out_dtype):
    b, length, d = x.shape
    tm = _row_tile(length, 512)
    in_specs = [
        pl.BlockSpec((1, tm, d), lambda bi, i: (bi, i, 0)),
        pl.BlockSpec((1, d), lambda bi, i: (0, 0)),
    ]
    args = [x, w.reshape(1, d)]
    if mod is not None:
        per_batch = mod.shape[0] != 1
        in_specs.append(pl.BlockSpec((1, 3, d), (lambda bi, i: (bi, 0, 0)) if per_batch else (lambda bi, i: (0, 0, 0))))
        args.append(mod)
    return pl.pallas_call(
        _norm_mod_kernel if mod is not None else _norm_kernel,
        grid=(b, length // tm),
        in_specs=in_specs,
        out_specs=pl.BlockSpec((1, tm, d), lambda bi, i: (bi, i, 0)),
        out_shape=jax.ShapeDtypeStruct((b, length, d), out_dtype),
        compiler_params=_params(2),
        name="norm",
    )(*args)


def _inproj_kernel(h_ref, w_ref, bg_ref, qw_ref, kw_ref, cb_ref, sb_ref, cc_ref, sc_ref, p_ref, g_ref):
    j = pl.program_id(1)
    acc = _dot(h_ref[...], w_ref[...])
    scale_ab = HEAD_DIM ** -0.5
    scale_c = DC ** -0.5
    heads = TN // HEAD_DIM

    def rope_b(y):
        return y * cb_ref[...] + pltpu.roll(y, HEAD_DIM // 2, 1) * sb_ref[...]

    def rope_c(y):
        lane = lax.broadcasted_iota(jnp.int32, y.shape, 1)
        first = (lane & (DC // 2)) == 0
        rot = jnp.where(first, pltpu.roll(y, HEAD_DIM - DC // 2, 1), pltpu.roll(y, DC // 2, 1))
        return y * cc_ref[...] + rot * sc_ref[...]

    def head(hh):
        return acc[:, hh * HEAD_DIM:(hh + 1) * HEAD_DIM]

    def put(hh, y):
        p_ref[:, hh * HEAD_DIM:(hh + 1) * HEAD_DIM] = y.astype(p_ref.dtype)

    t_aq = OFF_AQ // TN
    t_bq0, t_bq1 = OFF_BQ // TN, OFF_BK // TN
    t_bkv = OFF_BK // TN
    t_cq, t_ck = OFF_CQ // TN, OFF_CK // TN

    special = (j == t_aq) | ((j >= t_bq0) & (j <= t_bkv)) | (j == t_cq) | (j == t_ck)

    @pl.when((j < NP_TILES) & jnp.logical_not(special))
    def _():
        p_ref[...] = acc.astype(p_ref.dtype)

    @pl.when(j == t_aq)
    def _():
        p_ref[...] = (acc * scale_ab).astype(p_ref.dtype)

    @pl.when((j >= t_bq0) & (j < t_bq1))
    def _():
        for hh in range(heads):
            put(hh, rope_b(_head_rmsnorm(head(hh), qw_ref[...])) * scale_ab)

    @pl.when(j == t_bkv)
    def _():
        for hh in range(KV_B):
            put(hh, rope_b(_head_rmsnorm(head(hh), kw_ref[...])))
        for hh in range(KV_B, heads):
            put(hh, head(hh))

    @pl.when(j == t_cq)
    def _():
        for hh in range(heads):
            put(hh, rope_c(head(hh)) * scale_c)

    @pl.when(j == t_ck)
    def _():
        for hh in range(heads):
            put(hh, rope_c(head(hh)))

    @pl.when(j >= NP_TILES)
    def _():
        g_ref[...] = _sigmoid(acc + bg_ref[...]).astype(g_ref.dtype)


def _inproj_call(h, w_in, b_gate, qw, kw, tabs):
    m, d = h.shape
    n_in = w_in.shape[1]
    gate_w = n_in - P_W
    assert B_KW == TN // 2 and gate_w % TN == 0
    ltab = tabs[0].shape[0]
    tm = _row_tile(ltab, 1024)
    assert m % tm == 0
    nt = ltab // tm
    tab_spec = pl.BlockSpec((tm, HEAD_DIM), lambda i, j: (i % nt, 0))
    vec_spec = pl.BlockSpec((1, HEAD_DIM), lambda i, j: (0, 0))
    return pl.pallas_call(
        _inproj_kernel,
        grid=(m // tm, n_in // TN),
        in_specs=[
            pl.BlockSpec((tm, d), lambda i, j: (i, 0)),
            pl.BlockSpec((d, TN), lambda i, j: (0, j)),
            pl.BlockSpec((1, TN), lambda i, j: (0, jnp.maximum(j - NP_TILES, 0))),
            vec_spec, vec_spec, tab_spec, tab_spec, tab_spec, tab_spec,
        ],
        out_specs=[
            pl.BlockSpec((tm, TN), lambda i, j: (i, jnp.minimum(j, NP_TILES - 1))),
            pl.BlockSpec((tm, TN), lambda i, j: (i, jnp.maximum(j - NP_TILES, 0))),
        ],
        out_shape=[
            jax.ShapeDtypeStruct((m, P_W), BF16),
            jax.ShapeDtypeStruct((m, gate_w), BF16),
        ],
        compiler_params=_params(2),
        name="inproj",
    )(h, w_in, b_gate.reshape(1, gate_w), qw.reshape(1, HEAD_DIM), kw.reshape(1, HEAD_DIM), *tabs)


def _softmax2(s_a, s_b):
    m = jnp.maximum(jnp.max(s_a, axis=-1, keepdims=True), jnp.max(s_b, axis=-1, keepdims=True))
    e_a = jnp.exp(s_a - m)
    e_b = jnp.exp(s_b - m)
    return e_a, e_b, jnp.sum(e_a, axis=-1, keepdims=True) + jnp.sum(e_b, axis=-1, keepdims=True)


def _softmax1(s):
    m = jnp.max(s, axis=-1, keepdims=True)
    e = jnp.exp(s - m)
    return e, jnp.sum(e, axis=-1, keepdims=True)


def _silu_gate(o, z):
    z = z.astype(F32)
    return o * (z * _sigmoid(z))


def _split_maps(q):
    lane = lax.broadcasted_iota(jnp.int32, q.shape, 1)
    zero = jnp.zeros_like(q)
    return jnp.where(lane < DC, q, zero), jnp.where(lane >= DC, q, zero)


def _lam(lq1_ref, lk1_ref, lq2_ref, lk2_ref, li_ref):
    return (jnp.exp(jnp.sum(lq1_ref[...] * lk1_ref[...], axis=-1, keepdims=True))
            - jnp.exp(jnp.sum(lq2_ref[...] * lk2_ref[...], axis=-1, keepdims=True)) + li_ref[:, 0:1])


def _gqa_kernel(q_ref, kl_ref, vl_ref, kc_ref, vc_ref, z_ref, o_ref):
    q = q_ref[0]
    e_l, e_c, l = _softmax2(_dot_nt(q, kl_ref[0]), _dot_nt(q, kc_ref[0]))
    o = _dot(e_l.astype(BF16), vl_ref[0]) + _dot(e_c.astype(BF16), vc_ref[0])
    o_ref[0] = _silu_gate(o / l, z_ref[0]).astype(o_ref.dtype)


def _gqa_call(p, pc):
    b, length, _ = p.shape
    lc = pc.shape[1]
    tq = _row_tile(length, 512)
    rep = H_B // KV_B
    cq, ck, cv, cz = OFF_BQ // HEAD_DIM, OFF_BK // HEAD_DIM, OFF_BV // HEAD_DIM, OFF_BZ // HEAD_DIM
    return pl.pallas_call(
        _gqa_kernel,
        grid=(b, H_B, length // tq),
        in_specs=[
            pl.BlockSpec((1, tq, HEAD_DIM), lambda bi, h, i: (bi, i, cq + h)),
            pl.BlockSpec((1, length, HEAD_DIM), lambda bi, h, i: (bi, 0, ck + h // rep)),
            pl.BlockSpec((1, length, HEAD_DIM), lambda bi, h, i: (bi, 0, cv + h // rep)),
            pl.BlockSpec((1, lc, HEAD_DIM), lambda bi, h, i: (bi, 0, ck + h // rep)),
            pl.BlockSpec((1, lc, HEAD_DIM), lambda bi, h, i: (bi, 0, cv + h // rep)),
            pl.BlockSpec((1, tq, HEAD_DIM), lambda bi, h, i: (bi, i, cz + h)),
        ],
        out_specs=pl.BlockSpec((1, tq, HEAD_DIM), lambda bi, h, i: (bi, i, h)),
        out_shape=jax.ShapeDtypeStruct((b, length, B_QW), BF16),
        compiler_params=_params(3),
        name="gqa",
    )(p, p, p, pc, pc, p)


def _diff_core(q, k_sets, v_sets, lam, sw, li):
    q1, q2 = _split_maps(q)
    outs = []
    for qm in (q1, q2):
        s = [_dot_nt(qm, k) for k in k_sets]
        if len(s) == 2:
            e_a, e_b, l = _softmax2(s[0], s[1])
            outs.append(([e_a, e_b], l))
        else:
            e, l = _softmax1(s[0])
            outs.append(([e], l))
    (e1, l1), (e2, l2) = outs
    w1 = 1.0 / l1
    w2 = lam / l2
    o = None
    for ea, eb, v in zip(e1, e2, v_sets):
        a = (ea * w1 - eb * w2).astype(BF16)
        t = _dot(a, v)
        o = t if o is None else o + t
    o = _head_rmsnorm(o, sw)
    return o * (1.0 - li)


def _diff_kernel(q_ref, kl_ref, vl_ref, kc_ref, vc_ref, z_ref, lq1_ref, lk1_ref, lq2_ref, lk2_ref, li_ref, sw_ref,
                 o_ref):
    lam = _lam(lq1_ref, lk1_ref, lq2_ref, lk2_ref, li_ref)
    o = _diff_core(q_ref[0], [kl_ref[0], kc_ref[0]], [vl_ref[0], vc_ref[0]], lam, sw_ref[...], li_ref[...])
    o_ref[0] = _silu_gate(o, z_ref[0]).astype(o_ref.dtype)


def _diff_call(p, pc, lam_args, li, sw):
    b, length, _ = p.shape
    lc = pc.shape[1]
    tq = _row_tile(length, 256)
    cq, ck, cv, cz = OFF_CQ // HEAD_DIM, OFF_CK // HEAD_DIM, OFF_CV // HEAD_DIM, OFF_CZ // HEAD_DIM
    small = lambda w: pl.BlockSpec((1, w), lambda bi, h, i: (0, 0))
    return pl.pallas_call(
        _diff_kernel,
        grid=(b, H_C, length // tq),
        in_specs=[
            pl.BlockSpec((1, tq, HEAD_DIM), lambda bi, h, i: (bi, i, cq + h)),
            pl.BlockSpec((1, length, HEAD_DIM), lambda bi, h, i: (bi, 0, ck + h)),
            pl.BlockSpec((1, length, HEAD_DIM), lambda bi, h, i: (bi, 0, cv + h)),
            pl.BlockSpec((1, lc, HEAD_DIM), lambda bi, h, i: (bi, 0, ck + h)),
            pl.BlockSpec((1, lc, HEAD_DIM), lambda bi, h, i: (bi, 0, cv + h)),
            pl.BlockSpec((1, tq, HEAD_DIM), lambda bi, h, i: (bi, i, cz + h)),
            small(DC), small(DC), small(DC), small(DC), small(HEAD_DIM), small(HEAD_DIM),
        ],
        out_specs=pl.BlockSpec((1, tq, HEAD_DIM), lambda bi, h, i: (bi, i, h)),
        out_shape=jax.ShapeDtypeStruct((b, length, C_W), BF16),
        compiler_params=_params(3),
        name="diff",
    )(p, p, p, pc, pc, p, *lam_args, li, sw)


def _nbr_key_row0(t, rows):
    return jnp.clip(t * NBR_R - WIN_H // 2, 0, rows - NBR_KR)


def _nbr_kernel(q_ref, kl_ref, vl_ref, kc_ref, vc_ref, z_ref, bias_ref, o_ref, *, rows):
    t = pl.program_id(1)
    span = NBR_KR * GRID_W
    k_off = pl.multiple_of(_nbr_key_row0(t, rows) * GRID_W, NBR_R * GRID_W)
    for hh in range(H_A):
        cols = slice(hh * HEAD_DIM, (hh + 1) * HEAD_DIM)
        q = q_ref[0, :, cols]
        s_win = _dot_nt(q, kl_ref[0, pl.ds(k_off, span), cols]) + bias_ref[hh, 0]
        s_ctx = _dot_nt(q, kc_ref[0, :, cols])
        e_w, e_c, l = _softmax2(s_win, s_ctx)
        o = _dot(e_w.astype(BF16), vl_ref[0, pl.ds(k_off, span), cols]) + _dot(e_c.astype(BF16), vc_ref[0, :, cols])
        o_ref[0, :, cols] = _silu_gate(o / l, z_ref[0, :, cols]).astype(o_ref.dtype)


def _nbr_bias_index(rows):
    nt = rows // NBR_R
    kh = min(WIN_H, rows)
    i = np.arange(NBR_R)[:, None, None, None]
    qc = np.arange(GRID_W)[None, :, None, None]
    jr = np.arange(NBR_KR)[None, None, :, None]
    kc = np.arange(GRID_W)[None, None, None, :]
    out = []
    for t in (0, 1, nt - 1):
        r0 = NBR_R * t
        k0 = int(np.clip(r0 - WIN_H // 2, 0, rows - NBR_KR))
        rq = r0 + i
        rk = k0 + jr
        rs = np.clip(rq - kh // 2, 0, rows - kh)
        cs = np.clip(qc - WIN_W // 2, 0, GRID_W - WIN_W)
        valid = (rk >= rs) & (rk < rs + kh) & (kc >= cs) & (kc < cs + WIN_W)
        dr = np.clip(rk - rq + (WIN_H - 1), 0, 2 * WIN_H - 2)
        dc = np.clip(kc - qc, -(WIN_W - 1), WIN_W - 1) + (WIN_W - 1)
        shape = (NBR_R, GRID_W, NBR_KR, GRID_W)
        out.append((np.broadcast_to(dr, shape), np.broadcast_to(dc, shape), np.broadcast_to(valid, shape)))
    dr, dc, valid = (np.stack(a) for a in zip(*out))
    return dr, dc, valid


def _nbr_bias_tables(rpb_l, rows):
    dr, dc, valid = _nbr_bias_index(rows)
    bias = jnp.where(valid[None], rpb_l[:, dr, dc], NEG)
    return bias.reshape(H_A, 3, NBR_R * GRID_W, NBR_KR * GRID_W)


def _nbr_call(p, pc, bias):
    b, length, _ = p.shape
    lc = pc.shape[1]
    rows = length // GRID_W
    assert rows % NBR_R == 0 and rows >= NBR_KR
    tq = NBR_R * GRID_W
    nt = rows // NBR_R
    blk = lambda n, col: pl.BlockSpec((1, n, A_W), lambda bi, t: (bi, 0, col))
    tile = lambda col: pl.BlockSpec((1, tq, A_W), lambda bi, t: (bi, t, col))
    bias_type = lambda bi, t: (0, jnp.where(t == 0, 0, jnp.where(t == nt - 1, 2, 1)), 0, 0)
    return pl.pallas_call(
        functools.partial(_nbr_kernel, rows=rows),
        grid=(b, nt),
        in_specs=[
            tile(OFF_AQ // A_W), blk(length, OFF_AK // A_W), blk(length, OFF_AV // A_W),
            blk(lc, OFF_AK // A_W), blk(lc, OFF_AV // A_W), tile(OFF_AZ // A_W),
            pl.BlockSpec((H_A, 1, tq, NBR_KR * GRID_W), bias_type),
        ],
        out_specs=pl.BlockSpec((1, tq, A_W), lambda bi, t: (bi, t, 0)),
        out_shape=jax.ShapeDtypeStruct((b, length, A_W), BF16),
        compiler_params=_params(2),
        name="nbr",
    )(p, p, p, pc, pc, p, bias)


def _ctx_attn_kernel(p_ref, lq1_ref, lk1_ref, lq2_ref, lk2_ref, li_ref, sw_ref, oa_ref, ob_ref, oc_ref):
    def col(off, hh):
        return p_ref[0, :, off + hh * HEAD_DIM: off + (hh + 1) * HEAD_DIM]

    def plain(q, k, v):
        e, l = _softmax1(_dot_nt(q, k))
        return _dot(e.astype(BF16), v) / l

    for hh in range(H_A):
        o = plain(col(OFF_AQ, hh), col(OFF_AK, hh), col(OFF_AV, hh))
        oa_ref[0, :, hh * HEAD_DIM:(hh + 1) * HEAD_DIM] = _silu_gate(o, col(OFF_AZ, hh)).astype(oa_ref.dtype)
    rep = H_B // KV_B
    for hh in range(H_B):
        o = plain(col(OFF_BQ, hh), col(OFF_BK, hh // rep), col(OFF_BV, hh // rep))
        ob_ref[0, :, hh * HEAD_DIM:(hh + 1) * HEAD_DIM] = _silu_gate(o, col(OFF_BZ, hh)).astype(ob_ref.dtype)
    lam = _lam(lq1_ref, lk1_ref, lq2_ref, lk2_ref, li_ref)
    for hh in range(H_C):
        o = _diff_core(col(OFF_CQ, hh), [col(OFF_CK, hh)], [col(OFF_CV, hh)], lam, sw_ref[...], li_ref[...])
        oc_ref[0, :, hh * HEAD_DIM:(hh + 1) * HEAD_DIM] = _silu_gate(o, col(OFF_CZ, hh)).astype(oc_ref.dtype)


def _ctx_attn_call(pc, lam_args, li, sw):
    b, lc, _ = pc.shape
    small = lambda w: pl.BlockSpec((1, w), lambda bi: (0, 0))
    out = lambda w: pl.BlockSpec((1, lc, w), lambda bi: (bi, 0, 0))
    return pl.pallas_call(
        _ctx_attn_kernel,
        grid=(b,),
        in_specs=[pl.BlockSpec((1, lc, P_W), lambda bi: (bi, 0, 0)),
                  small(DC), small(DC), small(DC), small(DC), small(HEAD_DIM), small(HEAD_DIM)],
        out_specs=[out(A_W), out(B_QW), out(C_W)],
        out_shape=[jax.ShapeDtypeStruct((b, lc, A_W), BF16), jax.ShapeDtypeStruct((b, lc, B_QW), BF16),
                   jax.ShapeDtypeStruct((b, lc, C_W), BF16)],
        compiler_params=_params(1),
        name="ctx_attn",
    )(pc, *lam_args, li, sw)


def _merge_kernel(aa_ref, ab_ref, ac_ref, wa_ref, wb_ref, wc_ref, ga_ref, gb_ref, gc_ref, y_ref):
    y = (ga_ref[...].astype(F32) * _dot(aa_ref[...], wa_ref[...])
         + gb_ref[...].astype(F32) * _dot(ab_ref[...], wb_ref[...])
         + gc_ref[...].astype(F32) * _dot(ac_ref[...], wc_ref[...]))
    y_ref[...] = y.astype(y_ref.dtype)


def _merge_call(aa, ab, ac, wa, wb, wc, g):
    m = aa.shape[0]
    d = wa.shape[1]
    tm = _row_tile(m, 1024)
    nd = d // TN
    a_spec = lambda w: pl.BlockSpec((tm, w), lambda i, n: (i, 0))
    w_spec = lambda w: pl.BlockSpec((w, TN), lambda i, n: (0, n))
    g_spec = lambda k: pl.BlockSpec((tm, TN), lambda i, n: (i, k * nd + n))
    return pl.pallas_call(
        _merge_kernel,
        grid=(m // tm, nd),
        in_specs=[a_spec(A_W), a_spec(B_QW), a_spec(C_W), w_spec(A_W), w_spec(B_QW), w_spec(C_W),
                  g_spec(0), g_spec(1), g_spec(2)],
        out_specs=pl.BlockSpec((tm, TN), lambda i, n: (i, n)),
        out_shape=jax.ShapeDtypeStruct((m, d), BF16),
        compiler_params=_params(2),
        name="merge",
    )(aa, ab, ac, wa, wb, wc, g, g, g)


def _outproj_kernel(y_ref, w_ref, x_ref, mod_ref, o_ref):
    gate = mod_ref[0, 2:3, :]
    o_ref[0] = x_ref[0] + gate * _dot(y_ref[0], w_ref[...])


def _outproj_call(y, w_out, x, mod):
    b, length, d = x.shape
    tm = _row_tile(length, 512)
    per_batch = mod.shape[0] != 1
    row = lambda bi, i: (bi, i, 0)
    return pl.pallas_call(
        _outproj_kernel,
        grid=(b, length // tm),
        in_specs=[
            pl.BlockSpec((1, tm, d), row),
            pl.BlockSpec((d, d), lambda bi, i: (0, 0)),
            pl.BlockSpec((1, tm, d), row),
            pl.BlockSpec((1, 3, d), (lambda bi, i: (bi, 0, 0)) if per_batch else (lambda bi, i: (0, 0, 0))),
        ],
        out_specs=pl.BlockSpec((1, tm, d), row),
        out_shape=jax.ShapeDtypeStruct((b, length, d), F32),
        compiler_params=_params(2),
        name="outproj",
    )(y, w_out, x, mod)


def _rope_tables(length):
    pos = jnp.arange(length)
    row = (pos // GRID_W).astype(F32)[:, None]
    col = (pos % GRID_W).astype(F32)[:, None]

    def cos_sin(rot_dim):
        n = rot_dim // 4
        inv_freq = ROPE_THETA ** (-jnp.arange(n, dtype=F32) / n)
        ang = jnp.concatenate([row * inv_freq, col * inv_freq], axis=-1)
        return jnp.cos(ang), jnp.sin(ang)

    cb, sb = cos_sin(HEAD_DIM)
    cc, sc = cos_sin(DC)
    return (jnp.concatenate([cb, cb], axis=-1), jnp.concatenate([-sb, sb], axis=-1),
            jnp.concatenate([cc, cc, cc, cc], axis=-1), jnp.concatenate([-sc, sc, -sc, sc], axis=-1))


def _identity_tables(length):
    one = jnp.ones((length, HEAD_DIM), F32)
    zero = jnp.zeros((length, HEAD_DIM), F32)
    return one, zero, one, zero


def kernel(x, c, ctx, c_ctx, norm_w, w_ada, b_ada, w_in, b_gate, rpb, q_norm_w, k_norm_w, lam_q1, lam_k1, lam_q2,
           lam_k2, subln_w, w_bo_a, w_bo_b, w_bo_c, w_out, final_norm_w):
    b, length, d = x.shape
    lc = ctx.shape[1]
    depth = w_in.shape[0]
    rows = length // GRID_W

    n_cond = b + 1
    pad = (-n_cond) % 8
    cc = jnp.concatenate([c, c_ctx[None], jnp.zeros((pad, d), F32)], axis=0)
    mod_all = _ada_call(cc, w_ada, b_ada).reshape(depth, n_cond + pad, 3, d)

    tabs_lat = _rope_tables(length)
    tabs_ctx = _identity_tables(_row_tile(b * lc, 1024))

    w_in_bf = w_in.astype(BF16)
    wa_bf, wb_bf, wc_bf, wo_bf = (w.astype(BF16) for w in (w_bo_a, w_bo_b, w_bo_c, w_out))

    xl, xc = x, ctx
    for l in range(depth):
        update_ctx = l < depth - 1
        mod_l = mod_all[l, :b]
        mod_c = mod_all[l, b:b + 1]
        h = _norm_call(xl, norm_w[l], mod_l, BF16)
        hc = _norm_call(xc, norm_w[l], mod_c, BF16)
        p, g = _inproj_call(h.reshape(b * length, d), w_in_bf[l], b_gate[l], q_norm_w[l], k_norm_w[l], tabs_lat)
        pc, gc = _inproj_call(hc.reshape(b * lc, d), w_in_bf[l], b_gate[l], q_norm_w[l], k_norm_w[l], tabs_ctx)
        p = p.reshape(b, length, P_W)
        pc = pc.reshape(b, lc, P_W)

        lam_init = 0.8 - 0.6 * float(np.exp(-0.3 * l))
        li = jnp.full((1, HEAD_DIM), lam_init, F32)
        lam_args = [v[l].reshape(1, DC) for v in (lam_q1, lam_k1, lam_q2, lam_k2)]
        sw = subln_w[l].reshape(1, HEAD_DIM)

        o_a = _nbr_call(p, pc, _nbr_bias_tables(rpb[l], rows))
        o_b = _gqa_call(p, pc)
        o_c = _diff_call(p, pc, lam_args, li, sw)
        y = _merge_call(o_a.reshape(b * length, A_W), o_b.reshape(b * length, B_QW), o_c.reshape(b * length, C_W),
                        wa_bf[l], wb_bf[l], wc_bf[l], g)
        xl = _outproj_call(y.reshape(b, length, d), wo_bf[l], xl, mod_l)
        if update_ctx:
            oc_a, oc_b, oc_c = _ctx_attn_call(pc, lam_args, li, sw)
            yc = _merge_call(oc_a.reshape(b * lc, A_W), oc_b.reshape(b * lc, B_QW), oc_c.reshape(b * lc, C_W),
                             wa_bf[l], wb_bf[l], wc_bf[l], gc)
            xc = _outproj_call(yc.reshape(b, lc, d), wo_bf[l], xc, mod_c)
    return _norm_call(xl, final_norm_w, None, F32)
```

```python
import functools

import numpy as np
import jax
import jax.numpy as jnp
from jax import lax
from jax.experimental import pallas as pl
from jax.experimental.pallas import tpu as pltpu

F32 = jnp.float32
BF16 = jnp.bfloat16

GRID_W = 64
HEAD_DIM = 128
H_A = 4
WIN_H = 8
WIN_W = 16
H_B = 8
KV_B = 2
H_C = 4
DC = HEAD_DIM // 2
ROPE_THETA = 10000.0
EPS = 1e-6
NEG = -1e30

A_W = H_A * HEAD_DIM
B_QW = H_B * HEAD_DIM
B_KW = KV_B * HEAD_DIM
C_W = H_C * HEAD_DIM

OFF_AQ, OFF_AK, OFF_AV, OFF_AZ = 0, A_W, 2 * A_W, 3 * A_W
OFF_BQ = 4 * A_W
OFF_BK = OFF_BQ + B_QW
OFF_BV = OFF_BK + B_KW
OFF_BZ = OFF_BV + B_KW
OFF_CQ = OFF_BZ + B_QW
OFF_CK = OFF_CQ + C_W
OFF_CV = OFF_CK + C_W
OFF_CZ = OFF_CV + C_W
P_W = OFF_CZ + C_W

TN = 512
NP_TILES = P_W // TN

NBR_R = 4
NBR_KR = 12

VMEM_LIMIT = 56 * 1024 * 1024


def _params(n_axes):
    return pltpu.CompilerParams(dimension_semantics=("arbitrary",) * n_axes, vmem_limit_bytes=VMEM_LIMIT)


def _sigmoid(x):
    return 1.0 / (1.0 + jnp.exp(-x))


def _dot(a, b):
    return jnp.dot(a, b, preferred_element_type=F32)


def _dot_nt(a, b):
    return lax.dot_general(a, b, (((1,), (1,)), ((), ())), preferred_element_type=F32)


def _head_rmsnorm(y, w):
    return y * lax.rsqrt(jnp.mean(y * y, axis=-1, keepdims=True) + EPS) * w


def _ada_kernel(c_ref, w_ref, b_ref, o_ref):
    a = c_ref[...]
    a = a * _sigmoid(a)
    o_ref[0] = _dot(a.astype(BF16), w_ref[0].astype(BF16)) + b_ref[0]


def _ada_call(cc, w_ada, b_ada):
    depth, d, n3 = w_ada.shape
    rows = cc.shape[0]
    tn = 1536
    return pl.pallas_call(
        _ada_kernel,
        grid=(depth, n3 // tn),
        in_specs=[
            pl.BlockSpec((rows, d), lambda l, j: (0, 0)),
            pl.BlockSpec((1, d, tn), lambda l, j: (l, 0, j)),
            pl.BlockSpec((1, 1, tn), lambda l, j: (l, 0, j)),
        ],
        out_specs=pl.BlockSpec((1, rows, tn), lambda l, j: (l, 0, j)),
        out_shape=jax.ShapeDtypeStruct((depth, rows, n3), F32),
        compiler_params=_params(2),
        name="ada",
    )(cc, w_ada, b_ada.reshape(depth, 1, n3))


def _norm_mod_kernel(x_ref, w_ref, mod_ref, o_ref):
    x = x_ref[0]
    y = x * lax.rsqrt(jnp.mean(x * x, axis=-1, keepdims=True) + EPS)
    y = y * w_ref[...]
    shift = mod_ref[0, 0:1, :]
    scale = mod_ref[0, 1:2, :]
    o_ref[0] = (y * (1.0 + scale) + shift).astype(o_ref.dtype)


def _norm_kernel(x_ref, w_ref, o_ref):
    x = x_ref[0]
    y = x * lax.rsqrt(jnp.mean(x * x, axis=-1, keepdims=True) + EPS)
    o_ref[0] = (y * w_ref[...]).astype(o_ref.dtype)


def _row_tile(n, cap):
    t = min(n, cap)
    assert n % t == 0
    return t


def _norm_call(x, w, mod, out_dtype):
    b, length, d = x.shape
    tm = _row_tile(length, 512)
    in_specs = [
        pl.BlockSpec((1, tm, d), lambda bi, i: (bi, i, 0)),
        pl.BlockSpec((1, d), lambda bi, i: (0, 0)),
    ]
    args = [x, w.reshape(1, d)]
    if mod is not None:
        per_batch = mod.shape[0] != 1
        in_specs.append(pl.BlockSpec((1, 3, d), (lambda bi, i: (bi, 0, 0)) if per_batch else (lambda bi, i: (0, 0, 0))))
        args.append(mod)
    return pl.pallas_call(
        _norm_mod_kernel if mod is not None else _norm_kernel,
        grid=(b, length // tm),
        in_specs=in_specs,
        out_specs=pl.BlockSpec((1, tm, d), lambda bi, i: (bi, i, 0)),
        out_shape=jax.ShapeDtypeStruct((b, length, d), out_dtype),
        compiler_params=_params(2),
        name="norm",
    )(*args)


def _inproj_kernel(h_ref, w_ref, bg_ref, qw_ref, kw_ref, cb_ref, sb_ref, cc_ref, sc_ref, p_ref, g_ref):
    j = pl.program_id(1)
    acc = _dot(h_ref[...], w_ref[...])
    scale_ab = HEAD_DIM ** -0.5
    scale_c = DC ** -0.5
    heads = TN // HEAD_DIM

    def rope_b(y):
        return y * cb_ref[...] + pltpu.roll(y, HEAD_DIM // 2, 1) * sb_ref[...]

    def rope_c(y):
        lane = lax.broadcasted_iota(jnp.int32, y.shape, 1)
        first = (lane & (DC // 2)) == 0
        rot = jnp.where(first, pltpu.roll(y, HEAD_DIM - DC // 2, 1), pltpu.roll(y, DC // 2, 1))
        return y * cc_ref[...] + rot * sc_ref[...]

    def head(hh):
        return acc[:, hh * HEAD_DIM:(hh + 1) * HEAD_DIM]

    def put(hh, y):
        p_ref[:, hh * HEAD_DIM:(hh + 1) * HEAD_DIM] = y.astype(p_ref.dtype)

    t_aq = OFF_AQ // TN
    t_bq0, t_bq1 = OFF_BQ // TN, OFF_BK // TN
    t_bkv = OFF_BK // TN
    t_cq, t_ck = OFF_CQ // TN, OFF_CK // TN

    special = (j == t_aq) | ((j >= t_bq0) & (j <= t_bkv)) | (j == t_cq) | (j == t_ck)

    @pl.when((j < NP_TILES) & jnp.logical_not(special))
    def _():
        p_ref[...] = acc.astype(p_ref.dtype)

    @pl.when(j == t_aq)
    def _():
        p_ref[...] = (acc * scale_ab).astype(p_ref.dtype)

    @pl.when((j >= t_bq0) & (j < t_bq1))
    def _():
        for hh in range(heads):
            put(hh, rope_b(_head_rmsnorm(head(hh), qw_ref[...])) * scale_ab)

    @pl.when(j == t_bkv)
    def _():
        for hh in range(KV_B):
            put(hh, rope_b(_head_rmsnorm(head(hh), kw_ref[...])))
        for hh in range(KV_B, heads):
            put(hh, head(hh))

    @pl.when(j == t_cq)
    def _():
        for hh in range(heads):
            put(hh, rope_c(head(hh)) * scale_c)

    @pl.when(j == t_ck)
    def _():
        for hh in range(heads):
            put(hh, rope_c(head(hh)))

    @pl.when(j >= NP_TILES)
    def _():
        g_ref[...] = _sigmoid(acc + bg_ref[...]).astype(g_ref.dtype)


def _inproj_call(h, w_in, b_gate, qw, kw, tabs):
    m, d = h.shape
    n_in = w_in.shape[1]
    gate_w = n_in - P_W
    assert B_KW == TN // 2 and gate_w % TN == 0
    ltab = tabs[0].shape[0]
    tm = _row_tile(ltab, 1024)
    assert m % tm == 0
    nt = ltab // tm
    tab_spec = pl.BlockSpec((tm, HEAD_DIM), lambda i, j: (i % nt, 0))
    vec_spec = pl.BlockSpec((1, HEAD_DIM), lambda i, j: (0, 0))
    return pl.pallas_call(
        _inproj_kernel,
        grid=(m // tm, n_in // TN),
        in_specs=[
            pl.BlockSpec((tm, d), lambda i, j: (i, 0)),
            pl.BlockSpec((d, TN), lambda i, j: (0, j)),
            pl.BlockSpec((1, TN), lambda i, j: (0, jnp.maximum(j - NP_TILES, 0))),
            vec_spec, vec_spec, tab_spec, tab_spec, tab_spec, tab_spec,
        ],
        out_specs=[
            pl.BlockSpec((tm, TN), lambda i, j: (i, jnp.minimum(j, NP_TILES - 1))),
            pl.BlockSpec((tm, TN), lambda i, j: (i, jnp.maximum(j - NP_TILES, 0))),
        ],
        out_shape=[
            jax.ShapeDtypeStruct((m, P_W), BF16),
            jax.ShapeDtypeStruct((m, gate_w), BF16),
        ],
        compiler_params=_params(2),
        name="inproj",
    )(h, w_in, b_gate.reshape(1, gate_w), qw.reshape(1, HEAD_DIM), kw.reshape(1, HEAD_DIM), *tabs)


def _softmax2(s_a, s_b):
    m = jnp.maximum(jnp.max(s_a, axis=-1, keepdims=True), jnp.max(s_b, axis=-1, keepdims=True))
    e_a = jnp.exp(s_a - m)
    e_b = jnp.exp(s_b - m)
    return e_a, e_b, jnp.sum(e_a, axis=-1, keepdims=True) + jnp.sum(e_b, axis=-1, keepdims=True)


def _softmax1(s):
    m = jnp.max(s, axis=-1, keepdims=True)
    e = jnp.exp(s - m)
    return e, jnp.sum(e, axis=-1, keepdims=True)


def _silu_gate(o, z):
    z = z.astype(F32)
    return o * (z * _sigmoid(z))


def _split_maps(q):
    lane = lax.broadcasted_iota(jnp.int32, q.shape, 1)
    zero = jnp.zeros_like(q)
    return jnp.where(lane < DC, q, zero), jnp.where(lane >= DC, q, zero)


def _lam(lq1_ref, lk1_ref, lq2_ref, lk2_ref, li_ref):
    return (jnp.exp(jnp.sum(lq1_ref[...] * lk1_ref[...], axis=-1, keepdims=True))
            - jnp.exp(jnp.sum(lq2_ref[...] * lk2_ref[...], axis=-1, keepdims=True)) + li_ref[:, 0:1])


def _gqa_kernel(q_ref, kl_ref, vl_ref, kc_ref, vc_ref, z_ref, o_ref):
    q = q_ref[0]
    e_l, e_c, l = _softmax2(_dot_nt(q, kl_ref[0]), _dot_nt(q, kc_ref[0]))
    o = _dot(e_l.astype(BF16), vl_ref[0]) + _dot(e_c.astype(BF16), vc_ref[0])
    o_ref[0] = _silu_gate(o / l, z_ref[0]).astype(o_ref.dtype)


def _gqa_call(p, pc):
    b, length, _ = p.shape
    lc = pc.shape[1]
    tq = _row_tile(length, 512)
    rep = H_B // KV_B
    cq, ck, cv, cz = OFF_BQ // HEAD_DIM, OFF_BK // HEAD_DIM, OFF_BV // HEAD_DIM, OFF_BZ // HEAD_DIM
    return pl.pallas_call(
        _gqa_kernel,
        grid=(b, H_B, length // tq),
        in_specs=[
            pl.BlockSpec((1, tq, HEAD_DIM), lambda bi, h, i: (bi, i, cq + h)),
            pl.BlockSpec((1, length, HEAD_DIM), lambda bi, h, i: (bi, 0, ck + h // rep)),
            pl.BlockSpec((1, length, HEAD_DIM), lambda bi, h, i: (bi, 0, cv + h // rep)),
            pl.BlockSpec((1, lc, HEAD_DIM), lambda bi, h, i: (bi, 0, ck + h // rep)),
            pl.BlockSpec((1, lc, HEAD_DIM), lambda bi, h, i: (bi, 0, cv + h // rep)),
            pl.BlockSpec((1, tq, HEAD_DIM), lambda bi, h, i: (bi, i, cz + h)),
        ],
        out_specs=pl.BlockSpec((1, tq, HEAD_DIM), lambda bi, h, i: (bi, i, h)),
        out_shape=jax.ShapeDtypeStruct((b, length, B_QW), BF16),
        compiler_params=_params(3),
        name="gqa",
    )(p, p, p, pc, pc, p)


def _diff_core(q, k_sets, v_sets, lam, sw, li):
    q1, q2 = _split_maps(q)
    outs = []
    for qm in (q1, q2):
        s = [_dot_nt(qm, k) for k in k_sets]
        if len(s) == 2:
            e_a, e_b, l = _softmax2(s[0], s[1])
            outs.append(([e_a, e_b], l))
        else:
            e, l = _softmax1(s[0])
            outs.append(([e], l))
    (e1, l1), (e2, l2) = outs
    w1 = 1.0 / l1
    w2 = lam / l2
    o = None
    for ea, eb, v in zip(e1, e2, v_sets):
        a = (ea * w1 - eb * w2).astype(BF16)
        t = _dot(a, v)
        o = t if o is None else o + t
    o = _head_rmsnorm(o, sw)
    return o * (1.0 - li)


def _diff_kernel(q_ref, kl_ref, vl_ref, kc_ref, vc_ref, z_ref, lq1_ref, lk1_ref, lq2_ref, lk2_ref, li_ref, sw_ref,
                 o_ref):
    lam = _lam(lq1_ref, lk1_ref, lq2_ref, lk2_ref, li_ref)
    o = _diff_core(q_ref[0], [kl_ref[0], kc_ref[0]], [vl_ref[0], vc_ref[0]], lam, sw_ref[...], li_ref[...])
    o_ref[0] = _silu_gate(o, z_ref[0]).astype(o_ref.dtype)


def _diff_call(p, pc, lam_args, li, sw):
    b, length, _ = p.shape
    lc = pc.shape[1]
    tq = _row_tile(length, 256)
    cq, ck, cv, cz = OFF_CQ // HEAD_DIM, OFF_CK // HEAD_DIM, OFF_CV // HEAD_DIM, OFF_CZ // HEAD_DIM
    small = lambda w: pl.BlockSpec((1, w), lambda bi, h, i: (0, 0))
    return pl.pallas_call(
        _diff_kernel,
        grid=(b, H_C, length // tq),
        in_specs=[
            pl.BlockSpec((1, tq, HEAD_DIM), lambda bi, h, i: (bi, i, cq + h)),
            pl.BlockSpec((1, length, HEAD_DIM), lambda bi, h, i: (bi, 0, ck + h)),
            pl.BlockSpec((1, length, HEAD_DIM), lambda bi, h, i: (bi, 0, cv + h)),
            pl.BlockSpec((1, lc, HEAD_DIM), lambda bi, h, i: (bi, 0, ck + h)),
            pl.BlockSpec((1, lc, HEAD_DIM), lambda bi, h, i: (bi, 0, cv + h)),
            pl.BlockSpec((1, tq, HEAD_DIM), lambda bi, h, i: (bi, i, cz + h)),
            small(DC), small(DC), small(DC), small(DC), small(HEAD_DIM), small(HEAD_DIM),
        ],
        out_specs=pl.BlockSpec((1, tq, HEAD_DIM), lambda bi, h, i: (bi, i, h)),
        out_shape=jax.ShapeDtypeStruct((b, length, C_W), BF16),
        compiler_params=_params(3),
        name="diff",
    )(p, p, p, pc, pc, p, *lam_args, li, sw)


def _nbr_key_row0(t, rows):
    return jnp.clip(t * NBR_R - WIN_H // 2, 0, rows - NBR_KR)


def _nbr_kernel(q_ref, kl_ref, vl_ref, kc_ref, vc_ref, z_ref, bias_ref, o_ref, *, rows):
    t = pl.program_id(1)
    span = NBR_KR * GRID_W
    k_off = pl.multiple_of(_nbr_key_row0(t, rows) * GRID_W, NBR_R * GRID_W)
    for hh in range(H_A):
        cols = slice(hh * HEAD_DIM, (hh + 1) * HEAD_DIM)
        q = q_ref[0, :, cols]
        s_win = _dot_nt(q, kl_ref[0, pl.ds(k_off, span), cols]) + bias_ref[hh, 0]
        s_ctx = _dot_nt(q, kc_ref[0, :, cols])
        e_w, e_c, l = _softmax2(s_win, s_ctx)
        o = _dot(e_w.astype(BF16), vl_ref[0, pl.ds(k_off, span), cols]) + _dot(e_c.astype(BF16), vc_ref[0, :, cols])
        o_ref[0, :, cols] = _silu_gate(o / l, z_ref[0, :, cols]).astype(o_ref.dtype)


def _nbr_bias_selectors(rows):
    nt = rows // NBR_R
    kh = min(WIN_H, rows)
    n_dr, n_dc = 2 * WIN_H - 1, 2 * WIN_W - 1
    qc = np.arange(GRID_W)[:, None]
    kc = np.arange(GRID_W)[None, :]
    dc = np.clip(kc - qc, -(WIN_W - 1), WIN_W - 1) + (WIN_W - 1)
    cs = np.clip(qc - WIN_W // 2, 0, GRID_W - WIN_W)
    in_win = (kc >= cs) & (kc < cs + WIN_W)
    sel_dc = (dc[None] == np.arange(n_dc)[:, None, None]).astype(np.float32)
    i = np.arange(NBR_R)[:, None]
    jr = np.arange(NBR_KR)[None, :]
    sel_dr, in_band = [], []
    for t in (0, 1, nt - 1):
        r0 = NBR_R * t
        k0 = int(np.clip(r0 - WIN_H // 2, 0, rows - NBR_KR))
        rq = r0 + i
        rk = k0 + jr
        rs = np.clip(rq - kh // 2, 0, rows - kh)
        band = (rk >= rs) & (rk < rs + kh)
        dr = np.clip(rk - rq + (WIN_H - 1), 0, n_dr - 1)
        sel_dr.append((dr[..., None] == np.arange(n_dr)) & band[..., None])
        in_band.append(band)
    sel_dr = np.stack(sel_dr).astype(np.float32)
    valid = np.stack(in_band)[:, :, None, :, None] & in_win[None, None, :, None, :]
    return sel_dc, sel_dr, valid


def _nbr_bias_tables(rpb, rows):
    sel_dc, sel_dr, valid = _nbr_bias_selectors(rows)
    by_col = jnp.einsum("lhrd,dqk->lhrqk", rpb, sel_dc, precision=lax.Precision.HIGHEST)
    bias = jnp.einsum("tijr,lhrqk->lhtiqjk", sel_dr, by_col, precision=lax.Precision.HIGHEST)
    bias = jnp.where(valid[None, None], bias, NEG)
    return bias.reshape(rpb.shape[0], H_A, 3, NBR_R * GRID_W, NBR_KR * GRID_W)


def _nbr_call(p, pc, bias):
    b, length, _ = p.shape
    lc = pc.shape[1]
    rows = length // GRID_W
    assert rows % NBR_R == 0 and rows >= NBR_KR
    tq = NBR_R * GRID_W
    nt = rows // NBR_R
    blk = lambda n, col: pl.BlockSpec((1, n, A_W), lambda bi, t: (bi, 0, col))
    tile = lambda col: pl.BlockSpec((1, tq, A_W), lambda bi, t: (bi, t, col))
    bias_type = lambda bi, t: (0, jnp.where(t == 0, 0, jnp.where(t == nt - 1, 2, 1)), 0, 0)
    return pl.pallas_call(
        functools.partial(_nbr_kernel, rows=rows),
        grid=(b, nt),
        in_specs=[
            tile(OFF_AQ // A_W), blk(length, OFF_AK // A_W), blk(length, OFF_AV // A_W),
            blk(lc, OFF_AK // A_W), blk(lc, OFF_AV // A_W), tile(OFF_AZ // A_W),
            pl.BlockSpec((H_A, 1, tq, NBR_KR * GRID_W), bias_type),
        ],
        out_specs=pl.BlockSpec((1, tq, A_W), lambda bi, t: (bi, t, 0)),
        out_shape=jax.ShapeDtypeStruct((b, length, A_W), BF16),
        compiler_params=_params(2),
        name="nbr",
    )(p, p, p, pc, pc, p, bias)


def _ctx_attn_kernel(p_ref, lq1_ref, lk1_ref, lq2_ref, lk2_ref, li_ref, sw_ref, oa_ref, ob_ref, oc_ref):
    def col(off, hh):
        return p_ref[0, :, off + hh * HEAD_DIM: off + (hh + 1) * HEAD_DIM]

    def plain(q, k, v):
        e, l = _softmax1(_dot_nt(q, k))
        return _dot(e.astype(BF16), v) / l

    for hh in range(H_A):
        o = plain(col(OFF_AQ, hh), col(OFF_AK, hh), col(OFF_AV, hh))
        oa_ref[0, :, hh * HEAD_DIM:(hh + 1) * HEAD_DIM] = _silu_gate(o, col(OFF_AZ, hh)).astype(oa_ref.dtype)
    rep = H_B // KV_B
    for hh in range(H_B):
        o = plain(col(OFF_BQ, hh), col(OFF_BK, hh // rep), col(OFF_BV, hh // rep))
        ob_ref[0, :, hh * HEAD_DIM:(hh + 1) * HEAD_DIM] = _silu_gate(o, col(OFF_BZ, hh)).astype(ob_ref.dtype)
    lam = _lam(lq1_ref, lk1_ref, lq2_ref, lk2_ref, li_ref)
    for hh in range(H_C):
        o = _diff_core(col(OFF_CQ, hh), [col(OFF_CK, hh)], [col(OFF_CV, hh)], lam, sw_ref[...], li_ref[...])
        oc_ref[0, :, hh * HEAD_DIM:(hh + 1) * HEAD_DIM] = _silu_gate(o, col(OFF_CZ, hh)).astype(oc_ref.dtype)


def _ctx_attn_call(pc, lam_args, li, sw):
    b, lc, _ = pc.shape
    small = lambda w: pl.BlockSpec((1, w), lambda bi: (0, 0))
    out = lambda w: pl.BlockSpec((1, lc, w), lambda bi: (bi, 0, 0))
    return pl.pallas_call(
        _ctx_attn_kernel,
        grid=(b,),
        in_specs=[pl.BlockSpec((1, lc, P_W), lambda bi: (bi, 0, 0)),
                  small(DC), small(DC), small(DC), small(DC), small(HEAD_DIM), small(HEAD_DIM)],
        out_specs=[out(A_W), out(B_QW), out(C_W)],
        out_shape=[jax.ShapeDtypeStruct((b, lc, A_W), BF16), jax.ShapeDtypeStruct((b, lc, B_QW), BF16),
                   jax.ShapeDtypeStruct((b, lc, C_W), BF16)],
        compiler_params=_params(1),
        name="ctx_attn",
    )(pc, *lam_args, li, sw)


def _merge_kernel(aa_ref, ab_ref, ac_ref, wa_ref, wb_ref, wc_ref, ga_ref, gb_ref, gc_ref, y_ref):
    y = (ga_ref[...].astype(F32) * _dot(aa_ref[...], wa_ref[...])
         + gb_ref[...].astype(F32) * _dot(ab_ref[...], wb_ref[...])
         + gc_ref[...].astype(F32) * _dot(ac_ref[...], wc_ref[...]))
    y_ref[...] = y.astype(y_ref.dtype)


def _merge_call(aa, ab, ac, wa, wb, wc, g):
    m = aa.shape[0]
    d = wa.shape[1]
    tm = _row_tile(m, 1024)
    nd = d // TN
    a_spec = lambda w: pl.BlockSpec((tm, w), lambda i, n: (i, 0))
    w_spec = lambda w: pl.BlockSpec((w, TN), lambda i, n: (0, n))
    g_spec = lambda k: pl.BlockSpec((tm, TN), lambda i, n: (i, k * nd + n))
    return pl.pallas_call(
        _merge_kernel,
        grid=(m // tm, nd),
        in_specs=[a_spec(A_W), a_spec(B_QW), a_spec(C_W), w_spec(A_W), w_spec(B_QW), w_spec(C_W),
                  g_spec(0), g_spec(1), g_spec(2)],
        out_specs=pl.BlockSpec((tm, TN), lambda i, n: (i, n)),
        out_shape=jax.ShapeDtypeStruct((m, d), BF16),
        compiler_params=_params(2),
        name="merge",
    )(aa, ab, ac, wa, wb, wc, g, g, g)


def _outproj_kernel(y_ref, w_ref, x_ref, mod_ref, o_ref):
    gate = mod_ref[0, 2:3, :]
    o_ref[0] = x_ref[0] + gate * _dot(y_ref[0], w_ref[...])


def _outproj_call(y, w_out, x, mod):
    b, length, d = x.shape
    tm = _row_tile(length, 512)
    per_batch = mod.shape[0] != 1
    row = lambda bi, i: (bi, i, 0)
    return pl.pallas_call(
        _outproj_kernel,
        grid=(b, length // tm),
        in_specs=[
            pl.BlockSpec((1, tm, d), row),
            pl.BlockSpec((d, d), lambda bi, i: (0, 0)),
            pl.BlockSpec((1, tm, d), row),
            pl.BlockSpec((1, 3, d), (lambda bi, i: (bi, 0, 0)) if per_batch else (lambda bi, i: (0, 0, 0))),
        ],
        out_specs=pl.BlockSpec((1, tm, d), row),
        out_shape=jax.ShapeDtypeStruct((b, length, d), F32),
        compiler_params=_params(2),
        name="outproj",
    )(y, w_out, x, mod)


def _rope_tables(length):
    pos = jnp.arange(length)
    row = (pos // GRID_W).astype(F32)[:, None]
    col = (pos % GRID_W).astype(F32)[:, None]

    def cos_sin(rot_dim):
        n = rot_dim // 4
        inv_freq = ROPE_THETA ** (-jnp.arange(n, dtype=F32) / n)
        ang = jnp.concatenate([row * inv_freq, col * inv_freq], axis=-1)
        return jnp.cos(ang), jnp.sin(ang)

    cb, sb = cos_sin(HEAD_DIM)
    cc, sc = cos_sin(DC)
    return (jnp.concatenate([cb, cb], axis=-1), jnp.concatenate([-sb, sb], axis=-1),
            jnp.concatenate([cc, cc, cc, cc], axis=-1), jnp.concatenate([-sc, sc, -sc, sc], axis=-1))


def _identity_tables(length):
    one = jnp.ones((length, HEAD_DIM), F32)
    zero = jnp.zeros((length, HEAD_DIM), F32)
    return one, zero, one, zero


def kernel(x, c, ctx, c_ctx, norm_w, w_ada, b_ada, w_in, b_gate, rpb, q_norm_w, k_norm_w, lam_q1, lam_k1, lam_q2,
           lam_k2, subln_w, w_bo_a, w_bo_b, w_bo_c, w_out, final_norm_w):
    b, length, d = x.shape
    lc = ctx.shape[1]
    depth = w_in.shape[0]
    rows = length // GRID_W

    n_cond = b + 1
    pad = (-n_cond) % 8
    cc = jnp.concatenate([c, c_ctx[None], jnp.zeros((pad, d), F32)], axis=0)
    mod_all = _ada_call(cc, w_ada, b_ada).reshape(depth, n_cond + pad, 3, d)

    tabs_lat = _rope_tables(length)
    tabs_ctx = _identity_tables(_row_tile(b * lc, 1024))
    nbr_bias = _nbr_bias_tables(rpb, rows)

    w_in_bf = w_in.astype(BF16)
    wa_bf, wb_bf, wc_bf, wo_bf = (w.astype(BF16) for w in (w_bo_a, w_bo_b, w_bo_c, w_out))

    xl, xc = x, ctx
    for l in range(depth):
        update_ctx = l < depth - 1
        mod_l = mod_all[l, :b]
        mod_c = mod_all[l, b:b + 1]
        h = _norm_call(xl, norm_w[l], mod_l, BF16)
        hc = _norm_call(xc, norm_w[l], mod_c, BF16)
        p, g = _inproj_call(h.reshape(b * length, d), w_in_bf[l], b_gate[l], q_norm_w[l], k_norm_w[l], tabs_lat)
        pc, gc = _inproj_call(hc.reshape(b * lc, d), w_in_bf[l], b_gate[l], q_norm_w[l], k_norm_w[l], tabs_ctx)
        p = p.reshape(b, length, P_W)
        pc = pc.reshape(b, lc, P_W)

        lam_init = 0.8 - 0.6 * float(np.exp(-0.3 * l))
        li = jnp.full((1, HEAD_DIM), lam_init, F32)
        lam_args = [v[l].reshape(1, DC) for v in (lam_q1, lam_k1, lam_q2, lam_k2)]
        sw = subln_w[l].reshape(1, HEAD_DIM)

        o_a = _nbr_call(p, pc, nbr_bias[l])
        o_b = _gqa_call(p, pc)
        o_c = _diff_call(p, pc, lam_args, li, sw)
        y = _merge_call(o_a.reshape(b * length, A_W), o_b.reshape(b * length, B_QW), o_c.reshape(b * length, C_W),
                        wa_bf[l], wb_bf[l], wc_bf[l], g)
        xl = _outproj_call(y.reshape(b, length, d), wo_bf[l], xl, mod_l)
        if update_ctx:
            oc_a, oc_b, oc_c = _ctx_attn_call(pc, lam_args, li, sw)
            yc = _merge_call(oc_a.reshape(b * lc, A_W), oc_b.reshape(b * lc, B_QW), oc_c.reshape(b * lc, C_W),
                             wa_bf[l], wb_bf[l], wc_bf[l], gc)
            xc = _outproj_call(yc.reshape(b, lc, d), wo_bf[l], xc, mod_c)
    return _norm_call(xl, final_norm_w, None, F32)
```

```python
import functools

import numpy as np
import jax
import jax.numpy as jnp
from jax import lax
from jax.experimental import pallas as pl
from jax.experimental.pallas import tpu as pltpu

F32 = jnp.float32
BF16 = jnp.bfloat16

GRID_W = 64
HEAD_DIM = 128
H_A = 4
WIN_H = 8
WIN_W = 16
H_B = 8
KV_B = 2
H_C = 4
DC = HEAD_DIM // 2
ROPE_THETA = 10000.0
EPS = 1e-6
NEG = -1e30
LOG2E = 1.4426950408889634

A_W = H_A * HEAD_DIM
B_QW = H_B * HEAD_DIM
B_KW = KV_B * HEAD_DIM
C_W = H_C * HEAD_DIM

OFF_AQ, OFF_AK, OFF_AV, OFF_AZ = 0, A_W, 2 * A_W, 3 * A_W
OFF_BQ = 4 * A_W
OFF_BK = OFF_BQ + B_QW
OFF_BV = OFF_BK + B_KW
OFF_BZ = OFF_BV + B_KW
OFF_CQ = OFF_BZ + B_QW
OFF_CK = OFF_CQ + C_W
OFF_CV = OFF_CK + C_W
OFF_CZ = OFF_CV + C_W
P_W = OFF_CZ + C_W

TN = 512
NP_TILES = P_W // TN

NBR_R = 4
NBR_KR = 12

VMEM_LIMIT = 56 * 1024 * 1024


def _params(n_axes):
    return pltpu.CompilerParams(dimension_semantics=("arbitrary",) * n_axes, vmem_limit_bytes=VMEM_LIMIT)


def _sigmoid(x):
    return 1.0 / (1.0 + jnp.exp(-x))


def _dot(a, b):
    return jnp.dot(a, b, preferred_element_type=F32)


def _dot_nt(a, b):
    return lax.dot_general(a, b, (((1,), (1,)), ((), ())), preferred_element_type=F32)


def _head_rmsnorm(y, w):
    return y * lax.rsqrt(jnp.mean(y * y, axis=-1, keepdims=True) + EPS) * w


def _ada_kernel(c_ref, w_ref, b_ref, o_ref):
    a = c_ref[...]
    a = a * _sigmoid(a)
    o_ref[0] = _dot(a.astype(BF16), w_ref[0].astype(BF16)) + b_ref[0]


def _ada_call(cc, w_ada, b_ada):
    depth, d, n3 = w_ada.shape
    rows = cc.shape[0]
    tn = 1536
    return pl.pallas_call(
        _ada_kernel,
        grid=(depth, n3 // tn),
        in_specs=[
            pl.BlockSpec((rows, d), lambda l, j: (0, 0)),
            pl.BlockSpec((1, d, tn), lambda l, j: (l, 0, j)),
            pl.BlockSpec((1, 1, tn), lambda l, j: (l, 0, j)),
        ],
        out_specs=pl.BlockSpec((1, rows, tn), lambda l, j: (l, 0, j)),
        out_shape=jax.ShapeDtypeStruct((depth, rows, n3), F32),
        compiler_params=_params(2),
        name="ada",
    )(cc, w_ada, b_ada.reshape(depth, 1, n3))


def _norm_mod_kernel(x_ref, w_ref, mod_ref, o_ref):
    x = x_ref[0]
    y = x * lax.rsqrt(jnp.mean(x * x, axis=-1, keepdims=True) + EPS)
    y = y * w_ref[...]
    shift = mod_ref[0, 0:1, :]
    scale = mod_ref[0, 1:2, :]
    o_ref[0] = (y * (1.0 + scale) + shift).astype(o_ref.dtype)


def _norm_kernel(x_ref, w_ref, o_ref):
    x = x_ref[0]
    y = x * lax.rsqrt(jnp.mean(x * x, axis=-1, keepdims=True) + EPS)
    o_ref[0] = (y * w_ref[...]).astype(o_ref.dtype)


def _row_tile(n, cap):
    t = min(n, cap)
    assert n % t == 0
    return t


def _norm_call(x, w, mod, out_dtype):
    b, length, d = x.shape
    tm = _row_tile(length, 512)
    in_specs = [
        pl.BlockSpec((1, tm, d), lambda bi, i: (bi, i, 0)),
        pl.BlockSpec((1, d), lambda bi, i: (0, 0)),
    ]
    args = [x, w.reshape(1, d)]
    if mod is not None:
        per_batch = mod.shape[0] != 1
        in_specs.append(pl.BlockSpec((1, 3, d), (lambda bi, i: (bi, 0, 0)) if per_batch else (lambda bi, i: (0, 0, 0))))
        args.append(mod)
    return pl.pallas_call(
        _norm_mod_kernel if mod is not None else _norm_kernel,
        grid=(b, length // tm),
        in_specs=in_specs,
        out_specs=pl.BlockSpec((1, tm, d), lambda bi, i: (bi, i, 0)),
        out_shape=jax.ShapeDtypeStruct((b, length, d), out_dtype),
        compiler_params=_params(2),
        name="norm",
    )(*args)


def _inproj_kernel(h_ref, w_ref, bg_ref, qw_ref, kw_ref, cb_ref, sb_ref, cc_ref, sc_ref, p_ref, g_ref):
    j = pl.program_id(1)
    acc = _dot(h_ref[...], w_ref[...])
    scale_a = HEAD_DIM ** -0.5
    scale_b = HEAD_DIM ** -0.5 * LOG2E
    scale_c = DC ** -0.5 * LOG2E
    heads = TN // HEAD_DIM

    def rope_b(y):
        return y * cb_ref[...] + pltpu.roll(y, HEAD_DIM // 2, 1) * sb_ref[...]

    def rope_c(y):
        lane = lax.broadcasted_iota(jnp.int32, y.shape, 1)
        first = (lane & (DC // 2)) == 0
        rot = jnp.where(first, pltpu.roll(y, HEAD_DIM - DC // 2, 1), pltpu.roll(y, DC // 2, 1))
        return y * cc_ref[...] + rot * sc_ref[...]

    def head(hh):
        return acc[:, hh * HEAD_DIM:(hh + 1) * HEAD_DIM]

    def put(hh, y):
        p_ref[:, hh * HEAD_DIM:(hh + 1) * HEAD_DIM] = y.astype(p_ref.dtype)

    t_aq = OFF_AQ // TN
    t_bq0, t_bq1 = OFF_BQ // TN, OFF_BK // TN
    t_bkv = OFF_BK // TN
    t_cq, t_ck = OFF_CQ // TN, OFF_CK // TN

    special = (j == t_aq) | ((j >= t_bq0) & (j <= t_bkv)) | (j == t_cq) | (j == t_ck)

    @pl.when((j < NP_TILES) & jnp.logical_not(special))
    def _():
        p_ref[...] = acc.astype(p_ref.dtype)

    @pl.when(j == t_aq)
    def _():
        p_ref[...] = (acc * scale_a).astype(p_ref.dtype)

    @pl.when((j >= t_bq0) & (j < t_bq1))
    def _():
        for hh in range(heads):
            put(hh, rope_b(_head_rmsnorm(head(hh), qw_ref[...])) * scale_b)

    @pl.when(j == t_bkv)
    def _():
        for hh in range(KV_B):
            put(hh, rope_b(_head_rmsnorm(head(hh), kw_ref[...])))
        for hh in range(KV_B, heads):
            put(hh, head(hh))

    @pl.when(j == t_cq)
    def _():
        for hh in range(heads):
            put(hh, rope_c(head(hh)) * scale_c)

    @pl.when(j == t_ck)
    def _():
        for hh in range(heads):
            put(hh, rope_c(head(hh)))

    @pl.when(j >= NP_TILES)
    def _():
        g_ref[...] = _sigmoid(acc + bg_ref[...]).astype(g_ref.dtype)


def _inproj_call(h, w_in, b_gate, qw, kw, tabs):
    m, d = h.shape
    n_in = w_in.shape[1]
    gate_w = n_in - P_W
    assert B_KW == TN // 2 and gate_w % TN == 0
    ltab = tabs[0].shape[0]
    tm = _row_tile(ltab, 1024)
    assert m % tm == 0
    nt = ltab // tm
    tab_spec = pl.BlockSpec((tm, HEAD_DIM), lambda i, j: (i % nt, 0))
    vec_spec = pl.BlockSpec((1, HEAD_DIM), lambda i, j: (0, 0))
    return pl.pallas_call(
        _inproj_kernel,
        grid=(m // tm, n_in // TN),
        in_specs=[
            pl.BlockSpec((tm, d), lambda i, j: (i, 0)),
            pl.BlockSpec((d, TN), lambda i, j: (0, j)),
            pl.BlockSpec((1, TN), lambda i, j: (0, jnp.maximum(j - NP_TILES, 0))),
            vec_spec, vec_spec, tab_spec, tab_spec, tab_spec, tab_spec,
        ],
        out_specs=[
            pl.BlockSpec((tm, TN), lambda i, j: (i, jnp.minimum(j, NP_TILES - 1))),
            pl.BlockSpec((tm, TN), lambda i, j: (i, jnp.maximum(j - NP_TILES, 0))),
        ],
        out_shape=[
            jax.ShapeDtypeStruct((m, P_W), BF16),
            jax.ShapeDtypeStruct((m, gate_w), BF16),
        ],
        compiler_params=_params(2),
        name="inproj",
    )(h, w_in, b_gate.reshape(1, gate_w), qw.reshape(1, HEAD_DIM), kw.reshape(1, HEAD_DIM), *tabs)


def _softmax2(s_a, s_b):
    m = jnp.maximum(jnp.max(s_a, axis=-1, keepdims=True), jnp.max(s_b, axis=-1, keepdims=True))
    e_a = jnp.exp(s_a - m)
    e_b = jnp.exp(s_b - m)
    return e_a, e_b, jnp.sum(e_a, axis=-1, keepdims=True) + jnp.sum(e_b, axis=-1, keepdims=True)


def _softmax1(s, exp=jnp.exp):
    m = jnp.max(s, axis=-1, keepdims=True)
    e = exp(s - m)
    return e, jnp.sum(e, axis=-1, keepdims=True)


def _silu_gate(o, z):
    z = z.astype(F32)
    return o * (z * _sigmoid(z))


def _split_maps(q):
    lane = lax.broadcasted_iota(jnp.int32, q.shape, 1)
    zero = jnp.zeros_like(q)
    return jnp.where(lane < DC, q, zero), jnp.where(lane >= DC, q, zero)


def _lam(lq1_ref, lk1_ref, lq2_ref, lk2_ref, li_ref):
    return (jnp.exp(jnp.sum(lq1_ref[...] * lk1_ref[...], axis=-1, keepdims=True))
            - jnp.exp(jnp.sum(lq2_ref[...] * lk2_ref[...], axis=-1, keepdims=True)) + li_ref[:, 0:1])


UNITS = 4


def _scores_stage(q, kl, kc, s_ref, m_ref):
    length = kl.shape[0]
    s_l = _dot_nt(q, kl)
    s_c = _dot_nt(q, kc)
    s_ref[:, :length] = s_l
    s_ref[:, length:] = s_c
    m = jnp.maximum(jnp.max(s_l, axis=-1, keepdims=True), jnp.max(s_c, axis=-1, keepdims=True))
    m_ref[...] = jnp.broadcast_to(m, m_ref.shape)


def _softmax_pv_stage(s_ref, m_ref, p_ref, vl, vc):
    length = vl.shape[0]
    n_blk = s_ref.shape[1] // HEAD_DIM
    m = m_ref[...]
    l_part = None
    for k in range(n_blk):
        blk = slice(k * HEAD_DIM, (k + 1) * HEAD_DIM)
        e = jnp.exp2(s_ref[:, blk] - m)
        l_part = e if l_part is None else l_part + e
        p_ref[:, blk] = e.astype(p_ref.dtype)
    o = _dot(p_ref[:, :length], vl) + _dot(p_ref[:, length:], vc)
    return o, jnp.sum(l_part, axis=-1, keepdims=True)


def _pipelined_units(first_step, unit_q, next_tile_q, unit_kv, finish, scratch):
    s_bufs, m_bufs, p_bufs = scratch[0:2], scratch[2:4], scratch[4:6]

    @pl.when(first_step)
    def _():
        kl, kc, _, _ = unit_kv(0)
        _scores_stage(unit_q(0), kl, kc, s_bufs[0], m_bufs[0])

    for r in range(UNITS):
        cur, nxt = r % 2, (r + 1) % 2
        if r + 1 < UNITS:
            kl, kc, _, _ = unit_kv(r + 1)
            _scores_stage(unit_q(r + 1), kl, kc, s_bufs[nxt], m_bufs[nxt])
        else:
            kl, kc, _, _ = unit_kv(0)
            _scores_stage(next_tile_q(), kl, kc, s_bufs[nxt], m_bufs[nxt])
        _, _, vl, vc = unit_kv(r)
        o, l = _softmax_pv_stage(s_bufs[cur], m_bufs[cur], p_bufs[cur], vl, vc)
        finish(r, o, l)


def _pipeline_scratch(tq, lk):
    return ([pltpu.VMEM((tq, lk), F32)] * 2 + [pltpu.VMEM((tq, HEAD_DIM), F32)] * 2 + [pltpu.VMEM((tq, lk), BF16)] * 2)


def _gqa_kernel(q_ref, qn_ref, kl_ref, vl_ref, kc_ref, vc_ref, z_ref, o_ref, *scratch):
    def cols(r):
        return slice(r * HEAD_DIM, (r + 1) * HEAD_DIM)

    def finish(r, o, l):
        o_ref[0, :, cols(r)] = _silu_gate(o / l, z_ref[0, :, cols(r)]).astype(o_ref.dtype)

    _pipelined_units(
        pl.program_id(2) == 0,
        lambda r: q_ref[0, :, cols(r)],
        lambda: qn_ref[0],
        lambda r: (kl_ref[0], kc_ref[0], vl_ref[0], vc_ref[0]),
        finish, scratch)


def _gqa_call(p, pc):
    b, length, _ = p.shape
    lc = pc.shape[1]
    tq = _row_tile(length, 256)
    n_i = length // tq
    assert H_B // KV_B == UNITS
    gw = UNITS * HEAD_DIM
    kv = lambda n, off: pl.BlockSpec((1, n, HEAD_DIM), lambda bi, g, i: (bi, 0, off // HEAD_DIM + g))
    return pl.pallas_call(
        _gqa_kernel,
        grid=(b, KV_B, n_i),
        in_specs=[
            pl.BlockSpec((1, tq, gw), lambda bi, g, i: (bi, i, OFF_BQ // gw + g)),
            pl.BlockSpec((1, tq, HEAD_DIM),
                         lambda bi, g, i: (bi, jnp.minimum(i + 1, n_i - 1), OFF_BQ // HEAD_DIM + UNITS * g)),
            kv(length, OFF_BK), kv(length, OFF_BV), kv(lc, OFF_BK), kv(lc, OFF_BV),
            pl.BlockSpec((1, tq, gw), lambda bi, g, i: (bi, i, OFF_BZ // gw + g)),
        ],
        out_specs=pl.BlockSpec((1, tq, gw), lambda bi, g, i: (bi, i, g)),
        out_shape=jax.ShapeDtypeStruct((b, length, B_QW), BF16),
        scratch_shapes=_pipeline_scratch(tq, length + lc),
        compiler_params=_params(3),
        name="gqa",
    )(p, p, p, p, pc, pc, p)


def _diff_core(q, k, v, lam, sw, li):
    q1, q2 = _split_maps(q)
    e1, l1 = _softmax1(_dot_nt(q1, k), jnp.exp2)
    e2, l2 = _softmax1(_dot_nt(q2, k), jnp.exp2)
    a = (e1 * (1.0 / l1) - e2 * (lam / l2)).astype(BF16)
    o = _head_rmsnorm(_dot(a, v), sw)
    return o * (1.0 - li)


def _diff_kernel(q_ref, qn_ref, kl_ref, vl_ref, kc_ref, vc_ref, z_ref, lq1_ref, lk1_ref, lq2_ref, lk2_ref, li_ref,
                 sw_ref, o_ref, *scratch):
    lam = _lam(lq1_ref, lk1_ref, lq2_ref, lk2_ref, li_ref)
    pending = {}

    def cols(hh):
        return slice(hh * HEAD_DIM, (hh + 1) * HEAD_DIM)

    def finish(r, o, l):
        hh = r // 2
        if r % 2 == 0:
            pending[hh] = o / l
            return
        o = pending.pop(hh) - o * (lam / l)
        o = _head_rmsnorm(o, sw_ref[...]) * (1.0 - li_ref[...])
        o_ref[0, :, cols(hh)] = _silu_gate(o, z_ref[0, :, cols(hh)]).astype(o_ref.dtype)

    _pipelined_units(
        pl.program_id(2) == 0,
        lambda r: _split_maps(q_ref[0, :, cols(r // 2)])[r % 2],
        lambda: _split_maps(qn_ref[0])[0],
        lambda r: (kl_ref[0, :, cols(r // 2)], kc_ref[0, :, cols(r // 2)],
                   vl_ref[0, :, cols(r // 2)], vc_ref[0, :, cols(r // 2)]),
        finish, scratch)


def _diff_call(p, pc, lam_args, li, sw):
    b, length, _ = p.shape
    lc = pc.shape[1]
    tq = _row_tile(length, 256)
    n_i = length // tq
    heads = UNITS // 2
    assert H_C % heads == 0
    pw = heads * HEAD_DIM
    small = lambda w: pl.BlockSpec((1, w), lambda bi, g, i: (0, 0))
    kv = lambda n, off: pl.BlockSpec((1, n, pw), lambda bi, g, i: (bi, 0, off // pw + g))
    return pl.pallas_call(
        _diff_kernel,
        grid=(b, H_C // heads, n_i),
        in_specs=[
            pl.BlockSpec((1, tq, pw), lambda bi, g, i: (bi, i, OFF_CQ // pw + g)),
            pl.BlockSpec((1, tq, HEAD_DIM),
                         lambda bi, g, i: (bi, jnp.minimum(i + 1, n_i - 1), OFF_CQ // HEAD_DIM + heads * g)),
            kv(length, OFF_CK), kv(length, OFF_CV), kv(lc, OFF_CK), kv(lc, OFF_CV),
            pl.BlockSpec((1, tq, pw), lambda bi, g, i: (bi, i, OFF_CZ // pw + g)),
            small(DC), small(DC), small(DC), small(DC), small(HEAD_DIM), small(HEAD_DIM),
        ],
        out_specs=pl.BlockSpec((1, tq, pw), lambda bi, g, i: (bi, i, g)),
        out_shape=jax.ShapeDtypeStruct((b, length, C_W), BF16),
        scratch_shapes=_pipeline_scratch(tq, length + lc),
        compiler_params=_params(3),
        name="diff",
    )(p, p, p, p, pc, pc, p, *lam_args, li, sw)


def _nbr_key_row0(t, rows):
    return jnp.clip(t * NBR_R - WIN_H // 2, 0, rows - NBR_KR)


def _nbr_kernel(q_ref, kl_ref, vl_ref, kc_ref, vc_ref, z_ref, bias_ref, o_ref, *, rows):
    t = pl.program_id(1)
    span = NBR_KR * GRID_W
    k_off = pl.multiple_of(_nbr_key_row0(t, rows) * GRID_W, NBR_R * GRID_W)
    for hh in range(H_A):
        cols = slice(hh * HEAD_DIM, (hh + 1) * HEAD_DIM)
        q = q_ref[0, :, cols]
        s_win = _dot_nt(q, kl_ref[0, pl.ds(k_off, span), cols]) + bias_ref[hh, 0]
        s_ctx = _dot_nt(q, kc_ref[0, :, cols])
        e_w, e_c, l = _softmax2(s_win, s_ctx)
        o = _dot(e_w.astype(BF16), vl_ref[0, pl.ds(k_off, span), cols]) + _dot(e_c.astype(BF16), vc_ref[0, :, cols])
        o_ref[0, :, cols] = _silu_gate(o / l, z_ref[0, :, cols]).astype(o_ref.dtype)


def _nbr_bias_selectors(rows):
    nt = rows // NBR_R
    kh = min(WIN_H, rows)
    n_dr, n_dc = 2 * WIN_H - 1, 2 * WIN_W - 1
    qc = np.arange(GRID_W)[:, None]
    kc = np.arange(GRID_W)[None, :]
    dc = np.clip(kc - qc, -(WIN_W - 1), WIN_W - 1) + (WIN_W - 1)
    cs = np.clip(qc - WIN_W // 2, 0, GRID_W - WIN_W)
    in_win = (kc >= cs) & (kc < cs + WIN_W)
    sel_dc = (dc[None] == np.arange(n_dc)[:, None, None]).astype(np.float32)
    i = np.arange(NBR_R)[:, None]
    jr = np.arange(NBR_KR)[None, :]
    sel_dr, in_band = [], []
    for t in (0, 1, nt - 1):
        r0 = NBR_R * t
        k0 = int(np.clip(r0 - WIN_H // 2, 0, rows - NBR_KR))
        rq = r0 + i
        rk = k0 + jr
        rs = np.clip(rq - kh // 2, 0, rows - kh)
        band = (rk >= rs) & (rk < rs + kh)
        dr = np.clip(rk - rq + (WIN_H - 1), 0, n_dr - 1)
        sel_dr.append((dr[..., None] == np.arange(n_dr)) & band[..., None])
        in_band.append(band)
    sel_dr = np.stack(sel_dr).astype(np.float32)
    valid = np.stack(in_band)[:, :, None, :, None] & in_win[None, None, :, None, :]
    return sel_dc, sel_dr, valid


def _nbr_bias_tables(rpb, rows):
    sel_dc, sel_dr, valid = _nbr_bias_selectors(rows)
    by_col = jnp.einsum("lhrd,dqk->lhrqk", rpb, sel_dc, precision=lax.Precision.HIGHEST)
    bias = jnp.einsum("tijr,lhrqk->lhtiqjk", sel_dr, by_col, precision=lax.Precision.HIGHEST)
    bias = jnp.where(valid[None, None], bias, NEG)
    return bias.reshape(rpb.shape[0], H_A, 3, NBR_R * GRID_W, NBR_KR * GRID_W)


def _nbr_call(p, pc, bias):
    b, length, _ = p.shape
    lc = pc.shape[1]
    rows = length // GRID_W
    assert rows % NBR_R == 0 and rows >= NBR_KR
    tq = NBR_R * GRID_W
    nt = rows // NBR_R
    blk = lambda n, col: pl.BlockSpec((1, n, A_W), lambda bi, t: (bi, 0, col))
    tile = lambda col: pl.BlockSpec((1, tq, A_W), lambda bi, t: (bi, t, col))
    bias_type = lambda bi, t: (0, jnp.where(t == 0, 0, jnp.where(t == nt - 1, 2, 1)), 0, 0)
    return pl.pallas_call(
        functools.partial(_nbr_kernel, rows=rows),
        grid=(b, nt),
        in_specs=[
            tile(OFF_AQ // A_W), blk(length, OFF_AK // A_W), blk(length, OFF_AV // A_W),
            blk(lc, OFF_AK // A_W), blk(lc, OFF_AV // A_W), tile(OFF_AZ // A_W),
            pl.BlockSpec((H_A, 1, tq, NBR_KR * GRID_W), bias_type),
        ],
        out_specs=pl.BlockSpec((1, tq, A_W), lambda bi, t: (bi, t, 0)),
        out_shape=jax.ShapeDtypeStruct((b, length, A_W), BF16),
        compiler_params=_params(2),
        name="nbr",
    )(p, p, p, pc, pc, p, bias)


def _ctx_attn_kernel(p_ref, lq1_ref, lk1_ref, lq2_ref, lk2_ref, li_ref, sw_ref, oa_ref, ob_ref, oc_ref):
    def col(off, hh):
        return p_ref[0, :, off + hh * HEAD_DIM: off + (hh + 1) * HEAD_DIM]

    def plain(q, k, v, exp):
        e, l = _softmax1(_dot_nt(q, k), exp)
        return _dot(e.astype(BF16), v) / l

    for hh in range(H_A):
        o = plain(col(OFF_AQ, hh), col(OFF_AK, hh), col(OFF_AV, hh), jnp.exp)
        oa_ref[0, :, hh * HEAD_DIM:(hh + 1) * HEAD_DIM] = _silu_gate(o, col(OFF_AZ, hh)).astype(oa_ref.dtype)
    rep = H_B // KV_B
    for hh in range(H_B):
        o = plain(col(OFF_BQ, hh), col(OFF_BK, hh // rep), col(OFF_BV, hh // rep), jnp.exp2)
        ob_ref[0, :, hh * HEAD_DIM:(hh + 1) * HEAD_DIM] = _silu_gate(o, col(OFF_BZ, hh)).astype(ob_ref.dtype)
    lam = _lam(lq1_ref, lk1_ref, lq2_ref, lk2_ref, li_ref)
    for hh in range(H_C):
        o = _diff_core(col(OFF_CQ, hh), col(OFF_CK, hh), col(OFF_CV, hh), lam, sw_ref[...], li_ref[...])
        oc_ref[0, :, hh * HEAD_DIM:(hh + 1) * HEAD_DIM] = _silu_gate(o, col(OFF_CZ, hh)).astype(oc_ref.dtype)


def _ctx_attn_call(pc, lam_args, li, sw):
    b, lc, _ = pc.shape
    small = lambda w: pl.BlockSpec((1, w), lambda bi: (0, 0))
    out = lambda w: pl.BlockSpec((1, lc, w), lambda bi: (bi, 0, 0))
    return pl.pallas_call(
        _ctx_attn_kernel,
        grid=(b,),
        in_specs=[pl.BlockSpec((1, lc, P_W), lambda bi: (bi, 0, 0)),
                  small(DC), small(DC), small(DC), small(DC), small(HEAD_DIM), small(HEAD_DIM)],
        out_specs=[out(A_W), out(B_QW), out(C_W)],
        out_shape=[jax.ShapeDtypeStruct((b, lc, A_W), BF16), jax.ShapeDtypeStruct((b, lc, B_QW), BF16),
                   jax.ShapeDtypeStruct((b, lc, C_W), BF16)],
        compiler_params=_params(1),
        name="ctx_attn",
    )(pc, *lam_args, li, sw)


def _merge_kernel(aa_ref, ab_ref, ac_ref, wa_ref, wb_ref, wc_ref, ga_ref, gb_ref, gc_ref, y_ref):
    y = (ga_ref[...].astype(F32) * _dot(aa_ref[...], wa_ref[...])
         + gb_ref[...].astype(F32) * _dot(ab_ref[...], wb_ref[...])
         + gc_ref[...].astype(F32) * _dot(ac_ref[...], wc_ref[...]))
    y_ref[...] = y.astype(y_ref.dtype)


def _merge_call(aa, ab, ac, wa, wb, wc, g):
    m = aa.shape[0]
    d = wa.shape[1]
    tm = _row_tile(m, 1024)
    nd = d // TN
    a_spec = lambda w: pl.BlockSpec((tm, w), lambda i, n: (i, 0))
    w_spec = lambda w: pl.BlockSpec((w, TN), lambda i, n: (0, n))
    g_spec = lambda k: pl.BlockSpec((tm, TN), lambda i, n: (i, k * nd + n))
    return pl.pallas_call(
        _merge_kernel,
        grid=(m // tm, nd),
        in_specs=[a_spec(A_W), a_spec(B_QW), a_spec(C_W), w_spec(A_W), w_spec(B_QW), w_spec(C_W),
                  g_spec(0), g_spec(1), g_spec(2)],
        out_specs=pl.BlockSpec((tm, TN), lambda i, n: (i, n)),
        out_shape=jax.ShapeDtypeStruct((m, d), BF16),
        compiler_params=_params(2),
        name="merge",
    )(aa, ab, ac, wa, wb, wc, g, g, g)


def _outproj_kernel(y_ref, w_ref, x_ref, mod_ref, o_ref):
    gate = mod_ref[0, 2:3, :]
    o_ref[0] = x_ref[0] + gate * _dot(y_ref[0], w_ref[...])


def _outproj_call(y, w_out, x, mod):
    b, length, d = x.shape
    tm = _row_tile(length, 512)
    per_batch = mod.shape[0] != 1
    row = lambda bi, i: (bi, i, 0)
    return pl.pallas_call(
        _outproj_kernel,
        grid=(b, length // tm),
        in_specs=[
            pl.BlockSpec((1, tm, d), row),
            pl.BlockSpec((d, d), lambda bi, i: (0, 0)),
            pl.BlockSpec((1, tm, d), row),
            pl.BlockSpec((1, 3, d), (lambda bi, i: (bi, 0, 0)) if per_batch else (lambda bi, i: (0, 0, 0))),
        ],
        out_specs=pl.BlockSpec((1, tm, d), row),
        out_shape=jax.ShapeDtypeStruct((b, length, d), F32),
        compiler_params=_params(2),
        name="outproj",
    )(y, w_out, x, mod)


def _rope_tables(length):
    pos = jnp.arange(length)
    row = (pos // GRID_W).astype(F32)[:, None]
    col = (pos % GRID_W).astype(F32)[:, None]

    def cos_sin(rot_dim):
        n = rot_dim // 4
        inv_freq = ROPE_THETA ** (-jnp.arange(n, dtype=F32) / n)
        ang = jnp.concatenate([row * inv_freq, col * inv_freq], axis=-1)
        return jnp.cos(ang), jnp.sin(ang)

    cb, sb = cos_sin(HEAD_DIM)
    cc, sc = cos_sin(DC)
    return (jnp.concatenate([cb, cb], axis=-1), jnp.concatenate([-sb, sb], axis=-1),
            jnp.concatenate([cc, cc, cc, cc], axis=-1), jnp.concatenate([-sc, sc, -sc, sc], axis=-1))


def _identity_tables(length):
    one = jnp.ones((length, HEAD_DIM), F32)
    zero = jnp.zeros((length, HEAD_DIM), F32)
    return one, zero, one, zero


def kernel(x, c, ctx, c_ctx, norm_w, w_ada, b_ada, w_in, b_gate, rpb, q_norm_w, k_norm_w, lam_q1, lam_k1, lam_q2,
           lam_k2, subln_w, w_bo_a, w_bo_b, w_bo_c, w_out, final_norm_w):
    b, length, d = x.shape
    lc = ctx.shape[1]
    depth = w_in.shape[0]
    rows = length // GRID_W

    n_cond = b + 1
    pad = (-n_cond) % 8
    cc = jnp.concatenate([c, c_ctx[None], jnp.zeros((pad, d), F32)], axis=0)
    mod_all = _ada_call(cc, w_ada, b_ada).reshape(depth, n_cond + pad, 3, d)

    tabs_lat = _rope_tables(length)
    tabs_ctx = _identity_tables(_row_tile(b * lc, 1024))
    nbr_bias = _nbr_bias_tables(rpb, rows)

    w_in_bf = w_in.astype(BF16)
    wa_bf, wb_bf, wc_bf, wo_bf = (w.astype(BF16) for w in (w_bo_a, w_bo_b, w_bo_c, w_out))

    xl, xc = x, ctx
    for l in range(depth):
        update_ctx = l < depth - 1
        mod_l = mod_all[l, :b]
        mod_c = mod_all[l, b:b + 1]
        h = _norm_call(xl, norm_w[l], mod_l, BF16)
        hc = _norm_call(xc, norm_w[l], mod_c, BF16)
        p, g = _inproj_call(h.reshape(b * length, d), w_in_bf[l], b_gate[l], q_norm_w[l], k_norm_w[l], tabs_lat)
        pc, gc = _inproj_call(hc.reshape(b * lc, d), w_in_bf[l], b_gate[l], q_norm_w[l], k_norm_w[l], tabs_ctx)
        p = p.reshape(b, length, P_W)
        pc = pc.reshape(b, lc, P_W)

        lam_init = 0.8 - 0.6 * float(np.exp(-0.3 * l))
        li = jnp.full((1, HEAD_DIM), lam_init, F32)
        lam_args = [v[l].reshape(1, DC) for v in (lam_q1, lam_k1, lam_q2, lam_k2)]
        sw = subln_w[l].reshape(1, HEAD_DIM)

        o_a = _nbr_call(p, pc, nbr_bias[l])
        o_b = _gqa_call(p, pc)
        o_c = _diff_call(p, pc, lam_args, li, sw)
        y = _merge_call(o_a.reshape(b * length, A_W), o_b.reshape(b * length, B_QW), o_c.reshape(b * length, C_W),
                        wa_bf[l], wb_bf[l], wc_bf[l], g)
        xl = _outproj_call(y.reshape(b, length, d), wo_bf[l], xl, mod_l)
        if update_ctx:
            oc_a, oc_b, oc_c = _ctx_attn_call(pc, lam_args, li, sw)
            yc = _merge_call(oc_a.reshape(b * lc, A_W), oc_b.reshape(b * lc, B_QW), oc_c.reshape(b * lc, C_W),
                             wa_bf[l], wb_bf[l], wc_bf[l], gc)
            xc = _outproj_call(yc.reshape(b, lc, d), wo_bf[l], xc, mod_c)
    return _norm_call(xl, final_norm_w, None, F32)
```

```python
import functools

import numpy as np
import jax
import jax.numpy as jnp
from jax import lax
from jax.experimental import pallas as pl
from jax.experimental.pallas import tpu as pltpu

F32 = jnp.float32
BF16 = jnp.bfloat16

GRID_W = 64
HEAD_DIM = 128
H_A = 4
WIN_H = 8
WIN_W = 16
H_B = 8
KV_B = 2
H_C = 4
DC = HEAD_DIM // 2
ROPE_THETA = 10000.0
EPS = 1e-6
NEG = -1e30
LOG2E = 1.4426950408889634

A_W = H_A * HEAD_DIM
B_QW = H_B * HEAD_DIM
B_KW = KV_B * HEAD_DIM
C_W = H_C * HEAD_DIM

OFF_AQ, OFF_AK, OFF_AV, OFF_AZ = 0, A_W, 2 * A_W, 3 * A_W
OFF_BQ = 4 * A_W
OFF_BK = OFF_BQ + B_QW
OFF_BV = OFF_BK + B_KW
OFF_BZ = OFF_BV + B_KW
OFF_CQ = OFF_BZ + B_QW
OFF_CK = OFF_CQ + C_W
OFF_CV = OFF_CK + C_W
OFF_CZ = OFF_CV + C_W
P_W = OFF_CZ + C_W

TN = 512
INPROJ_TM = 2048
NP_TILES = P_W // TN

NBR_R = 4
NBR_KR = 12

VMEM_LIMIT = 56 * 1024 * 1024


def _params(n_axes):
    return pltpu.CompilerParams(dimension_semantics=("arbitrary",) * n_axes, vmem_limit_bytes=VMEM_LIMIT)


def _sigmoid(x):
    return 1.0 / (1.0 + jnp.exp(-x))


def _dot(a, b):
    return jnp.dot(a, b, preferred_element_type=F32)


def _dot_nt(a, b):
    return lax.dot_general(a, b, (((1,), (1,)), ((), ())), preferred_element_type=F32)


def _head_rmsnorm(y, w):
    return y * lax.rsqrt(jnp.mean(y * y, axis=-1, keepdims=True) + EPS) * w


def _ada_kernel(c_ref, w_ref, b_ref, o_ref):
    a = c_ref[...]
    a = a * _sigmoid(a)
    o_ref[0] = _dot(a.astype(BF16), w_ref[0].astype(BF16)) + b_ref[0]


def _ada_call(cc, w_ada, b_ada):
    depth, d, n3 = w_ada.shape
    rows = cc.shape[0]
    tn = 1536
    return pl.pallas_call(
        _ada_kernel,
        grid=(depth, n3 // tn),
        in_specs=[
            pl.BlockSpec((rows, d), lambda l, j: (0, 0)),
            pl.BlockSpec((1, d, tn), lambda l, j: (l, 0, j)),
            pl.BlockSpec((1, 1, tn), lambda l, j: (l, 0, j)),
        ],
        out_specs=pl.BlockSpec((1, rows, tn), lambda l, j: (l, 0, j)),
        out_shape=jax.ShapeDtypeStruct((depth, rows, n3), F32),
        compiler_params=_params(2),
        name="ada",
    )(cc, w_ada, b_ada.reshape(depth, 1, n3))


def _norm_mod_kernel(x_ref, w_ref, mod_ref, o_ref):
    x = x_ref[0]
    y = x * lax.rsqrt(jnp.mean(x * x, axis=-1, keepdims=True) + EPS)
    y = y * w_ref[...]
    shift = mod_ref[0, 0:1, :]
    scale = mod_ref[0, 1:2, :]
    o_ref[0] = (y * (1.0 + scale) + shift).astype(o_ref.dtype)


def _norm_kernel(x_ref, w_ref, o_ref):
    x = x_ref[0]
    y = x * lax.rsqrt(jnp.mean(x * x, axis=-1, keepdims=True) + EPS)
    o_ref[0] = (y * w_ref[...]).astype(o_ref.dtype)


def _row_tile(n, cap):
    t = min(n, cap)
    assert n % t == 0
    return t


def _norm_call(x, w, mod, out_dtype):
    b, length, d = x.shape
    tm = _row_tile(length, 512)
    in_specs = [
        pl.BlockSpec((1, tm, d), lambda bi, i: (bi, i, 0)),
        pl.BlockSpec((1, d), lambda bi, i: (0, 0)),
    ]
    args = [x, w.reshape(1, d)]
    if mod is not None:
        per_batch = mod.shape[0] != 1
        in_specs.append(pl.BlockSpec((1, 3, d), (lambda bi, i: (bi, 0, 0)) if per_batch else (lambda bi, i: (0, 0, 0))))
        args.append(mod)
    return pl.pallas_call(
        _norm_mod_kernel if mod is not None else _norm_kernel,
        grid=(b, length // tm),
        in_specs=in_specs,
        out_specs=pl.BlockSpec((1, tm, d), lambda bi, i: (bi, i, 0)),
        out_shape=jax.ShapeDtypeStruct((b, length, d), out_dtype),
        compiler_params=_params(2),
        name="norm",
    )(*args)


def _inproj_kernel(h_ref, w_ref, bg_ref, qw_ref, kw_ref, cb_ref, sb_ref, cc_ref, sc_ref, p_ref, g_ref):
    j = pl.program_id(1)
    acc = _dot(h_ref[...], w_ref[...])
    scale_a = HEAD_DIM ** -0.5
    scale_b = HEAD_DIM ** -0.5 * LOG2E
    scale_c = DC ** -0.5 * LOG2E
    heads = TN // HEAD_DIM

    def rope_b(y):
        return y * cb_ref[...] + pltpu.roll(y, HEAD_DIM // 2, 1) * sb_ref[...]

    def rope_c(y):
        lane = lax.broadcasted_iota(jnp.int32, y.shape, 1)
        first = (lane & (DC // 2)) == 0
        rot = jnp.where(first, pltpu.roll(y, HEAD_DIM - DC // 2, 1), pltpu.roll(y, DC // 2, 1))
        return y * cc_ref[...] + rot * sc_ref[...]

    def head(hh):
        return acc[:, hh * HEAD_DIM:(hh + 1) * HEAD_DIM]

    def put(hh, y):
        p_ref[:, hh * HEAD_DIM:(hh + 1) * HEAD_DIM] = y.astype(p_ref.dtype)

    t_aq = OFF_AQ // TN
    t_bq0, t_bq1 = OFF_BQ // TN, OFF_BK // TN
    t_bkv = OFF_BK // TN
    t_cq, t_ck = OFF_CQ // TN, OFF_CK // TN

    special = (j == t_aq) | ((j >= t_bq0) & (j <= t_bkv)) | (j == t_cq) | (j == t_ck)

    @pl.when((j < NP_TILES) & jnp.logical_not(special))
    def _():
        p_ref[...] = acc.astype(p_ref.dtype)

    @pl.when(j == t_aq)
    def _():
        p_ref[...] = (acc * scale_a).astype(p_ref.dtype)

    @pl.when((j >= t_bq0) & (j < t_bq1))
    def _():
        for hh in range(heads):
            put(hh, rope_b(_head_rmsnorm(head(hh), qw_ref[...])) * scale_b)

    @pl.when(j == t_bkv)
    def _():
        for hh in range(KV_B):
            put(hh, rope_b(_head_rmsnorm(head(hh), kw_ref[...])))
        for hh in range(KV_B, heads):
            put(hh, head(hh))

    @pl.when(j == t_cq)
    def _():
        for hh in range(heads):
            put(hh, rope_c(head(hh)) * scale_c)

    @pl.when(j == t_ck)
    def _():
        for hh in range(heads):
            put(hh, rope_c(head(hh)))

    @pl.when(j >= NP_TILES)
    def _():
        g_ref[...] = _sigmoid(acc + bg_ref[...]).astype(g_ref.dtype)


def _inproj_call(h, w_in, b_gate, qw, kw, tabs):
    m, d = h.shape
    n_in = w_in.shape[1]
    gate_w = n_in - P_W
    assert B_KW == TN // 2 and gate_w % TN == 0
    ltab = tabs[0].shape[0]
    tm = _row_tile(ltab, INPROJ_TM)
    assert m % tm == 0
    nt = ltab // tm
    tab_spec = pl.BlockSpec((tm, HEAD_DIM), lambda i, j: (i % nt, 0))
    vec_spec = pl.BlockSpec((1, HEAD_DIM), lambda i, j: (0, 0))
    return pl.pallas_call(
        _inproj_kernel,
        grid=(m // tm, n_in // TN),
        in_specs=[
            pl.BlockSpec((tm, d), lambda i, j: (i, 0)),
            pl.BlockSpec((d, TN), lambda i, j: (0, j)),
            pl.BlockSpec((1, TN), lambda i, j: (0, jnp.maximum(j - NP_TILES, 0))),
            vec_spec, vec_spec, tab_spec, tab_spec, tab_spec, tab_spec,
        ],
        out_specs=[
            pl.BlockSpec((tm, TN), lambda i, j: (i, jnp.minimum(j, NP_TILES - 1))),
            pl.BlockSpec((tm, TN), lambda i, j: (i, jnp.maximum(j - NP_TILES, 0))),
        ],
        out_shape=[
            jax.ShapeDtypeStruct((m, P_W), BF16),
            jax.ShapeDtypeStruct((m, gate_w), BF16),
        ],
        compiler_params=_params(2),
        name="inproj",
    )(h, w_in, b_gate.reshape(1, gate_w), qw.reshape(1, HEAD_DIM), kw.reshape(1, HEAD_DIM), *tabs)


def _softmax2(s_a, s_b):
    m = jnp.maximum(jnp.max(s_a, axis=-1, keepdims=True), jnp.max(s_b, axis=-1, keepdims=True))
    e_a = jnp.exp(s_a - m)
    e_b = jnp.exp(s_b - m)
    return e_a, e_b, jnp.sum(e_a, axis=-1, keepdims=True) + jnp.sum(e_b, axis=-1, keepdims=True)


def _softmax1(s, exp=jnp.exp):
    m = jnp.max(s, axis=-1, keepdims=True)
    e = exp(s - m)
    return e, jnp.sum(e, axis=-1, keepdims=True)


def _silu_gate(o, z):
    z = z.astype(F32)
    return o * (z * _sigmoid(z))


def _split_maps(q):
    lane = lax.broadcasted_iota(jnp.int32, q.shape, 1)
    zero = jnp.zeros_like(q)
    return jnp.where(lane < DC, q, zero), jnp.where(lane >= DC, q, zero)


def _lam(lq1_ref, lk1_ref, lq2_ref, lk2_ref, li_ref):
    return (jnp.exp(jnp.sum(lq1_ref[...] * lk1_ref[...], axis=-1, keepdims=True))
            - jnp.exp(jnp.sum(lq2_ref[...] * lk2_ref[...], axis=-1, keepdims=True)) + li_ref[:, 0:1])


UNITS = 4


def _scores_stage(q, kl, kc, s_ref, m_ref):
    length = kl.shape[0]
    s_l = _dot_nt(q, kl)
    s_c = _dot_nt(q, kc)
    s_ref[:, :length] = s_l
    s_ref[:, length:] = s_c
    m = jnp.maximum(jnp.max(s_l, axis=-1, keepdims=True), jnp.max(s_c, axis=-1, keepdims=True))
    m_ref[...] = jnp.broadcast_to(m, m_ref.shape)


def _softmax_pv_stage(s_ref, m_ref, p_ref, v_ext):
    n_blk = s_ref.shape[1] // HEAD_DIM
    m = m_ref[...]
    for k in range(n_blk):
        blk = slice(k * HEAD_DIM, (k + 1) * HEAD_DIM)
        p_ref[:, blk] = jnp.exp2((s_ref[:, blk] - m).astype(p_ref.dtype))
    o_ext = _dot(p_ref[...], v_ext)
    return o_ext[:, :HEAD_DIM], o_ext[:, HEAD_DIM:HEAD_DIM + 1]


def _fill_v_ext(v_ext_ref, vl, vc):
    length = vl.shape[0]
    v_ext_ref[:length, :HEAD_DIM] = vl
    v_ext_ref[length:, :HEAD_DIM] = vc
    v_ext_ref[:, HEAD_DIM:] = jnp.ones((v_ext_ref.shape[0], HEAD_DIM), v_ext_ref.dtype)


def _pipelined_units(first_step, prepare, unit_q, next_tile_q, unit_k, unit_v, finish, scratch):
    s_bufs, m_bufs, p_bufs = scratch[0:2], scratch[2:4], scratch[4:6]

    @pl.when(first_step)
    def _():
        prepare()
        _scores_stage(unit_q(0), *unit_k(0), s_bufs[0], m_bufs[0])

    for r in range(UNITS):
        cur, nxt = r % 2, (r + 1) % 2
        if r + 1 < UNITS:
            _scores_stage(unit_q(r + 1), *unit_k(r + 1), s_bufs[nxt], m_bufs[nxt])
        else:
            _scores_stage(next_tile_q(), *unit_k(0), s_bufs[nxt], m_bufs[nxt])
        o, l = _softmax_pv_stage(s_bufs[cur], m_bufs[cur], p_bufs[cur], unit_v(r))
        finish(r, o, l)


def _pipeline_scratch(tq, lk, n_v):
    return ([pltpu.VMEM((tq, lk), F32)] * 2 + [pltpu.VMEM((tq, HEAD_DIM), F32)] * 2 + [pltpu.VMEM((tq, lk), BF16)] * 2
            + [pltpu.VMEM((n_v, lk, 2 * HEAD_DIM), BF16)])


def _gqa_kernel(q_ref, qn_ref, kl_ref, vl_ref, kc_ref, vc_ref, z_ref, o_ref, *scratch):
    def cols(r):
        return slice(r * HEAD_DIM, (r + 1) * HEAD_DIM)

    def finish(r, o, l):
        o_ref[0, :, cols(r)] = _silu_gate(o / l, z_ref[0, :, cols(r)]).astype(o_ref.dtype)

    v_ext_ref = scratch[-1]
    _pipelined_units(
        pl.program_id(2) == 0,
        lambda: _fill_v_ext(v_ext_ref.at[0], vl_ref[0], vc_ref[0]),
        lambda r: q_ref[0, :, cols(r)],
        lambda: qn_ref[0],
        lambda r: (kl_ref[0], kc_ref[0]),
        lambda r: v_ext_ref[0],
        finish, scratch)


def _gqa_call(p, pc):
    b, length, _ = p.shape
    lc = pc.shape[1]
    tq = _row_tile(length, 256)
    n_i = length // tq
    assert H_B // KV_B == UNITS
    gw = UNITS * HEAD_DIM
    kv = lambda n, off: pl.BlockSpec((1, n, HEAD_DIM), lambda bi, g, i: (bi, 0, off // HEAD_DIM + g))
    return pl.pallas_call(
        _gqa_kernel,
        grid=(b, KV_B, n_i),
        in_specs=[
            pl.BlockSpec((1, tq, gw), lambda bi, g, i: (bi, i, OFF_BQ // gw + g)),
            pl.BlockSpec((1, tq, HEAD_DIM),
                         lambda bi, g, i: (bi, jnp.minimum(i + 1, n_i - 1), OFF_BQ // HEAD_DIM + UNITS * g)),
            kv(length, OFF_BK), kv(length, OFF_BV), kv(lc, OFF_BK), kv(lc, OFF_BV),
            pl.BlockSpec((1, tq, gw), lambda bi, g, i: (bi, i, OFF_BZ // gw + g)),
        ],
        out_specs=pl.BlockSpec((1, tq, gw), lambda bi, g, i: (bi, i, g)),
        out_shape=jax.ShapeDtypeStruct((b, length, B_QW), BF16),
        scratch_shapes=_pipeline_scratch(tq, length + lc, 1),
        compiler_params=_params(3),
        name="gqa",
    )(p, p, p, p, pc, pc, p)


def _diff_core(q, k, v, lam, sw, li):
    q1, q2 = _split_maps(q)
    e1, l1 = _softmax1(_dot_nt(q1, k), jnp.exp2)
    e2, l2 = _softmax1(_dot_nt(q2, k), jnp.exp2)
    a = (e1 * (1.0 / l1) - e2 * (lam / l2)).astype(BF16)
    o = _head_rmsnorm(_dot(a, v), sw)
    return o * (1.0 - li)


def _diff_kernel(q_ref, qn_ref, kl_ref, vl_ref, kc_ref, vc_ref, z_ref, lq1_ref, lk1_ref, lq2_ref, lk2_ref, li_ref,
                 sw_ref, o_ref, *scratch):
    lam = _lam(lq1_ref, lk1_ref, lq2_ref, lk2_ref, li_ref)
    pending = {}

    def cols(hh):
        return slice(hh * HEAD_DIM, (hh + 1) * HEAD_DIM)

    def finish(r, o, l):
        hh = r // 2
        if r % 2 == 0:
            pending[hh] = o / l
            return
        o = pending.pop(hh) - o * (lam / l)
        o = _head_rmsnorm(o, sw_ref[...]) * (1.0 - li_ref[...])
        o_ref[0, :, cols(hh)] = _silu_gate(o, z_ref[0, :, cols(hh)]).astype(o_ref.dtype)

    v_ext_ref = scratch[-1]

    def prepare():
        for hh in range(UNITS // 2):
            _fill_v_ext(v_ext_ref.at[hh], vl_ref[0, :, cols(hh)], vc_ref[0, :, cols(hh)])

    _pipelined_units(
        pl.program_id(2) == 0,
        prepare,
        lambda r: _split_maps(q_ref[0, :, cols(r // 2)])[r % 2],
        lambda: _split_maps(qn_ref[0])[0],
        lambda r: (kl_ref[0, :, cols(r // 2)], kc_ref[0, :, cols(r // 2)]),
        lambda r: v_ext_ref[r // 2],
        finish, scratch)


def _diff_call(p, pc, lam_args, li, sw):
    b, length, _ = p.shape
    lc = pc.shape[1]
    tq = _row_tile(length, 256)
    n_i = length // tq
    heads = UNITS // 2
    assert H_C % heads == 0
    pw = heads * HEAD_DIM
    small = lambda w: pl.BlockSpec((1, w), lambda bi, g, i: (0, 0))
    kv = lambda n, off: pl.BlockSpec((1, n, pw), lambda bi, g, i: (bi, 0, off // pw + g))
    return pl.pallas_call(
        _diff_kernel,
        grid=(b, H_C // heads, n_i),
        in_specs=[
            pl.BlockSpec((1, tq, pw), lambda bi, g, i: (bi, i, OFF_CQ // pw + g)),
            pl.BlockSpec((1, tq, HEAD_DIM),
                         lambda bi, g, i: (bi, jnp.minimum(i + 1, n_i - 1), OFF_CQ // HEAD_DIM + heads * g)),
            kv(length, OFF_CK), kv(length, OFF_CV), kv(lc, OFF_CK), kv(lc, OFF_CV),
            pl.BlockSpec((1, tq, pw), lambda bi, g, i: (bi, i, OFF_CZ // pw + g)),
            small(DC), small(DC), small(DC), small(DC), small(HEAD_DIM), small(HEAD_DIM),
        ],
        out_specs=pl.BlockSpec((1, tq, pw), lambda bi, g, i: (bi, i, g)),
        out_shape=jax.ShapeDtypeStruct((b, length, C_W), BF16),
        scratch_shapes=_pipeline_scratch(tq, length + lc, heads),
        compiler_params=_params(3),
        name="diff",
    )(p, p, p, p, pc, pc, p, *lam_args, li, sw)


def _nbr_key_row0(t, rows):
    return jnp.clip(t * NBR_R - WIN_H // 2, 0, rows - NBR_KR)


def _nbr_kernel(q_ref, kl_ref, vl_ref, kc_ref, vc_ref, z_ref, bias_ref, o_ref, *, rows):
    t = pl.program_id(1)
    span = NBR_KR * GRID_W
    k_off = pl.multiple_of(_nbr_key_row0(t, rows) * GRID_W, NBR_R * GRID_W)
    for hh in range(H_A):
        cols = slice(hh * HEAD_DIM, (hh + 1) * HEAD_DIM)
        q = q_ref[0, :, cols]
        s_win = _dot_nt(q, kl_ref[0, pl.ds(k_off, span), cols]) + bias_ref[hh, 0]
        s_ctx = _dot_nt(q, kc_ref[0, :, cols])
        e_w, e_c, l = _softmax2(s_win, s_ctx)
        o = _dot(e_w.astype(BF16), vl_ref[0, pl.ds(k_off, span), cols]) + _dot(e_c.astype(BF16), vc_ref[0, :, cols])
        o_ref[0, :, cols] = _silu_gate(o / l, z_ref[0, :, cols]).astype(o_ref.dtype)


def _nbr_bias_selectors(rows):
    nt = rows // NBR_R
    kh = min(WIN_H, rows)
    n_dr, n_dc = 2 * WIN_H - 1, 2 * WIN_W - 1
    qc = np.arange(GRID_W)[:, None]
    kc = np.arange(GRID_W)[None, :]
    dc = np.clip(kc - qc, -(WIN_W - 1), WIN_W - 1) + (WIN_W - 1)
    cs = np.clip(qc - WIN_W // 2, 0, GRID_W - WIN_W)
    in_win = (kc >= cs) & (kc < cs + WIN_W)
    sel_dc = (dc[None] == np.arange(n_dc)[:, None, None]).astype(np.float32)
    i = np.arange(NBR_R)[:, None]
    jr = np.arange(NBR_KR)[None, :]
    sel_dr, in_band = [], []
    for t in (0, 1, nt - 1):
        r0 = NBR_R * t
        k0 = int(np.clip(r0 - WIN_H // 2, 0, rows - NBR_KR))
        rq = r0 + i
        rk = k0 + jr
        rs = np.clip(rq - kh // 2, 0, rows - kh)
        band = (rk >= rs) & (rk < rs + kh)
        dr = np.clip(rk - rq + (WIN_H - 1), 0, n_dr - 1)
        sel_dr.append((dr[..., None] == np.arange(n_dr)) & band[..., None])
        in_band.append(band)
    sel_dr = np.stack(sel_dr).astype(np.float32)
    valid = np.stack(in_band)[:, :, None, :, None] & in_win[None, None, :, None, :]
    return sel_dc, sel_dr, valid


def _nbr_bias_tables(rpb, rows):
    sel_dc, sel_dr, valid = _nbr_bias_selectors(rows)
    by_col = jnp.einsum("lhrd,dqk->lhrqk", rpb, sel_dc, precision=lax.Precision.HIGHEST)
    bias = jnp.einsum("tijr,lhrqk->lhtiqjk", sel_dr, by_col, precision=lax.Precision.HIGHEST)
    bias = jnp.where(valid[None, None], bias, NEG)
    return bias.reshape(rpb.shape[0], H_A, 3, NBR_R * GRID_W, NBR_KR * GRID_W)


def _nbr_call(p, pc, bias):
    b, length, _ = p.shape
    lc = pc.shape[1]
    rows = length // GRID_W
    assert rows % NBR_R == 0 and rows >= NBR_KR
    tq = NBR_R * GRID_W
    nt = rows // NBR_R
    blk = lambda n, col: pl.BlockSpec((1, n, A_W), lambda bi, t: (bi, 0, col))
    tile = lambda col: pl.BlockSpec((1, tq, A_W), lambda bi, t: (bi, t, col))
    bias_type = lambda bi, t: (0, jnp.where(t == 0, 0, jnp.where(t == nt - 1, 2, 1)), 0, 0)
    return pl.pallas_call(
        functools.partial(_nbr_kernel, rows=rows),
        grid=(b, nt),
        in_specs=[
            tile(OFF_AQ // A_W), blk(length, OFF_AK // A_W), blk(length, OFF_AV // A_W),
            blk(lc, OFF_AK // A_W), blk(lc, OFF_AV // A_W), tile(OFF_AZ // A_W),
            pl.BlockSpec((H_A, 1, tq, NBR_KR * GRID_W), bias_type),
        ],
        out_specs=pl.BlockSpec((1, tq, A_W), lambda bi, t: (bi, t, 0)),
        out_shape=jax.ShapeDtypeStruct((b, length, A_W), BF16),
        compiler_params=_params(2),
        name="nbr",
    )(p, p, p, pc, pc, p, bias)


def _ctx_attn_kernel(p_ref, lq1_ref, lk1_ref, lq2_ref, lk2_ref, li_ref, sw_ref, oa_ref, ob_ref, oc_ref):
    def col(off, hh):
        return p_ref[0, :, off + hh * HEAD_DIM: off + (hh + 1) * HEAD_DIM]

    def plain(q, k, v, exp):
        e, l = _softmax1(_dot_nt(q, k), exp)
        return _dot(e.astype(BF16), v) / l

    for hh in range(H_A):
        o = plain(col(OFF_AQ, hh), col(OFF_AK, hh), col(OFF_AV, hh), jnp.exp)
        oa_ref[0, :, hh * HEAD_DIM:(hh + 1) * HEAD_DIM] = _silu_gate(o, col(OFF_AZ, hh)).astype(oa_ref.dtype)
    rep = H_B // KV_B
    for hh in range(H_B):
        o = plain(col(OFF_BQ, hh), col(OFF_BK, hh // rep), col(OFF_BV, hh // rep), jnp.exp2)
        ob_ref[0, :, hh * HEAD_DIM:(hh + 1) * HEAD_DIM] = _silu_gate(o, col(OFF_BZ, hh)).astype(ob_ref.dtype)
    lam = _lam(lq1_ref, lk1_ref, lq2_ref, lk2_ref, li_ref)
    for hh in range(H_C):
        o = _diff_core(col(OFF_CQ, hh), col(OFF_CK, hh), col(OFF_CV, hh), lam, sw_ref[...], li_ref[...])
        oc_ref[0, :, hh * HEAD_DIM:(hh + 1) * HEAD_DIM] = _silu_gate(o, col(OFF_CZ, hh)).astype(oc_ref.dtype)


def _ctx_attn_call(pc, lam_args, li, sw):
    b, lc, _ = pc.shape
    small = lambda w: pl.BlockSpec((1, w), lambda bi: (0, 0))
    out = lambda w: pl.BlockSpec((1, lc, w), lambda bi: (bi, 0, 0))
    return pl.pallas_call(
        _ctx_attn_kernel,
        grid=(b,),
        in_specs=[pl.BlockSpec((1, lc, P_W), lambda bi: (bi, 0, 0)),
                  small(DC), small(DC), small(DC), small(DC), small(HEAD_DIM), small(HEAD_DIM)],
        out_specs=[out(A_W), out(B_QW), out(C_W)],
        out_shape=[jax.ShapeDtypeStruct((b, lc, A_W), BF16), jax.ShapeDtypeStruct((b, lc, B_QW), BF16),
                   jax.ShapeDtypeStruct((b, lc, C_W), BF16)],
        compiler_params=_params(1),
        name="ctx_attn",
    )(pc, *lam_args, li, sw)


def _merge_kernel(aa_ref, ab_ref, ac_ref, wa_ref, wb_ref, wc_ref, ga_ref, gb_ref, gc_ref, y_ref):
    y = (ga_ref[...].astype(F32) * _dot(aa_ref[...], wa_ref[...])
         + gb_ref[...].astype(F32) * _dot(ab_ref[...], wb_ref[...])
         + gc_ref[...].astype(F32) * _dot(ac_ref[...], wc_ref[...]))
    y_ref[...] = y.astype(y_ref.dtype)


def _merge_call(aa, ab, ac, wa, wb, wc, g):
    m = aa.shape[0]
    d = wa.shape[1]
    tm = _row_tile(m, 1024)
    nd = d // TN
    a_spec = lambda w: pl.BlockSpec((tm, w), lambda i, n: (i, 0))
    w_spec = lambda w: pl.BlockSpec((w, TN), lambda i, n: (0, n))
    g_spec = lambda k: pl.BlockSpec((tm, TN), lambda i, n: (i, k * nd + n))
    return pl.pallas_call(
        _merge_kernel,
        grid=(m // tm, nd),
        in_specs=[a_spec(A_W), a_spec(B_QW), a_spec(C_W), w_spec(A_W), w_spec(B_QW), w_spec(C_W),
                  g_spec(0), g_spec(1), g_spec(2)],
        out_specs=pl.BlockSpec((tm, TN), lambda i, n: (i, n)),
        out_shape=jax.ShapeDtypeStruct((m, d), BF16),
        compiler_params=_params(2),
        name="merge",
    )(aa, ab, ac, wa, wb, wc, g, g, g)


def _outproj_kernel(y_ref, w_ref, x_ref, mod_ref, o_ref):
    gate = mod_ref[0, 2:3, :]
    o_ref[0] = x_ref[0] + gate * _dot(y_ref[0], w_ref[...])


def _outproj_call(y, w_out, x, mod):
    b, length, d = x.shape
    tm = _row_tile(length, 512)
    per_batch = mod.shape[0] != 1
    row = lambda bi, i: (bi, i, 0)
    return pl.pallas_call(
        _outproj_kernel,
        grid=(b, length // tm),
        in_specs=[
            pl.BlockSpec((1, tm, d), row),
            pl.BlockSpec((d, d), lambda bi, i: (0, 0)),
            pl.BlockSpec((1, tm, d), row),
            pl.BlockSpec((1, 3, d), (lambda bi, i: (bi, 0, 0)) if per_batch else (lambda bi, i: (0, 0, 0))),
        ],
        out_specs=pl.BlockSpec((1, tm, d), row),
        out_shape=jax.ShapeDtypeStruct((b, length, d), F32),
        compiler_params=_params(2),
        name="outproj",
    )(y, w_out, x, mod)


def _rope_tables(length):
    pos = jnp.arange(length)
    row = (pos // GRID_W).astype(F32)[:, None]
    col = (pos % GRID_W).astype(F32)[:, None]

    def cos_sin(rot_dim):
        n = rot_dim // 4
        inv_freq = ROPE_THETA ** (-jnp.arange(n, dtype=F32) / n)
        ang = jnp.concatenate([row * inv_freq, col * inv_freq], axis=-1)
        return jnp.cos(ang), jnp.sin(ang)

    cb, sb = cos_sin(HEAD_DIM)
    cc, sc = cos_sin(DC)
    return (jnp.concatenate([cb, cb], axis=-1), jnp.concatenate([-sb, sb], axis=-1),
            jnp.concatenate([cc, cc, cc, cc], axis=-1), jnp.concatenate([-sc, sc, -sc, sc], axis=-1))


def _identity_tables(length):
    one = jnp.ones((length, HEAD_DIM), F32)
    zero = jnp.zeros((length, HEAD_DIM), F32)
    return one, zero, one, zero


def kernel(x, c, ctx, c_ctx, norm_w, w_ada, b_ada, w_in, b_gate, rpb, q_norm_w, k_norm_w, lam_q1, lam_k1, lam_q2,
           lam_k2, subln_w, w_bo_a, w_bo_b, w_bo_c, w_out, final_norm_w):
    b, length, d = x.shape
    lc = ctx.shape[1]
    depth = w_in.shape[0]
    rows = length // GRID_W

    n_cond = b + 1
    pad = (-n_cond) % 8
    cc = jnp.concatenate([c, c_ctx[None], jnp.zeros((pad, d), F32)], axis=0)
    mod_all = _ada_call(cc, w_ada, b_ada).reshape(depth, n_cond + pad, 3, d)

    tabs_lat = _rope_tables(length)
    tabs_ctx = _identity_tables(_row_tile(b * lc, INPROJ_TM))
    nbr_bias = _nbr_bias_tables(rpb, rows)

    w_in_bf = w_in.astype(BF16)
    wa_bf, wb_bf, wc_bf, wo_bf = (w.astype(BF16) for w in (w_bo_a, w_bo_b, w_bo_c, w_out))

    xl, xc = x, ctx
    for l in range(depth):
        update_ctx = l < depth - 1
        mod_l = mod_all[l, :b]
        mod_c = mod_all[l, b:b + 1]
        h = _norm_call(xl, norm_w[l], mod_l, BF16)
        hc = _norm_call(xc, norm_w[l], mod_c, BF16)
        p, g = _inproj_call(h.reshape(b * length, d), w_in_bf[l], b_gate[l], q_norm_w[l], k_norm_w[l], tabs_lat)
        pc, gc = _inproj_call(hc.reshape(b * lc, d), w_in_bf[l], b_gate[l], q_norm_w[l], k_norm_w[l], tabs_ctx)
        p = p.reshape(b, length, P_W)
        pc = pc.reshape(b, lc, P_W)

        lam_init = 0.8 - 0.6 * float(np.exp(-0.3 * l))
        li = jnp.full((1, HEAD_DIM), lam_init, F32)
        lam_args = [v[l].reshape(1, DC) for v in (lam_q1, lam_k1, lam_q2, lam_k2)]
        sw = subln_w[l].reshape(1, HEAD_DIM)

        o_a = _nbr_call(p, pc, nbr_bias[l])
        o_b = _gqa_call(p, pc)
        o_c = _diff_call(p, pc, lam_args, li, sw)
        y = _merge_call(o_a.reshape(b * length, A_W), o_b.reshape(b * length, B_QW), o_c.reshape(b * length, C_W),
                        wa_bf[l], wb_bf[l], wc_bf[l], g)
        xl = _outproj_call(y.reshape(b, length, d), wo_bf[l], xl, mod_l)
        if update_ctx:
            oc_a, oc_b, oc_c = _ctx_attn_call(pc, lam_args, li, sw)
            yc = _merge_call(oc_a.reshape(b * lc, A_W), oc_b.reshape(b * lc, B_QW), oc_c.reshape(b * lc, C_W),
                             wa_bf[l], wb_bf[l], wc_bf[l], gc)
            xc = _outproj_call(yc.reshape(b, lc, d), wo_bf[l], xc, mod_c)
    return _norm_call(xl, final_norm_w, None, F32)
```

```python
import functools

import numpy as np
import jax
import jax.numpy as jnp
from jax import lax
from jax.experimental import pallas as pl
from jax.experimental.pallas import tpu as pltpu

F32 = jnp.float32
BF16 = jnp.bfloat16

GRID_W = 64
HEAD_DIM = 128
H_A = 4
WIN_H = 8
WIN_W = 16
H_B = 8
KV_B = 2
H_C = 4
DC = HEAD_DIM // 2
ROPE_THETA = 10000.0
EPS = 1e-6
NEG = -1e30
LOG2E = 1.4426950408889634

A_W = H_A * HEAD_DIM
B_QW = H_B * HEAD_DIM
B_KW = KV_B * HEAD_DIM
C_W = H_C * HEAD_DIM

OFF_AQ, OFF_AK, OFF_AV, OFF_AZ = 0, A_W, 2 * A_W, 3 * A_W
OFF_BQ = 4 * A_W
OFF_BK = OFF_BQ + B_QW
OFF_BV = OFF_BK + B_KW
OFF_BZ = OFF_BV + B_KW
OFF_CQ = OFF_BZ + B_QW
OFF_CK = OFF_CQ + C_W
OFF_CV = OFF_CK + C_W
OFF_CZ = OFF_CV + C_W
P_W = OFF_CZ + C_W

TN = 512
INPROJ_TM = 2048
NP_TILES = P_W // TN

NBR_R = 4
NBR_KR = 12

VMEM_LIMIT = 56 * 1024 * 1024


def _params(n_axes):
    return pltpu.CompilerParams(dimension_semantics=("arbitrary",) * n_axes, vmem_limit_bytes=VMEM_LIMIT)


def _sigmoid(x):
    return 1.0 / (1.0 + jnp.exp(-x))


def _dot(a, b):
    return jnp.dot(a, b, preferred_element_type=F32)


def _dot_nt(a, b):
    return lax.dot_general(a, b, (((1,), (1,)), ((), ())), preferred_element_type=F32)


def _head_rmsnorm(y, w):
    return y * lax.rsqrt(jnp.mean(y * y, axis=-1, keepdims=True) + EPS) * w


def _ada_kernel(c_ref, w_ref, b_ref, o_ref):
    a = c_ref[...]
    a = a * _sigmoid(a)
    o_ref[0] = _dot(a.astype(BF16), w_ref[0].astype(BF16)) + b_ref[0]


def _ada_call(cc, w_ada, b_ada):
    depth, d, n3 = w_ada.shape
    rows = cc.shape[0]
    tn = 1536
    return pl.pallas_call(
        _ada_kernel,
        grid=(depth, n3 // tn),
        in_specs=[
            pl.BlockSpec((rows, d), lambda l, j: (0, 0)),
            pl.BlockSpec((1, d, tn), lambda l, j: (l, 0, j)),
            pl.BlockSpec((1, 1, tn), lambda l, j: (l, 0, j)),
        ],
        out_specs=pl.BlockSpec((1, rows, tn), lambda l, j: (l, 0, j)),
        out_shape=jax.ShapeDtypeStruct((depth, rows, n3), F32),
        compiler_params=_params(2),
        name="ada",
    )(cc, w_ada, b_ada.reshape(depth, 1, n3))


def _rms(x):
    return x * lax.rsqrt(jnp.mean(x * x, axis=-1, keepdims=True) + EPS)


def _modulated_norm(x, w, mod):
    return _rms(x) * w * (1.0 + mod[1:2, :]) + mod[0:1, :]


def _norm_kernel(x_ref, w_ref, mod_ref, o_ref):
    o_ref[0] = _modulated_norm(x_ref[0], w_ref[...], mod_ref[0]).astype(o_ref.dtype)


def _row_tile(n, cap):
    t = min(n, cap)
    assert n % t == 0
    return t


def _mod_spec(mod):
    d = mod.shape[-1]
    return pl.BlockSpec((1, 3, d), (lambda bi, i: (bi, 0, 0)) if mod.shape[0] != 1 else (lambda bi, i: (0, 0, 0)))


def _norm_call(x, w, mod):
    b, length, d = x.shape
    tm = _row_tile(length, 512)
    row_spec = pl.BlockSpec((1, tm, d), lambda bi, i: (bi, i, 0))
    return pl.pallas_call(
        _norm_kernel,
        grid=(b, length // tm),
        in_specs=[row_spec, pl.BlockSpec((1, d), lambda bi, i: (0, 0)), _mod_spec(mod)],
        out_specs=row_spec,
        out_shape=jax.ShapeDtypeStruct((b, length, d), BF16),
        compiler_params=_params(2),
        name="norm",
    )(x, w.reshape(1, d), mod)


def _inproj_kernel(h_ref, w_ref, qw_ref, kw_ref, cb_ref, sb_ref, cc_ref, sc_ref, p_ref):
    j = pl.program_id(1)
    acc = _dot(h_ref[...], w_ref[...])
    scale_a = HEAD_DIM ** -0.5
    scale_b = HEAD_DIM ** -0.5 * LOG2E
    scale_c = DC ** -0.5 * LOG2E
    heads = TN // HEAD_DIM

    def rope_b(y):
        return y * cb_ref[...] + pltpu.roll(y, HEAD_DIM // 2, 1) * sb_ref[...]

    def rope_c(y):
        lane = lax.broadcasted_iota(jnp.int32, y.shape, 1)
        first = (lane & (DC // 2)) == 0
        rot = jnp.where(first, pltpu.roll(y, HEAD_DIM - DC // 2, 1), pltpu.roll(y, DC // 2, 1))
        return y * cc_ref[...] + rot * sc_ref[...]

    def head(hh):
        return acc[:, hh * HEAD_DIM:(hh + 1) * HEAD_DIM]

    def put(hh, y):
        p_ref[:, hh * HEAD_DIM:(hh + 1) * HEAD_DIM] = y.astype(p_ref.dtype)

    t_aq = OFF_AQ // TN
    t_bq0, t_bq1 = OFF_BQ // TN, OFF_BK // TN
    t_bkv = OFF_BK // TN
    t_cq, t_ck = OFF_CQ // TN, OFF_CK // TN

    p_ref[...] = (acc * jnp.where(j == t_aq, scale_a, 1.0)).astype(p_ref.dtype)

    @pl.when((j >= t_bq0) & (j < t_bq1))
    def _():
        for hh in range(heads):
            put(hh, rope_b(_head_rmsnorm(head(hh), qw_ref[...])) * scale_b)

    @pl.when(j == t_bkv)
    def _():
        for hh in range(KV_B):
            put(hh, rope_b(_head_rmsnorm(head(hh), kw_ref[...])))
        for hh in range(KV_B, heads):
            put(hh, head(hh))

    @pl.when(j == t_cq)
    def _():
        for hh in range(heads):
            put(hh, rope_c(head(hh)) * scale_c)

    @pl.when(j == t_ck)
    def _():
        for hh in range(heads):
            put(hh, rope_c(head(hh)))


def _gates_kernel(h_ref, w_ref, bg_ref, g_ref):
    g_ref[...] = _sigmoid(_dot(h_ref[...], w_ref[...]) + bg_ref[...]).astype(g_ref.dtype)


def _inproj_call(h, w_in, b_gate, qw, kw, tabs):
    m, d = h.shape
    n_in = w_in.shape[1]
    gate_w = n_in - P_W
    assert B_KW == TN // 2 and gate_w % TN == 0
    ltab = tabs[0].shape[0]
    tm = _row_tile(ltab, INPROJ_TM)
    assert m % tm == 0
    nt = ltab // tm
    h_spec = pl.BlockSpec((tm, d), lambda i, j: (i, 0))
    out_spec = pl.BlockSpec((tm, TN), lambda i, j: (i, j))
    tab_spec = pl.BlockSpec((tm, HEAD_DIM), lambda i, j: (i % nt, 0))
    vec_spec = pl.BlockSpec((1, HEAD_DIM), lambda i, j: (0, 0))
    p = pl.pallas_call(
        _inproj_kernel,
        grid=(m // tm, NP_TILES),
        in_specs=[h_spec, pl.BlockSpec((d, TN), lambda i, j: (0, j)),
                  vec_spec, vec_spec, tab_spec, tab_spec, tab_spec, tab_spec],
        out_specs=out_spec,
        out_shape=jax.ShapeDtypeStruct((m, P_W), BF16),
        compiler_params=_params(2),
        name="inproj",
    )(h, w_in, qw.reshape(1, HEAD_DIM), kw.reshape(1, HEAD_DIM), *tabs)
    g = pl.pallas_call(
        _gates_kernel,
        grid=(m // tm, gate_w // TN),
        in_specs=[h_spec, pl.BlockSpec((d, TN), lambda i, j: (0, NP_TILES + j)),
                  pl.BlockSpec((1, TN), lambda i, j: (0, j))],
        out_specs=out_spec,
        out_shape=jax.ShapeDtypeStruct((m, gate_w), BF16),
        compiler_params=_params(2),
        name="gates",
    )(h, w_in, b_gate.reshape(1, gate_w))
    return p, g


def _softmax2(s_a, s_b):
    m = jnp.maximum(jnp.max(s_a, axis=-1, keepdims=True), jnp.max(s_b, axis=-1, keepdims=True))
    e_a = jnp.exp(s_a - m)
    e_b = jnp.exp(s_b - m)
    return e_a, e_b, jnp.sum(e_a, axis=-1, keepdims=True) + jnp.sum(e_b, axis=-1, keepdims=True)


def _softmax1(s, exp=jnp.exp):
    m = jnp.max(s, axis=-1, keepdims=True)
    e = exp(s - m)
    return e, jnp.sum(e, axis=-1, keepdims=True)


def _silu_gate(o, z):
    z = z.astype(F32)
    return o * (z * _sigmoid(z))


def _split_maps(q):
    lane = lax.broadcasted_iota(jnp.int32, q.shape, 1)
    zero = jnp.zeros_like(q)
    return jnp.where(lane < DC, q, zero), jnp.where(lane >= DC, q, zero)


def _lam(lq1_ref, lk1_ref, lq2_ref, lk2_ref, li_ref):
    return (jnp.exp(jnp.sum(lq1_ref[...] * lk1_ref[...], axis=-1, keepdims=True))
            - jnp.exp(jnp.sum(lq2_ref[...] * lk2_ref[...], axis=-1, keepdims=True)) + li_ref[:, 0:1])


UNITS = 4
ATTN_TQ = 512


def _scores_stage(q, kl, kc, s_ref, m_ref):
    length = kl.shape[0]
    s_l = _dot_nt(q, kl)
    s_c = _dot_nt(q, kc)
    s_ref[:, :length] = s_l
    s_ref[:, length:] = s_c
    m = jnp.maximum(jnp.max(s_l, axis=-1, keepdims=True), jnp.max(s_c, axis=-1, keepdims=True))
    m_ref[...] = jnp.broadcast_to(m, m_ref.shape)


def _softmax_pv_stage(s_ref, m_ref, p_ref, v_ext):
    n_blk = s_ref.shape[1] // HEAD_DIM
    m = m_ref[...]
    for k in range(n_blk):
        blk = slice(k * HEAD_DIM, (k + 1) * HEAD_DIM)
        p_ref[:, blk] = jnp.exp2((s_ref[:, blk] - m).astype(p_ref.dtype))
    o_ext = _dot(p_ref[...], v_ext)
    return o_ext[:, :HEAD_DIM], o_ext[:, HEAD_DIM:HEAD_DIM + 1]


def _fill_v_ext(v_ext_ref, vl, vc):
    length = vl.shape[0]
    v_ext_ref[:length, :HEAD_DIM] = vl
    v_ext_ref[length:, :HEAD_DIM] = vc
    v_ext_ref[:, HEAD_DIM:] = jnp.ones((v_ext_ref.shape[0], HEAD_DIM), v_ext_ref.dtype)


def _pipelined_units(first_step, prepare, unit_q, next_tile_q, unit_k, unit_v, finish, scratch):
    s_bufs, m_bufs, p_bufs = scratch[0:2], scratch[2:4], scratch[4:6]

    @pl.when(first_step)
    def _():
        prepare()
        _scores_stage(unit_q(0), *unit_k(0), s_bufs[0], m_bufs[0])

    for r in range(UNITS):
        cur, nxt = r % 2, (r + 1) % 2
        if r + 1 < UNITS:
            _scores_stage(unit_q(r + 1), *unit_k(r + 1), s_bufs[nxt], m_bufs[nxt])
        else:
            _scores_stage(next_tile_q(), *unit_k(0), s_bufs[nxt], m_bufs[nxt])
        o, l = _softmax_pv_stage(s_bufs[cur], m_bufs[cur], p_bufs[cur], unit_v(r))
        finish(r, o, l)


def _pipeline_scratch(tq, lk, n_v):
    return ([pltpu.VMEM((tq, lk), F32)] * 2 + [pltpu.VMEM((tq, HEAD_DIM), F32)] * 2 + [pltpu.VMEM((tq, lk), BF16)] * 2
            + [pltpu.VMEM((n_v, lk, 2 * HEAD_DIM), BF16)])


def _gqa_kernel(q_ref, qn_ref, kl_ref, vl_ref, kc_ref, vc_ref, z_ref, o_ref, *scratch):
    def cols(r):
        return slice(r * HEAD_DIM, (r + 1) * HEAD_DIM)

    def finish(r, o, l):
        o_ref[0, :, cols(r)] = _silu_gate(o / l, z_ref[0, :, cols(r)]).astype(o_ref.dtype)

    v_ext_ref = scratch[-1]
    _pipelined_units(
        pl.program_id(2) == 0,
        lambda: _fill_v_ext(v_ext_ref.at[0], vl_ref[0], vc_ref[0]),
        lambda r: q_ref[0, :, cols(r)],
        lambda: qn_ref[0],
        lambda r: (kl_ref[0], kc_ref[0]),
        lambda r: v_ext_ref[0],
        finish, scratch)


def _gqa_call(p, pc):
    b, length, _ = p.shape
    lc = pc.shape[1]
    tq = _row_tile(length, ATTN_TQ)
    n_i = length // tq
    assert H_B // KV_B == UNITS
    gw = UNITS * HEAD_DIM
    kv = lambda n, off: pl.BlockSpec((1, n, HEAD_DIM), lambda bi, g, i: (bi, 0, off // HEAD_DIM + g))
    return pl.pallas_call(
        _gqa_kernel,
        grid=(b, KV_B, n_i),
        in_specs=[
            pl.BlockSpec((1, tq, gw), lambda bi, g, i: (bi, i, OFF_BQ // gw + g)),
            pl.BlockSpec((1, tq, HEAD_DIM),
                         lambda bi, g, i: (bi, jnp.minimum(i + 1, n_i - 1), OFF_BQ // HEAD_DIM + UNITS * g)),
            kv(length, OFF_BK), kv(length, OFF_BV), kv(lc, OFF_BK), kv(lc, OFF_BV),
            pl.BlockSpec((1, tq, gw), lambda bi, g, i: (bi, i, OFF_BZ // gw + g)),
        ],
        out_specs=pl.BlockSpec((1, tq, gw), lambda bi, g, i: (bi, i, g)),
        out_shape=jax.ShapeDtypeStruct((b, length, B_QW), BF16),
        scratch_shapes=_pipeline_scratch(tq, length + lc, 1),
        compiler_params=_params(3),
        name="gqa",
    )(p, p, p, p, pc, pc, p)


def _diff_core(q, k, v, lam, sw, li):
    q1, q2 = _split_maps(q)
    e1, l1 = _softmax1(_dot_nt(q1, k), jnp.exp2)
    e2, l2 = _softmax1(_dot_nt(q2, k), jnp.exp2)
    a = (e1 * (1.0 / l1) - e2 * (lam / l2)).astype(BF16)
    o = _head_rmsnorm(_dot(a, v), sw)
    return o * (1.0 - li)


def _diff_kernel(q_ref, qn_ref, kl_ref, vl_ref, kc_ref, vc_ref, z_ref, lq1_ref, lk1_ref, lq2_ref, lk2_ref, li_ref,
                 sw_ref, o_ref, *scratch):
    lam = _lam(lq1_ref, lk1_ref, lq2_ref, lk2_ref, li_ref)
    pending = {}

    def cols(hh):
        return slice(hh * HEAD_DIM, (hh + 1) * HEAD_DIM)

    def finish(r, o, l):
        hh = r // 2
        if r % 2 == 0:
            pending[hh] = o / l
            return
        o = pending.pop(hh) - o * (lam / l)
        o = _head_rmsnorm(o, sw_ref[...]) * (1.0 - li_ref[...])
        o_ref[0, :, cols(hh)] = _silu_gate(o, z_ref[0, :, cols(hh)]).astype(o_ref.dtype)

    v_ext_ref = scratch[-1]

    def prepare():
        for hh in range(UNITS // 2):
            _fill_v_ext(v_ext_ref.at[hh], vl_ref[0, :, cols(hh)], vc_ref[0, :, cols(hh)])

    _pipelined_units(
        pl.program_id(2) == 0,
        prepare,
        lambda r: _split_maps(q_ref[0, :, cols(r // 2)])[r % 2],
        lambda: _split_maps(qn_ref[0])[0],
        lambda r: (kl_ref[0, :, cols(r // 2)], kc_ref[0, :, cols(r // 2)]),
        lambda r: v_ext_ref[r // 2],
        finish, scratch)


def _diff_call(p, pc, lam_args, li, sw):
    b, length, _ = p.shape
    lc = pc.shape[1]
    tq = _row_tile(length, ATTN_TQ)
    n_i = length // tq
    heads = UNITS // 2
    assert H_C % heads == 0
    pw = heads * HEAD_DIM
    small = lambda w: pl.BlockSpec((1, w), lambda bi, g, i: (0, 0))
    kv = lambda n, off: pl.BlockSpec((1, n, pw), lambda bi, g, i: (bi, 0, off // pw + g))
    return pl.pallas_call(
        _diff_kernel,
        grid=(b, H_C // heads, n_i),
        in_specs=[
            pl.BlockSpec((1, tq, pw), lambda bi, g, i: (bi, i, OFF_CQ // pw + g)),
            pl.BlockSpec((1, tq, HEAD_DIM),
                         lambda bi, g, i: (bi, jnp.minimum(i + 1, n_i - 1), OFF_CQ // HEAD_DIM + heads * g)),
            kv(length, OFF_CK), kv(length, OFF_CV), kv(lc, OFF_CK), kv(lc, OFF_CV),
            pl.BlockSpec((1, tq, pw), lambda bi, g, i: (bi, i, OFF_CZ // pw + g)),
            small(DC), small(DC), small(DC), small(DC), small(HEAD_DIM), small(HEAD_DIM),
        ],
        out_specs=pl.BlockSpec((1, tq, pw), lambda bi, g, i: (bi, i, g)),
        out_shape=jax.ShapeDtypeStruct((b, length, C_W), BF16),
        scratch_shapes=_pipeline_scratch(tq, length + lc, heads),
        compiler_params=_params(3),
        name="diff",
    )(p, p, p, p, pc, pc, p, *lam_args, li, sw)


def _nbr_key_row0(t, rows):
    return jnp.clip(t * NBR_R - WIN_H // 2, 0, rows - NBR_KR)


def _nbr_kernel(q_ref, kl_ref, vl_ref, kc_ref, vc_ref, z_ref, bias_ref, o_ref, *, rows):
    t = pl.program_id(1)
    span = NBR_KR * GRID_W
    k_off = pl.multiple_of(_nbr_key_row0(t, rows) * GRID_W, NBR_R * GRID_W)
    for hh in range(H_A):
        cols = slice(hh * HEAD_DIM, (hh + 1) * HEAD_DIM)
        q = q_ref[0, :, cols]
        s_win = _dot_nt(q, kl_ref[0, pl.ds(k_off, span), cols]) + bias_ref[hh, 0]
        s_ctx = _dot_nt(q, kc_ref[0, :, cols])
        e_w, e_c, l = _softmax2(s_win, s_ctx)
        o = _dot(e_w.astype(BF16), vl_ref[0, pl.ds(k_off, span), cols]) + _dot(e_c.astype(BF16), vc_ref[0, :, cols])
        o_ref[0, :, cols] = _silu_gate(o / l, z_ref[0, :, cols]).astype(o_ref.dtype)


def _nbr_bias_selectors(rows):
    nt = rows // NBR_R
    kh = min(WIN_H, rows)
    n_dr, n_dc = 2 * WIN_H - 1, 2 * WIN_W - 1
    qc = np.arange(GRID_W)[:, None]
    kc = np.arange(GRID_W)[None, :]
    dc = np.clip(kc - qc, -(WIN_W - 1), WIN_W - 1) + (WIN_W - 1)
    cs = np.clip(qc - WIN_W // 2, 0, GRID_W - WIN_W)
    in_win = (kc >= cs) & (kc < cs + WIN_W)
    sel_dc = (dc[None] == np.arange(n_dc)[:, None, None]).astype(np.float32)
    i = np.arange(NBR_R)[:, None]
    jr = np.arange(NBR_KR)[None, :]
    sel_dr, in_band = [], []
    for t in (0, 1, nt - 1):
        r0 = NBR_R * t
        k0 = int(np.clip(r0 - WIN_H // 2, 0, rows - NBR_KR))
        rq = r0 + i
        rk = k0 + jr
        rs = np.clip(rq - kh // 2, 0, rows - kh)
        band = (rk >= rs) & (rk < rs + kh)
        dr = np.clip(rk - rq + (WIN_H - 1), 0, n_dr - 1)
        sel_dr.append((dr[..., None] == np.arange(n_dr)) & band[..., None])
        in_band.append(band)
    sel_dr = np.stack(sel_dr).astype(np.float32)
    valid = np.stack(in_band)[:, :, None, :, None] & in_win[None, None, :, None, :]
    return sel_dc, sel_dr, valid


def _nbr_bias_tables(rpb, rows):
    sel_dc, sel_dr, valid = _nbr_bias_selectors(rows)
    by_col = jnp.einsum("lhrd,dqk->lhrqk", rpb, sel_dc, precision=lax.Precision.HIGHEST)
    bias = jnp.einsum("tijr,lhrqk->lhtiqjk", sel_dr, by_col, precision=lax.Precision.HIGHEST)
    bias = jnp.where(valid[None, None], bias, NEG)
    return bias.reshape(rpb.shape[0], H_A, 3, NBR_R * GRID_W, NBR_KR * GRID_W)


def _nbr_call(p, pc, bias):
    b, length, _ = p.shape
    lc = pc.shape[1]
    rows = length // GRID_W
    assert rows % NBR_R == 0 and rows >= NBR_KR
    tq = NBR_R * GRID_W
    nt = rows // NBR_R
    blk = lambda n, col: pl.BlockSpec((1, n, A_W), lambda bi, t: (bi, 0, col))
    tile = lambda col: pl.BlockSpec((1, tq, A_W), lambda bi, t: (bi, t, col))
    bias_type = lambda bi, t: (0, jnp.where(t == 0, 0, jnp.where(t == nt - 1, 2, 1)), 0, 0)
    return pl.pallas_call(
        functools.partial(_nbr_kernel, rows=rows),
        grid=(b, nt),
        in_specs=[
            tile(OFF_AQ // A_W), blk(length, OFF_AK // A_W), blk(length, OFF_AV // A_W),
            blk(lc, OFF_AK // A_W), blk(lc, OFF_AV // A_W), tile(OFF_AZ // A_W),
            pl.BlockSpec((H_A, 1, tq, NBR_KR * GRID_W), bias_type),
        ],
        out_specs=pl.BlockSpec((1, tq, A_W), lambda bi, t: (bi, t, 0)),
        out_shape=jax.ShapeDtypeStruct((b, length, A_W), BF16),
        compiler_params=_params(2),
        name="nbr",
    )(p, p, p, pc, pc, p, bias)


def _ctx_attn_kernel(p_ref, lq1_ref, lk1_ref, lq2_ref, lk2_ref, li_ref, sw_ref, oa_ref, ob_ref, oc_ref):
    def col(off, hh):
        return p_ref[0, :, off + hh * HEAD_DIM: off + (hh + 1) * HEAD_DIM]

    def plain(q, k, v, exp):
        e, l = _softmax1(_dot_nt(q, k), exp)
        return _dot(e.astype(BF16), v) / l

    for hh in range(H_A):
        o = plain(col(OFF_AQ, hh), col(OFF_AK, hh), col(OFF_AV, hh), jnp.exp)
        oa_ref[0, :, hh * HEAD_DIM:(hh + 1) * HEAD_DIM] = _silu_gate(o, col(OFF_AZ, hh)).astype(oa_ref.dtype)
    rep = H_B // KV_B
    for hh in range(H_B):
        o = plain(col(OFF_BQ, hh), col(OFF_BK, hh // rep), col(OFF_BV, hh // rep), jnp.exp2)
        ob_ref[0, :, hh * HEAD_DIM:(hh + 1) * HEAD_DIM] = _silu_gate(o, col(OFF_BZ, hh)).astype(ob_ref.dtype)
    lam = _lam(lq1_ref, lk1_ref, lq2_ref, lk2_ref, li_ref)
    for hh in range(H_C):
        o = _diff_core(col(OFF_CQ, hh), col(OFF_CK, hh), col(OFF_CV, hh), lam, sw_ref[...], li_ref[...])
        oc_ref[0, :, hh * HEAD_DIM:(hh + 1) * HEAD_DIM] = _silu_gate(o, col(OFF_CZ, hh)).astype(oc_ref.dtype)


def _ctx_attn_call(pc, lam_args, li, sw):
    b, lc, _ = pc.shape
    small = lambda w: pl.BlockSpec((1, w), lambda bi: (0, 0))
    out = lambda w: pl.BlockSpec((1, lc, w), lambda bi: (bi, 0, 0))
    return pl.pallas_call(
        _ctx_attn_kernel,
        grid=(b,),
        in_specs=[pl.BlockSpec((1, lc, P_W), lambda bi: (bi, 0, 0)),
                  small(DC), small(DC), small(DC), small(DC), small(HEAD_DIM), small(HEAD_DIM)],
        out_specs=[out(A_W), out(B_QW), out(C_W)],
        out_shape=[jax.ShapeDtypeStruct((b, lc, A_W), BF16), jax.ShapeDtypeStruct((b, lc, B_QW), BF16),
                   jax.ShapeDtypeStruct((b, lc, C_W), BF16)],
        compiler_params=_params(1),
        name="ctx_attn",
    )(pc, *lam_args, li, sw)


def _merge_kernel(aa_ref, ab_ref, ac_ref, wa_ref, wb_ref, wc_ref, ga_ref, gb_ref, gc_ref, y_ref):
    y = (ga_ref[...].astype(F32) * _dot(aa_ref[...], wa_ref[...])
         + gb_ref[...].astype(F32) * _dot(ab_ref[...], wb_ref[...])
         + gc_ref[...].astype(F32) * _dot(ac_ref[...], wc_ref[...]))
    y_ref[...] = y.astype(y_ref.dtype)


def _merge_call(aa, ab, ac, wa, wb, wc, g):
    m = aa.shape[0]
    d = wa.shape[1]
    tm = _row_tile(m, 1024)
    nd = d // TN
    a_spec = lambda w: pl.BlockSpec((tm, w), lambda i, n: (i, 0))
    w_spec = lambda w: pl.BlockSpec((w, TN), lambda i, n: (0, n))
    g_spec = lambda k: pl.BlockSpec((tm, TN), lambda i, n: (i, k * nd + n))
    return pl.pallas_call(
        _merge_kernel,
        grid=(m // tm, nd),
        in_specs=[a_spec(A_W), a_spec(B_QW), a_spec(C_W), w_spec(A_W), w_spec(B_QW), w_spec(C_W),
                  g_spec(0), g_spec(1), g_spec(2)],
        out_specs=pl.BlockSpec((tm, TN), lambda i, n: (i, n)),
        out_shape=jax.ShapeDtypeStruct((m, d), BF16),
        compiler_params=_params(2),
        name="merge",
    )(aa, ab, ac, wa, wb, wc, g, g, g)


def _residual(y_ref, w_ref, x_ref, mod_ref):
    gate = mod_ref[0, 2:3, :]
    return x_ref[0] + gate * _dot(y_ref[0], w_ref[...])


def _outproj_next_kernel(y_ref, w_ref, x_ref, mod_ref, nw_ref, modn_ref, xo_ref, ho_ref):
    x = _residual(y_ref, w_ref, x_ref, mod_ref)
    xo_ref[0] = x
    ho_ref[0] = _modulated_norm(x, nw_ref[...], modn_ref[0]).astype(ho_ref.dtype)


def _outproj_final_kernel(y_ref, w_ref, x_ref, mod_ref, nw_ref, o_ref):
    o_ref[0] = _rms(_residual(y_ref, w_ref, x_ref, mod_ref)) * nw_ref[...]


def _outproj_call(y, w_out, x, mod, nw, mod_next):
    b, length, d = x.shape
    tm = _row_tile(length, 256)
    row_spec = pl.BlockSpec((1, tm, d), lambda bi, i: (bi, i, 0))
    in_specs = [row_spec, pl.BlockSpec((d, d), lambda bi, i: (0, 0)), row_spec, _mod_spec(mod),
                pl.BlockSpec((1, d), lambda bi, i: (0, 0))]
    args = [y, w_out, x, mod, nw.reshape(1, d)]
    if mod_next is None:
        return pl.pallas_call(
            _outproj_final_kernel, grid=(b, length // tm), in_specs=in_specs, out_specs=row_spec,
            out_shape=jax.ShapeDtypeStruct((b, length, d), F32), compiler_params=_params(2), name="outproj_final",
        )(*args)
    return pl.pallas_call(
        _outproj_next_kernel, grid=(b, length // tm), in_specs=in_specs + [_mod_spec(mod_next)],
        out_specs=[row_spec, row_spec],
        out_shape=[jax.ShapeDtypeStruct((b, length, d), F32), jax.ShapeDtypeStruct((b, length, d), BF16)],
        compiler_params=_params(2), name="outproj",
    )(*args, mod_next)


def _rope_tables(length):
    pos = jnp.arange(length)
    row = (pos // GRID_W).astype(F32)[:, None]
    col = (pos % GRID_W).astype(F32)[:, None]

    def cos_sin(rot_dim):
        n = rot_dim // 4
        inv_freq = ROPE_THETA ** (-jnp.arange(n, dtype=F32) / n)
        ang = jnp.concatenate([row * inv_freq, col * inv_freq], axis=-1)
        return jnp.cos(ang), jnp.sin(ang)

    cb, sb = cos_sin(HEAD_DIM)
    cc, sc = cos_sin(DC)
    return (jnp.concatenate([cb, cb], axis=-1), jnp.concatenate([-sb, sb], axis=-1),
            jnp.concatenate([cc, cc, cc, cc], axis=-1), jnp.concatenate([-sc, sc, -sc, sc], axis=-1))


def _identity_tables(length):
    one = jnp.ones((length, HEAD_DIM), F32)
    zero = jnp.zeros((length, HEAD_DIM), F32)
    return one, zero, one, zero


def kernel(x, c, ctx, c_ctx, norm_w, w_ada, b_ada, w_in, b_gate, rpb, q_norm_w, k_norm_w, lam_q1, lam_k1, lam_q2,
           lam_k2, subln_w, w_bo_a, w_bo_b, w_bo_c, w_out, final_norm_w):
    b, length, d = x.shape
    lc = ctx.shape[1]
    depth = w_in.shape[0]
    rows = length // GRID_W

    n_cond = b + 1
    pad = (-n_cond) % 8
    cc = jnp.concatenate([c, c_ctx[None], jnp.zeros((pad, d), F32)], axis=0)
    mod_all = _ada_call(cc, w_ada, b_ada).reshape(depth, n_cond + pad, 3, d)

    tabs_lat = _rope_tables(length)
    tabs_ctx = _identity_tables(_row_tile(b * lc, INPROJ_TM))
    nbr_bias = _nbr_bias_tables(rpb, rows)

    w_in_bf = w_in.astype(BF16)
    wa_bf, wb_bf, wc_bf, wo_bf = (w.astype(BF16) for w in (w_bo_a, w_bo_b, w_bo_c, w_out))

    xl, xc = x, ctx
    h = _norm_call(xl, norm_w[0], mod_all[0, :b])
    hc = _norm_call(xc, norm_w[0], mod_all[0, b:b + 1])
    out = None
    for l in range(depth):
        last = l == depth - 1
        mod_l = mod_all[l, :b]
        mod_c = mod_all[l, b:b + 1]
        p, g = _inproj_call(h.reshape(b * length, d), w_in_bf[l], b_gate[l], q_norm_w[l], k_norm_w[l], tabs_lat)
        pc, gc = _inproj_call(hc.reshape(b * lc, d), w_in_bf[l], b_gate[l], q_norm_w[l], k_norm_w[l], tabs_ctx)
        p = p.reshape(b, length, P_W)
        pc = pc.reshape(b, lc, P_W)

        lam_init = 0.8 - 0.6 * float(np.exp(-0.3 * l))
        li = jnp.full((1, HEAD_DIM), lam_init, F32)
        lam_args = [v[l].reshape(1, DC) for v in (lam_q1, lam_k1, lam_q2, lam_k2)]
        sw = subln_w[l].reshape(1, HEAD_DIM)

        o_a = _nbr_call(p, pc, nbr_bias[l])
        o_b = _gqa_call(p, pc)
        o_c = _diff_call(p, pc, lam_args, li, sw)
        y = _merge_call(o_a.reshape(b * length, A_W), o_b.reshape(b * length, B_QW), o_c.reshape(b * length, C_W),
                        wa_bf[l], wb_bf[l], wc_bf[l], g)
        if last:
            out = _outproj_call(y.reshape(b, length, d), wo_bf[l], xl, mod_l, final_norm_w, None)
        else:
            xl, h = _outproj_call(y.reshape(b, length, d), wo_bf[l], xl, mod_l, norm_w[l + 1], mod_all[l + 1, :b])
            oc_a, oc_b, oc_c = _ctx_attn_call(pc, lam_args, li, sw)
            yc = _merge_call(oc_a.reshape(b * lc, A_W), oc_b.reshape(b * lc, B_QW), oc_c.reshape(b * lc, C_W),
                             wa_bf[l], wb_bf[l], wc_bf[l], gc)
            xc, hc = _outproj_call(yc.reshape(b, lc, d), wo_bf[l], xc, mod_c, norm_w[l + 1],
                                   mod_all[l + 1, b:b + 1])
    return out
```

```python
import functools

import numpy as np
import jax
import jax.numpy as jnp
from jax import lax
from jax.experimental import pallas as pl
from jax.experimental.pallas import tpu as pltpu

F32 = jnp.float32
BF16 = jnp.bfloat16

GRID_W = 64
HEAD_DIM = 128
H_A = 4
WIN_H = 8
WIN_W = 16
H_B = 8
KV_B = 2
H_C = 4
DC = HEAD_DIM // 2
ROPE_THETA = 10000.0
EPS = 1e-6
NEG = -1e30
LOG2E = 1.4426950408889634

A_W = H_A * HEAD_DIM
B_QW = H_B * HEAD_DIM
B_KW = KV_B * HEAD_DIM
C_W = H_C * HEAD_DIM

OFF_AQ, OFF_AK, OFF_AV, OFF_AZ = 0, A_W, 2 * A_W, 3 * A_W
OFF_BQ = 4 * A_W
OFF_BK = OFF_BQ + B_QW
OFF_BV = OFF_BK + B_KW
OFF_BZ = OFF_BV + B_KW
OFF_CQ = OFF_BZ + B_QW
OFF_CK = OFF_CQ + C_W
OFF_CV = OFF_CK + C_W
OFF_CZ = OFF_CV + C_W
P_W = OFF_CZ + C_W

TN = 512
TN_GATES = 1024
INPROJ_TM = 2048
NP_TILES = P_W // TN

NBR_R = 4
NBR_KR = 12

VMEM_LIMIT = 56 * 1024 * 1024


def _params(n_axes):
    return pltpu.CompilerParams(dimension_semantics=("arbitrary",) * n_axes, vmem_limit_bytes=VMEM_LIMIT)


def _sigmoid(x):
    return 1.0 / (1.0 + jnp.exp(-x))


def _dot(a, b):
    return jnp.dot(a, b, preferred_element_type=F32)


def _dot_nt(a, b):
    return lax.dot_general(a, b, (((1,), (1,)), ((), ())), preferred_element_type=F32)


def _head_rmsnorm(y, w):
    return y * lax.rsqrt(jnp.mean(y * y, axis=-1, keepdims=True) + EPS) * w


def _ada_kernel(c_ref, w_ref, b_ref, o_ref):
    a = c_ref[...]
    a = a * _sigmoid(a)
    o_ref[0] = _dot(a.astype(BF16), w_ref[0].astype(BF16)) + b_ref[0]


def _ada_call(cc, w_ada, b_ada):
    depth, d, n3 = w_ada.shape
    rows = cc.shape[0]
    tn = 1536
    return pl.pallas_call(
        _ada_kernel,
        grid=(depth, n3 // tn),
        in_specs=[
            pl.BlockSpec((rows, d), lambda l, j: (0, 0)),
            pl.BlockSpec((1, d, tn), lambda l, j: (l, 0, j)),
            pl.BlockSpec((1, 1, tn), lambda l, j: (l, 0, j)),
        ],
        out_specs=pl.BlockSpec((1, rows, tn), lambda l, j: (l, 0, j)),
        out_shape=jax.ShapeDtypeStruct((depth, rows, n3), F32),
        compiler_params=_params(2),
        name="ada",
    )(cc, w_ada, b_ada.reshape(depth, 1, n3))


def _rms(x):
    return x * lax.rsqrt(jnp.mean(x * x, axis=-1, keepdims=True) + EPS)


def _modulated_norm(x, w, mod):
    return _rms(x) * w * (1.0 + mod[1:2, :]) + mod[0:1, :]


def _norm_kernel(x_ref, w_ref, mod_ref, o_ref):
    o_ref[0] = _modulated_norm(x_ref[0], w_ref[...], mod_ref[0]).astype(o_ref.dtype)


def _row_tile(n, cap):
    t = min(n, cap)
    assert n % t == 0
    return t


def _mod_spec(mod):
    d = mod.shape[-1]
    return pl.BlockSpec((1, 3, d), (lambda bi, i: (bi, 0, 0)) if mod.shape[0] != 1 else (lambda bi, i: (0, 0, 0)))


def _norm_call(x, w, mod):
    b, length, d = x.shape
    tm = _row_tile(length, 512)
    row_spec = pl.BlockSpec((1, tm, d), lambda bi, i: (bi, i, 0))
    return pl.pallas_call(
        _norm_kernel,
        grid=(b, length // tm),
        in_specs=[row_spec, pl.BlockSpec((1, d), lambda bi, i: (0, 0)), _mod_spec(mod)],
        out_specs=row_spec,
        out_shape=jax.ShapeDtypeStruct((b, length, d), BF16),
        compiler_params=_params(2),
        name="norm",
    )(x, w.reshape(1, d), mod)


def _inproj_kernel(h_ref, w_ref, qw_ref, kw_ref, cb_ref, sb_ref, cc_ref, sc_ref, p_ref):
    j = pl.program_id(1)
    acc = _dot(h_ref[...], w_ref[...])
    scale_a = HEAD_DIM ** -0.5
    scale_b = HEAD_DIM ** -0.5 * LOG2E
    scale_c = DC ** -0.5 * LOG2E
    heads = TN // HEAD_DIM

    def rope_b(y):
        return y * cb_ref[...] + pltpu.roll(y, HEAD_DIM // 2, 1) * sb_ref[...]

    def rope_c(y):
        lane = lax.broadcasted_iota(jnp.int32, y.shape, 1)
        first = (lane & (DC // 2)) == 0
        rot = jnp.where(first, pltpu.roll(y, HEAD_DIM - DC // 2, 1), pltpu.roll(y, DC // 2, 1))
        return y * cc_ref[...] + rot * sc_ref[...]

    def head(hh):
        return acc[:, hh * HEAD_DIM:(hh + 1) * HEAD_DIM]

    def put(hh, y):
        p_ref[:, hh * HEAD_DIM:(hh + 1) * HEAD_DIM] = y.astype(p_ref.dtype)

    t_aq = OFF_AQ // TN
    t_bq0, t_bq1 = OFF_BQ // TN, OFF_BK // TN
    t_bkv = OFF_BK // TN
    t_cq, t_ck = OFF_CQ // TN, OFF_CK // TN

    p_ref[...] = (acc * jnp.where(j == t_aq, scale_a, 1.0)).astype(p_ref.dtype)

    @pl.when((j >= t_bq0) & (j < t_bq1))
    def _():
        for hh in range(heads):
            put(hh, rope_b(_head_rmsnorm(head(hh), qw_ref[...])) * scale_b)

    @pl.when(j == t_bkv)
    def _():
        for hh in range(KV_B):
            put(hh, rope_b(_head_rmsnorm(head(hh), kw_ref[...])))
        for hh in range(KV_B, heads):
            put(hh, head(hh))

    @pl.when(j == t_cq)
    def _():
        for hh in range(heads):
            put(hh, rope_c(head(hh)) * scale_c)

    @pl.when(j == t_ck)
    def _():
        for hh in range(heads):
            put(hh, rope_c(head(hh)))


def _gates_kernel(h_ref, w_ref, bg_ref, g_ref):
    g_ref[...] = _sigmoid(_dot(h_ref[...], w_ref[...]) + bg_ref[...]).astype(g_ref.dtype)


def _inproj_call(h, w_p, w_g, b_gate, qw, kw, tabs):
    m, d = h.shape
    gate_w = w_g.shape[1]
    assert w_p.shape[1] == P_W and B_KW == TN // 2 and gate_w % TN_GATES == 0
    ltab = tabs[0].shape[0]
    tm = _row_tile(ltab, INPROJ_TM)
    assert m % tm == 0
    nt = ltab // tm
    h_spec = pl.BlockSpec((tm, d), lambda i, j: (i, 0))
    out_spec = pl.BlockSpec((tm, TN), lambda i, j: (i, j))
    tab_spec = pl.BlockSpec((tm, HEAD_DIM), lambda i, j: (i % nt, 0))
    vec_spec = pl.BlockSpec((1, HEAD_DIM), lambda i, j: (0, 0))
    p = pl.pallas_call(
        _inproj_kernel,
        grid=(m // tm, NP_TILES),
        in_specs=[h_spec, pl.BlockSpec((d, TN), lambda i, j: (0, j)),
                  vec_spec, vec_spec, tab_spec, tab_spec, tab_spec, tab_spec],
        out_specs=out_spec,
        out_shape=jax.ShapeDtypeStruct((m, P_W), BF16),
        compiler_params=_params(2),
        name="inproj",
    )(h, w_p, qw.reshape(1, HEAD_DIM), kw.reshape(1, HEAD_DIM), *tabs)
    g = pl.pallas_call(
        _gates_kernel,
        grid=(m // tm, gate_w // TN_GATES),
        in_specs=[h_spec, pl.BlockSpec((d, TN_GATES), lambda i, j: (0, j)),
                  pl.BlockSpec((1, TN_GATES), lambda i, j: (0, j))],
        out_specs=pl.BlockSpec((tm, TN_GATES), lambda i, j: (i, j)),
        out_shape=jax.ShapeDtypeStruct((m, gate_w), BF16),
        compiler_params=_params(2),
        name="gates",
    )(h, w_g, b_gate.reshape(1, gate_w))
    return p, g


def _softmax2(s_a, s_b):
    m = jnp.maximum(jnp.max(s_a, axis=-1, keepdims=True), jnp.max(s_b, axis=-1, keepdims=True))
    e_a = jnp.exp(s_a - m)
    e_b = jnp.exp(s_b - m)
    return e_a, e_b, jnp.sum(e_a, axis=-1, keepdims=True) + jnp.sum(e_b, axis=-1, keepdims=True)


def _softmax1(s, exp=jnp.exp):
    m = jnp.max(s, axis=-1, keepdims=True)
    e = exp(s - m)
    return e, jnp.sum(e, axis=-1, keepdims=True)


def _silu_gate(o, z):
    z = z.astype(F32)
    return o * (z * _sigmoid(z))


def _split_maps(q):
    lane = lax.broadcasted_iota(jnp.int32, q.shape, 1)
    zero = jnp.zeros_like(q)
    return jnp.where(lane < DC, q, zero), jnp.where(lane >= DC, q, zero)


def _lam(lq1_ref, lk1_ref, lq2_ref, lk2_ref, li_ref):
    return (jnp.exp(jnp.sum(lq1_ref[...] * lk1_ref[...], axis=-1, keepdims=True))
            - jnp.exp(jnp.sum(lq2_ref[...] * lk2_ref[...], axis=-1, keepdims=True)) + li_ref[:, 0:1])


UNITS = 4
ATTN_TQ = 512


def _scores_stage(q, kl, kc, s_ref, m_ref):
    length = kl.shape[0]
    s_l = _dot_nt(q, kl)
    s_c = _dot_nt(q, kc)
    s_ref[:, :length] = s_l
    s_ref[:, length:] = s_c
    m = jnp.maximum(jnp.max(s_l, axis=-1, keepdims=True), jnp.max(s_c, axis=-1, keepdims=True))
    m_ref[...] = jnp.broadcast_to(m, m_ref.shape)


def _softmax_pv_stage(s_ref, m_ref, p_ref, v_ext):
    n_blk = s_ref.shape[1] // HEAD_DIM
    m = m_ref[...]
    for k in range(n_blk):
        blk = slice(k * HEAD_DIM, (k + 1) * HEAD_DIM)
        p_ref[:, blk] = jnp.exp2(s_ref[:, blk] - m).astype(p_ref.dtype)
    o_ext = _dot(p_ref[...], v_ext)
    return o_ext[:, :HEAD_DIM], o_ext[:, HEAD_DIM:HEAD_DIM + 1]


def _fill_v_ext(v_ext_ref, vl, vc):
    length = vl.shape[0]
    v_ext_ref[:length, :HEAD_DIM] = vl
    v_ext_ref[length:, :HEAD_DIM] = vc
    v_ext_ref[:, HEAD_DIM:] = jnp.ones((v_ext_ref.shape[0], HEAD_DIM), v_ext_ref.dtype)


def _pipelined_units(first_step, prepare, unit_q, next_tile_q, unit_k, unit_v, finish, scratch):
    s_bufs, m_bufs, p_bufs = scratch[0:2], scratch[2:4], scratch[4:6]

    @pl.when(first_step)
    def _():
        prepare()
        _scores_stage(unit_q(0), *unit_k(0), s_bufs[0], m_bufs[0])

    for r in range(UNITS):
        cur, nxt = r % 2, (r + 1) % 2
        if r + 1 < UNITS:
            _scores_stage(unit_q(r + 1), *unit_k(r + 1), s_bufs[nxt], m_bufs[nxt])
        else:
            _scores_stage(next_tile_q(), *unit_k(0), s_bufs[nxt], m_bufs[nxt])
        o, l = _softmax_pv_stage(s_bufs[cur], m_bufs[cur], p_bufs[cur], unit_v(r))
        finish(r, o, l)


def _pipeline_scratch(tq, lk, n_v):
    return ([pltpu.VMEM((tq, lk), F32)] * 2 + [pltpu.VMEM((tq, HEAD_DIM), F32)] * 2 + [pltpu.VMEM((tq, lk), BF16)] * 2
            + [pltpu.VMEM((n_v, lk, 2 * HEAD_DIM), BF16)])


def _gqa_kernel(q_ref, qn_ref, kl_ref, vl_ref, kc_ref, vc_ref, z_ref, o_ref, *scratch):
    def cols(r):
        return slice(r * HEAD_DIM, (r + 1) * HEAD_DIM)

    def finish(r, o, l):
        o_ref[0, :, cols(r)] = _silu_gate(o / l, z_ref[0, :, cols(r)]).astype(o_ref.dtype)

    v_ext_ref = scratch[-1]
    _pipelined_units(
        pl.program_id(2) == 0,
        lambda: _fill_v_ext(v_ext_ref.at[0], vl_ref[0], vc_ref[0]),
        lambda r: q_ref[0, :, cols(r)],
        lambda: qn_ref[0],
        lambda r: (kl_ref[0], kc_ref[0]),
        lambda r: v_ext_ref[0],
        finish, scratch)


def _gqa_call(p, pc):
    b, length, _ = p.shape
    lc = pc.shape[1]
    tq = _row_tile(length, ATTN_TQ)
    n_i = length // tq
    assert H_B // KV_B == UNITS
    gw = UNITS * HEAD_DIM
    kv = lambda n, off: pl.BlockSpec((1, n, HEAD_DIM), lambda bi, g, i: (bi, 0, off // HEAD_DIM + g))
    return pl.pallas_call(
        _gqa_kernel,
        grid=(b, KV_B, n_i),
        in_specs=[
            pl.BlockSpec((1, tq, gw), lambda bi, g, i: (bi, i, OFF_BQ // gw + g)),
            pl.BlockSpec((1, tq, HEAD_DIM),
                         lambda bi, g, i: (bi, jnp.minimum(i + 1, n_i - 1), OFF_BQ // HEAD_DIM + UNITS * g)),
            kv(length, OFF_BK), kv(length, OFF_BV), kv(lc, OFF_BK), kv(lc, OFF_BV),
            pl.BlockSpec((1, tq, gw), lambda bi, g, i: (bi, i, OFF_BZ // gw + g)),
        ],
        out_specs=pl.BlockSpec((1, tq, gw), lambda bi, g, i: (bi, i, g)),
        out_shape=jax.ShapeDtypeStruct((b, length, B_QW), BF16),
        scratch_shapes=_pipeline_scratch(tq, length + lc, 1),
        compiler_params=_params(3),
        name="gqa",
    )(p, p, p, p, pc, pc, p)


def _diff_core(q, k, v, lam, sw, li):
    q1, q2 = _split_maps(q)
    e1, l1 = _softmax1(_dot_nt(q1, k), jnp.exp2)
    e2, l2 = _softmax1(_dot_nt(q2, k), jnp.exp2)
    a = (e1 * (1.0 / l1) - e2 * (lam / l2)).astype(BF16)
    o = _head_rmsnorm(_dot(a, v), sw)
    return o * (1.0 - li)


def _diff_kernel(q_ref, qn_ref, kl_ref, vl_ref, kc_ref, vc_ref, z_ref, lq1_ref, lk1_ref, lq2_ref, lk2_ref, li_ref,
                 sw_ref, o_ref, *scratch):
    lam = _lam(lq1_ref, lk1_ref, lq2_ref, lk2_ref, li_ref)
    pending = {}

    def cols(hh):
        return slice(hh * HEAD_DIM, (hh + 1) * HEAD_DIM)

    def finish(r, o, l):
        hh = r // 2
        if r % 2 == 0:
            pending[hh] = o / l
            return
        o = pending.pop(hh) - o * (lam / l)
        o = _head_rmsnorm(o, sw_ref[...]) * (1.0 - li_ref[...])
        o_ref[0, :, cols(hh)] = _silu_gate(o, z_ref[0, :, cols(hh)]).astype(o_ref.dtype)

    v_ext_ref = scratch[-1]

    def prepare():
        for hh in range(UNITS // 2):
            _fill_v_ext(v_ext_ref.at[hh], vl_ref[0, :, cols(hh)], vc_ref[0, :, cols(hh)])

    _pipelined_units(
        pl.program_id(2) == 0,
        prepare,
        lambda r: _split_maps(q_ref[0, :, cols(r // 2)])[r % 2],
        lambda: _split_maps(qn_ref[0])[0],
        lambda r: (kl_ref[0, :, cols(r // 2)], kc_ref[0, :, cols(r // 2)]),
        lambda r: v_ext_ref[r // 2],
        finish, scratch)


def _diff_call(p, pc, lam_args, li, sw):
    b, length, _ = p.shape
    lc = pc.shape[1]
    tq = _row_tile(length, ATTN_TQ)
    n_i = length // tq
    heads = UNITS // 2
    assert H_C % heads == 0
    pw = heads * HEAD_DIM
    small = lambda w: pl.BlockSpec((1, w), lambda bi, g, i: (0, 0))
    kv = lambda n, off: pl.BlockSpec((1, n, pw), lambda bi, g, i: (bi, 0, off // pw + g))
    return pl.pallas_call(
        _diff_kernel,
        grid=(b, H_C // heads, n_i),
        in_specs=[
            pl.BlockSpec((1, tq, pw), lambda bi, g, i: (bi, i, OFF_CQ // pw + g)),
            pl.BlockSpec((1, tq, HEAD_DIM),
                         lambda bi, g, i: (bi, jnp.minimum(i + 1, n_i - 1), OFF_CQ // HEAD_DIM + heads * g)),
            kv(length, OFF_CK), kv(length, OFF_CV), kv(lc, OFF_CK), kv(lc, OFF_CV),
            pl.BlockSpec((1, tq, pw), lambda bi, g, i: (bi, i, OFF_CZ // pw + g)),
            small(DC), small(DC), small(DC), small(DC), small(HEAD_DIM), small(HEAD_DIM),
        ],
        out_specs=pl.BlockSpec((1, tq, pw), lambda bi, g, i: (bi, i, g)),
        out_shape=jax.ShapeDtypeStruct((b, length, C_W), BF16),
        scratch_shapes=_pipeline_scratch(tq, length + lc, heads),
        compiler_params=_params(3),
        name="diff",
    )(p, p, p, p, pc, pc, p, *lam_args, li, sw)


def _nbr_key_row0(t, rows):
    return jnp.clip(t * NBR_R - WIN_H // 2, 0, rows - NBR_KR)


def _nbr_kernel(q_ref, kl_ref, vl_ref, kc_ref, vc_ref, z_ref, bias_ref, o_ref, *, rows):
    t = pl.program_id(1)
    span = NBR_KR * GRID_W
    k_off = pl.multiple_of(_nbr_key_row0(t, rows) * GRID_W, NBR_R * GRID_W)
    for hh in range(H_A):
        cols = slice(hh * HEAD_DIM, (hh + 1) * HEAD_DIM)
        q = q_ref[0, :, cols]
        s_win = _dot_nt(q, kl_ref[0, pl.ds(k_off, span), cols]) + bias_ref[hh, 0]
        s_ctx = _dot_nt(q, kc_ref[0, :, cols])
        e_w, e_c, l = _softmax2(s_win, s_ctx)
        o = _dot(e_w.astype(BF16), vl_ref[0, pl.ds(k_off, span), cols]) + _dot(e_c.astype(BF16), vc_ref[0, :, cols])
        o_ref[0, :, cols] = _silu_gate(o / l, z_ref[0, :, cols]).astype(o_ref.dtype)


def _nbr_bias_selectors(rows):
    nt = rows // NBR_R
    kh = min(WIN_H, rows)
    n_dr, n_dc = 2 * WIN_H - 1, 2 * WIN_W - 1
    qc = np.arange(GRID_W)[:, None]
    kc = np.arange(GRID_W)[None, :]
    dc = np.clip(kc - qc, -(WIN_W - 1), WIN_W - 1) + (WIN_W - 1)
    cs = np.clip(qc - WIN_W // 2, 0, GRID_W - WIN_W)
    in_win = (kc >= cs) & (kc < cs + WIN_W)
    sel_dc = (dc[None] == np.arange(n_dc)[:, None, None]).astype(np.float32)
    i = np.arange(NBR_R)[:, None]
    jr = np.arange(NBR_KR)[None, :]
    sel_dr, in_band = [], []
    for t in (0, 1, nt - 1):
        r0 = NBR_R * t
        k0 = int(np.clip(r0 - WIN_H // 2, 0, rows - NBR_KR))
        rq = r0 + i
        rk = k0 + jr
        rs = np.clip(rq - kh // 2, 0, rows - kh)
        band = (rk >= rs) & (rk < rs + kh)
        dr = np.clip(rk - rq + (WIN_H - 1), 0, n_dr - 1)
        sel_dr.append((dr[..., None] == np.arange(n_dr)) & band[..., None])
        in_band.append(band)
    sel_dr = np.stack(sel_dr).astype(np.float32)
    valid = np.stack(in_band)[:, :, None, :, None] & in_win[None, None, :, None, :]
    return sel_dc, sel_dr, valid


def _nbr_bias_tables(rpb, rows):
    sel_dc, sel_dr, valid = _nbr_bias_selectors(rows)
    by_col = jnp.einsum("lhrd,dqk->lhrqk", rpb, sel_dc, precision=lax.Precision.HIGHEST)
    bias = jnp.einsum("tijr,lhrqk->lhtiqjk", sel_dr, by_col, precision=lax.Precision.HIGHEST)
    bias = jnp.where(valid[None, None], bias, NEG)
    return bias.reshape(rpb.shape[0], H_A, 3, NBR_R * GRID_W, NBR_KR * GRID_W)


def _nbr_call(p, pc, bias):
    b, length, _ = p.shape
    lc = pc.shape[1]
    rows = length // GRID_W
    assert rows % NBR_R == 0 and rows >= NBR_KR
    tq = NBR_R * GRID_W
    nt = rows // NBR_R
    blk = lambda n, col: pl.BlockSpec((1, n, A_W), lambda bi, t: (bi, 0, col))
    tile = lambda col: pl.BlockSpec((1, tq, A_W), lambda bi, t: (bi, t, col))
    bias_type = lambda bi, t: (0, jnp.where(t == 0, 0, jnp.where(t == nt - 1, 2, 1)), 0, 0)
    return pl.pallas_call(
        functools.partial(_nbr_kernel, rows=rows),
        grid=(b, nt),
        in_specs=[
            tile(OFF_AQ // A_W), blk(length, OFF_AK // A_W), blk(length, OFF_AV // A_W),
            blk(lc, OFF_AK // A_W), blk(lc, OFF_AV // A_W), tile(OFF_AZ // A_W),
            pl.BlockSpec((H_A, 1, tq, NBR_KR * GRID_W), bias_type),
        ],
        out_specs=pl.BlockSpec((1, tq, A_W), lambda bi, t: (bi, t, 0)),
        out_shape=jax.ShapeDtypeStruct((b, length, A_W), BF16),
        compiler_params=_params(2),
        name="nbr",
    )(p, p, p, pc, pc, p, bias)


def _ctx_attn_kernel(p_ref, lq1_ref, lk1_ref, lq2_ref, lk2_ref, li_ref, sw_ref, oa_ref, ob_ref, oc_ref):
    def col(off, hh):
        return p_ref[0, :, off + hh * HEAD_DIM: off + (hh + 1) * HEAD_DIM]

    def plain(q, k, v, exp):
        e, l = _softmax1(_dot_nt(q, k), exp)
        return _dot(e.astype(BF16), v) / l

    for hh in range(H_A):
        o = plain(col(OFF_AQ, hh), col(OFF_AK, hh), col(OFF_AV, hh), jnp.exp)
        oa_ref[0, :, hh * HEAD_DIM:(hh + 1) * HEAD_DIM] = _silu_gate(o, col(OFF_AZ, hh)).astype(oa_ref.dtype)
    rep = H_B // KV_B
    for hh in range(H_B):
        o = plain(col(OFF_BQ, hh), col(OFF_BK, hh // rep), col(OFF_BV, hh // rep), jnp.exp2)
        ob_ref[0, :, hh * HEAD_DIM:(hh + 1) * HEAD_DIM] = _silu_gate(o, col(OFF_BZ, hh)).astype(ob_ref.dtype)
    lam = _lam(lq1_ref, lk1_ref, lq2_ref, lk2_ref, li_ref)
    for hh in range(H_C):
        o = _diff_core(col(OFF_CQ, hh), col(OFF_CK, hh), col(OFF_CV, hh), lam, sw_ref[...], li_ref[...])
        oc_ref[0, :, hh * HEAD_DIM:(hh + 1) * HEAD_DIM] = _silu_gate(o, col(OFF_CZ, hh)).astype(oc_ref.dtype)


def _ctx_attn_call(pc, lam_args, li, sw):
    b, lc, _ = pc.shape
    small = lambda w: pl.BlockSpec((1, w), lambda bi: (0, 0))
    out = lambda w: pl.BlockSpec((1, lc, w), lambda bi: (bi, 0, 0))
    return pl.pallas_call(
        _ctx_attn_kernel,
        grid=(b,),
        in_specs=[pl.BlockSpec((1, lc, P_W), lambda bi: (bi, 0, 0)),
                  small(DC), small(DC), small(DC), small(DC), small(HEAD_DIM), small(HEAD_DIM)],
        out_specs=[out(A_W), out(B_QW), out(C_W)],
        out_shape=[jax.ShapeDtypeStruct((b, lc, A_W), BF16), jax.ShapeDtypeStruct((b, lc, B_QW), BF16),
                   jax.ShapeDtypeStruct((b, lc, C_W), BF16)],
        compiler_params=_params(1),
        name="ctx_attn",
    )(pc, *lam_args, li, sw)


def _merge_kernel(aa_ref, ab_ref, ac_ref, wa_ref, wb_ref, wc_ref, ga_ref, gb_ref, gc_ref, y_ref):
    y = (ga_ref[...].astype(F32) * _dot(aa_ref[...], wa_ref[...])
         + gb_ref[...].astype(F32) * _dot(ab_ref[...], wb_ref[...])
         + gc_ref[...].astype(F32) * _dot(ac_ref[...], wc_ref[...]))
    y_ref[...] = y.astype(y_ref.dtype)


def _merge_call(aa, ab, ac, wa, wb, wc, g):
    m = aa.shape[0]
    d = wa.shape[1]
    tm = _row_tile(m, 1024)
    nd = d // TN
    a_spec = lambda w: pl.BlockSpec((tm, w), lambda i, n: (i, 0))
    w_spec = lambda w: pl.BlockSpec((w, TN), lambda i, n: (0, n))
    g_spec = lambda k: pl.BlockSpec((tm, TN), lambda i, n: (i, k * nd + n))
    return pl.pallas_call(
        _merge_kernel,
        grid=(m // tm, nd),
        in_specs=[a_spec(A_W), a_spec(B_QW), a_spec(C_W), w_spec(A_W), w_spec(B_QW), w_spec(C_W),
                  g_spec(0), g_spec(1), g_spec(2)],
        out_specs=pl.BlockSpec((tm, TN), lambda i, n: (i, n)),
        out_shape=jax.ShapeDtypeStruct((m, d), BF16),
        compiler_params=_params(2),
        name="merge",
    )(aa, ab, ac, wa, wb, wc, g, g, g)


def _residual(y_ref, w_ref, x_ref, mod_ref):
    gate = mod_ref[0, 2:3, :]
    return x_ref[0] + gate * _dot(y_ref[0], w_ref[...])


def _outproj_next_kernel(y_ref, w_ref, x_ref, mod_ref, nw_ref, modn_ref, xo_ref, ho_ref):
    x = _residual(y_ref, w_ref, x_ref, mod_ref)
    xo_ref[0] = x
    ho_ref[0] = _modulated_norm(x, nw_ref[...], modn_ref[0]).astype(ho_ref.dtype)


def _outproj_final_kernel(y_ref, w_ref, x_ref, mod_ref, nw_ref, o_ref):
    o_ref[0] = _rms(_residual(y_ref, w_ref, x_ref, mod_ref)) * nw_ref[...]


def _outproj_call(y, w_out, x, mod, nw, mod_next):
    b, length, d = x.shape
    tm = _row_tile(length, 256)
    row_spec = pl.BlockSpec((1, tm, d), lambda bi, i: (bi, i, 0))
    in_specs = [row_spec, pl.BlockSpec((d, d), lambda bi, i: (0, 0)), row_spec, _mod_spec(mod),
                pl.BlockSpec((1, d), lambda bi, i: (0, 0))]
    args = [y, w_out, x, mod, nw.reshape(1, d)]
    if mod_next is None:
        return pl.pallas_call(
            _outproj_final_kernel, grid=(b, length // tm), in_specs=in_specs, out_specs=row_spec,
            out_shape=jax.ShapeDtypeStruct((b, length, d), F32), compiler_params=_params(2), name="outproj_final",
        )(*args)
    return pl.pallas_call(
        _outproj_next_kernel, grid=(b, length // tm), in_specs=in_specs + [_mod_spec(mod_next)],
        out_specs=[row_spec, row_spec],
        out_shape=[jax.ShapeDtypeStruct((b, length, d), F32), jax.ShapeDtypeStruct((b, length, d), BF16)],
        compiler_params=_params(2), name="outproj",
    )(*args, mod_next)


def _rope_tables(length):
    pos = jnp.arange(length)
    row = (pos // GRID_W).astype(F32)[:, None]
    col = (pos % GRID_W).astype(F32)[:, None]

    def cos_sin(rot_dim):
        n = rot_dim // 4
        inv_freq = ROPE_THETA ** (-jnp.arange(n, dtype=F32) / n)
        ang = jnp.concatenate([row * inv_freq, col * inv_freq], axis=-1)
        return jnp.cos(ang), jnp.sin(ang)

    cb, sb = cos_sin(HEAD_DIM)
    cc, sc = cos_sin(DC)
    return (jnp.concatenate([cb, cb], axis=-1), jnp.concatenate([-sb, sb], axis=-1),
            jnp.concatenate([cc, cc, cc, cc], axis=-1), jnp.concatenate([-sc, sc, -sc, sc], axis=-1))


def _identity_tables(length):
    one = jnp.ones((length, HEAD_DIM), F32)
    zero = jnp.zeros((length, HEAD_DIM), F32)
    return one, zero, one, zero


def kernel(x, c, ctx, c_ctx, norm_w, w_ada, b_ada, w_in, b_gate, rpb, q_norm_w, k_norm_w, lam_q1, lam_k1, lam_q2,
           lam_k2, subln_w, w_bo_a, w_bo_b, w_bo_c, w_out, final_norm_w):
    b, length, d = x.shape
    lc = ctx.shape[1]
    depth = w_in.shape[0]
    rows = length // GRID_W

    n_cond = b + 1
    pad = (-n_cond) % 8
    cc = jnp.concatenate([c, c_ctx[None], jnp.zeros((pad, d), F32)], axis=0)
    mod_all = _ada_call(cc, w_ada, b_ada).reshape(depth, n_cond + pad, 3, d)

    tabs_lat = _rope_tables(length)
    tabs_ctx = _identity_tables(_row_tile(b * lc, INPROJ_TM))
    nbr_bias = _nbr_bias_tables(rpb, rows)

    w_p_bf = w_in[:, :, :P_W].astype(BF16)
    w_g_bf = w_in[:, :, P_W:].astype(BF16)
    wa_bf, wb_bf, wc_bf, wo_bf = (w.astype(BF16) for w in (w_bo_a, w_bo_b, w_bo_c, w_out))

    xl, xc = x, ctx
    h = _norm_call(xl, norm_w[0], mod_all[0, :b])
    hc = _norm_call(xc, norm_w[0], mod_all[0, b:b + 1])
    out = None
    for l in range(depth):
        last = l == depth - 1
        mod_l = mod_all[l, :b]
        mod_c = mod_all[l, b:b + 1]
        p, g = _inproj_call(h.reshape(b * length, d), w_p_bf[l], w_g_bf[l], b_gate[l], q_norm_w[l], k_norm_w[l],
                            tabs_lat)
        pc, gc = _inproj_call(hc.reshape(b * lc, d), w_p_bf[l], w_g_bf[l], b_gate[l], q_norm_w[l], k_norm_w[l],
                              tabs_ctx)
        p = p.reshape(b, length, P_W)
        pc = pc.reshape(b, lc, P_W)

        lam_init = 0.8 - 0.6 * float(np.exp(-0.3 * l))
        li = jnp.full((1, HEAD_DIM), lam_init, F32)
        lam_args = [v[l].reshape(1, DC) for v in (lam_q1, lam_k1, lam_q2, lam_k2)]
        sw = subln_w[l].reshape(1, HEAD_DIM)

        o_a = _nbr_call(p, pc, nbr_bias[l])
        o_b = _gqa_call(p, pc)
        o_c = _diff_call(p, pc, lam_args, li, sw)
        y = _merge_call(o_a.reshape(b * length, A_W), o_b.reshape(b * length, B_QW), o_c.reshape(b * length, C_W),
                        wa_bf[l], wb_bf[l], wc_bf[l], g)
        if last:
            out = _outproj_call(y.reshape(b, length, d), wo_bf[l], xl, mod_l, final_norm_w, None)
        else:
            xl, h = _outproj_call(y.reshape(b, length, d), wo_bf[l], xl, mod_l, norm_w[l + 1], mod_all[l + 1, :b])
            oc_a, oc_b, oc_c = _ctx_attn_call(pc, lam_args, li, sw)
            yc = _merge_call(oc_a.reshape(b * lc, A_W), oc_b.reshape(b * lc, B_QW), oc_c.reshape(b * lc, C_W),
                             wa_bf[l], wb_bf[l], wc_bf[l], gc)
            xc, hc = _outproj_call(yc.reshape(b, lc, d), wo_bf[l], xc, mod_c, norm_w[l + 1],
                                   mod_all[l + 1, b:b + 1])
    return out
```

```python
import functools

import numpy as np
import jax
import jax.numpy as jnp
from jax import lax
from jax.experimental import pallas as pl
from jax.experimental.pallas import tpu as pltpu

F32 = jnp.float32
BF16 = jnp.bfloat16

GRID_W = 64
HEAD_DIM = 128
H_A = 4
WIN_H = 8
WIN_W = 16
H_B = 8
KV_B = 2
H_C = 4
DC = HEAD_DIM // 2
ROPE_THETA = 10000.0
EPS = 1e-6
NEG = -1e30
LOG2E = 1.4426950408889634

A_W = H_A * HEAD_DIM
B_QW = H_B * HEAD_DIM
B_KW = KV_B * HEAD_DIM
C_W = H_C * HEAD_DIM

OFF_AQ, OFF_AK, OFF_AV, OFF_AZ = 0, A_W, 2 * A_W, 3 * A_W
OFF_BQ = 4 * A_W
OFF_BK = OFF_BQ + B_QW
OFF_BV = OFF_BK + B_KW
OFF_BZ = OFF_BV + B_KW
OFF_CQ = OFF_BZ + B_QW
OFF_CK = OFF_CQ + C_W
OFF_CV = OFF_CK + C_W
OFF_CZ = OFF_CV + C_W
P_W = OFF_CZ + C_W

TN = 512
TN_GATES = 1024
INPROJ_TM = 2048
NP_TILES = P_W // TN

NBR_R = 4
NBR_KR = 12

VMEM_LIMIT = 56 * 1024 * 1024


def _params(n_axes):
    return pltpu.CompilerParams(dimension_semantics=("arbitrary",) * n_axes, vmem_limit_bytes=VMEM_LIMIT)


def _sigmoid(x):
    return 1.0 / (1.0 + jnp.exp(-x))


def _dot(a, b):
    return jnp.dot(a, b, preferred_element_type=F32)


def _dot_nt(a, b):
    return lax.dot_general(a, b, (((1,), (1,)), ((), ())), preferred_element_type=F32)


def _head_rmsnorm(y, w):
    return y * lax.rsqrt(jnp.mean(y * y, axis=-1, keepdims=True) + EPS) * w


def _ada_kernel(c_ref, w_ref, b_ref, o_ref):
    a = c_ref[...]
    a = a * _sigmoid(a)
    o_ref[0] = _dot(a.astype(BF16), w_ref[0].astype(BF16)) + b_ref[0]


def _ada_call(cc, w_ada, b_ada):
    depth, d, n3 = w_ada.shape
    rows = cc.shape[0]
    tn = 1536
    return pl.pallas_call(
        _ada_kernel,
        grid=(depth, n3 // tn),
        in_specs=[
            pl.BlockSpec((rows, d), lambda l, j: (0, 0)),
            pl.BlockSpec((1, d, tn), lambda l, j: (l, 0, j)),
            pl.BlockSpec((1, 1, tn), lambda l, j: (l, 0, j)),
        ],
        out_specs=pl.BlockSpec((1, rows, tn), lambda l, j: (l, 0, j)),
        out_shape=jax.ShapeDtypeStruct((depth, rows, n3), F32),
        compiler_params=_params(2),
        name="ada",
    )(cc, w_ada, b_ada.reshape(depth, 1, n3))


def _rms(x):
    return x * lax.rsqrt(jnp.mean(x * x, axis=-1, keepdims=True) + EPS)


def _modulated_norm(x, w, mod):
    return _rms(x) * w * (1.0 + mod[1:2, :]) + mod[0:1, :]


def _norm_kernel(x_ref, w_ref, mod_ref, o_ref):
    o_ref[0] = _modulated_norm(x_ref[0], w_ref[...], mod_ref[0]).astype(o_ref.dtype)


def _row_tile(n, cap):
    t = min(n, cap)
    assert n % t == 0
    return t


def _mod_spec(mod):
    d = mod.shape[-1]
    return pl.BlockSpec((1, 3, d), (lambda bi, i: (bi, 0, 0)) if mod.shape[0] != 1 else (lambda bi, i: (0, 0, 0)))


def _norm_call(x, w, mod):
    b, length, d = x.shape
    tm = _row_tile(length, 512)
    row_spec = pl.BlockSpec((1, tm, d), lambda bi, i: (bi, i, 0))
    return pl.pallas_call(
        _norm_kernel,
        grid=(b, length // tm),
        in_specs=[row_spec, pl.BlockSpec((1, d), lambda bi, i: (0, 0)), _mod_spec(mod)],
        out_specs=row_spec,
        out_shape=jax.ShapeDtypeStruct((b, length, d), BF16),
        compiler_params=_params(2),
        name="norm",
    )(x, w.reshape(1, d), mod)


def _inproj_kernel(h_ref, w_ref, qw_ref, kw_ref, cb_ref, sb_ref, cc_ref, sc_ref, p_ref):
    j = pl.program_id(1)
    acc = _dot(h_ref[...], w_ref[...])
    scale_a = HEAD_DIM ** -0.5
    scale_b = HEAD_DIM ** -0.5 * LOG2E
    scale_c = DC ** -0.5 * LOG2E
    heads = TN // HEAD_DIM

    def rope_b(y):
        return y * cb_ref[...] + pltpu.roll(y, HEAD_DIM // 2, 1) * sb_ref[...]

    def rope_c(y):
        lane = lax.broadcasted_iota(jnp.int32, y.shape, 1)
        first = (lane & (DC // 2)) == 0
        rot = jnp.where(first, pltpu.roll(y, HEAD_DIM - DC // 2, 1), pltpu.roll(y, DC // 2, 1))
        return y * cc_ref[...] + rot * sc_ref[...]

    def head(hh):
        return acc[:, hh * HEAD_DIM:(hh + 1) * HEAD_DIM]

    def put(hh, y):
        p_ref[:, hh * HEAD_DIM:(hh + 1) * HEAD_DIM] = y.astype(p_ref.dtype)

    t_aq = OFF_AQ // TN
    t_bq0, t_bq1 = OFF_BQ // TN, OFF_BK // TN
    t_bkv = OFF_BK // TN
    t_cq, t_ck = OFF_CQ // TN, OFF_CK // TN

    p_ref[...] = (acc * jnp.where(j == t_aq, scale_a, 1.0)).astype(p_ref.dtype)

    @pl.when((j >= t_bq0) & (j < t_bq1))
    def _():
        for hh in range(heads):
            put(hh, rope_b(_head_rmsnorm(head(hh), qw_ref[...])) * scale_b)

    @pl.when(j == t_bkv)
    def _():
        for hh in range(KV_B):
            put(hh, rope_b(_head_rmsnorm(head(hh), kw_ref[...])))
        for hh in range(KV_B, heads):
            put(hh, head(hh))

    @pl.when(j == t_cq)
    def _():
        for hh in range(heads):
            put(hh, rope_c(head(hh)) * scale_c)

    @pl.when(j == t_ck)
    def _():
        for hh in range(heads):
            put(hh, rope_c(head(hh)))


def _gates_kernel(h_ref, w_ref, bg_ref, g_ref):
    g_ref[...] = _sigmoid(_dot(h_ref[...], w_ref[...]) + bg_ref[...]).astype(g_ref.dtype)


def _inproj_call(h, w_p, w_g, b_gate, qw, kw, tabs):
    m, d = h.shape
    gate_w = w_g.shape[1]
    assert w_p.shape[1] == P_W and B_KW == TN // 2 and gate_w % TN_GATES == 0
    ltab = tabs[0].shape[0]
    tm = _row_tile(ltab, INPROJ_TM)
    assert m % tm == 0
    nt = ltab // tm
    h_spec = pl.BlockSpec((tm, d), lambda i, j: (i, 0))
    out_spec = pl.BlockSpec((tm, TN), lambda i, j: (i, j))
    tab_spec = pl.BlockSpec((tm, HEAD_DIM), lambda i, j: (i % nt, 0))
    vec_spec = pl.BlockSpec((1, HEAD_DIM), lambda i, j: (0, 0))
    p = pl.pallas_call(
        _inproj_kernel,
        grid=(m // tm, NP_TILES),
        in_specs=[h_spec, pl.BlockSpec((d, TN), lambda i, j: (0, j)),
                  vec_spec, vec_spec, tab_spec, tab_spec, tab_spec, tab_spec],
        out_specs=out_spec,
        out_shape=jax.ShapeDtypeStruct((m, P_W), BF16),
        compiler_params=_params(2),
        name="inproj",
    )(h, w_p, qw.reshape(1, HEAD_DIM), kw.reshape(1, HEAD_DIM), *tabs)
    g = pl.pallas_call(
        _gates_kernel,
        grid=(m // tm, gate_w // TN_GATES),
        in_specs=[h_spec, pl.BlockSpec((d, TN_GATES), lambda i, j: (0, j)),
                  pl.BlockSpec((1, TN_GATES), lambda i, j: (0, j))],
        out_specs=pl.BlockSpec((tm, TN_GATES), lambda i, j: (i, j)),
        out_shape=jax.ShapeDtypeStruct((m, gate_w), BF16),
        compiler_params=_params(2),
        name="gates",
    )(h, w_g, b_gate.reshape(1, gate_w))
    return p, g


def _softmax2(s_a, s_b):
    m = jnp.maximum(jnp.max(s_a, axis=-1, keepdims=True), jnp.max(s_b, axis=-1, keepdims=True))
    e_a = jnp.exp(s_a - m)
    e_b = jnp.exp(s_b - m)
    return e_a, e_b, jnp.sum(e_a, axis=-1, keepdims=True) + jnp.sum(e_b, axis=-1, keepdims=True)


def _softmax1(s, exp=jnp.exp):
    m = jnp.max(s, axis=-1, keepdims=True)
    e = exp(s - m)
    return e, jnp.sum(e, axis=-1, keepdims=True)


def _silu_gate(o, z):
    z = z.astype(F32)
    return o * (z * _sigmoid(z))


def _split_maps(q):
    lane = lax.broadcasted_iota(jnp.int32, q.shape, 1)
    zero = jnp.zeros_like(q)
    return jnp.where(lane < DC, q, zero), jnp.where(lane >= DC, q, zero)


def _lam(lq1_ref, lk1_ref, lq2_ref, lk2_ref, li_ref):
    return (jnp.exp(jnp.sum(lq1_ref[...] * lk1_ref[...], axis=-1, keepdims=True))
            - jnp.exp(jnp.sum(lq2_ref[...] * lk2_ref[...], axis=-1, keepdims=True)) + li_ref[:, 0:1])


UNITS = 4
ATTN_TQ = 512
SOFTMAX_ROWS = 128
MAX_CHAINS = 2


def _scores_stage(q, kl, kc, s_ref):
    length = kl.shape[0]
    s_ref[:, :length] = _dot_nt(q, kl)
    s_ref[:, length:] = _dot_nt(q, kc)


def _softmax_pv_stage(s_ref, p_ref, v_ext):
    tq, lk = s_ref.shape
    blocks = [slice(k * HEAD_DIM, (k + 1) * HEAD_DIM) for k in range(lk // HEAD_DIM)]
    for r0 in range(0, tq, SOFTMAX_ROWS):
        rows = slice(r0, r0 + SOFTMAX_ROWS)
        parts = [s_ref[rows, blk] for blk in blocks[:MAX_CHAINS]]
        for k, blk in enumerate(blocks[MAX_CHAINS:]):
            parts[k % MAX_CHAINS] = jnp.maximum(parts[k % MAX_CHAINS], s_ref[rows, blk])
        m = functools.reduce(jnp.maximum, parts)
        m = jnp.broadcast_to(jnp.max(m, axis=-1, keepdims=True), (SOFTMAX_ROWS, HEAD_DIM))
        for blk in blocks:
            p_ref[rows, blk] = jnp.exp2(s_ref[rows, blk] - m).astype(p_ref.dtype)
    o_ext = _dot(p_ref[...], v_ext)
    return o_ext[:, :HEAD_DIM], o_ext[:, HEAD_DIM:HEAD_DIM + 1]


def _fill_v_ext(v_ext_ref, vl, vc):
    length = vl.shape[0]
    v_ext_ref[:length, :HEAD_DIM] = vl
    v_ext_ref[length:, :HEAD_DIM] = vc
    v_ext_ref[:, HEAD_DIM:] = jnp.ones((v_ext_ref.shape[0], HEAD_DIM), v_ext_ref.dtype)


def _pipelined_units(first_step, prepare, unit_q, next_tile_q, unit_k, unit_v, finish, scratch):
    s_bufs, p_bufs = scratch[0:2], scratch[2:4]

    @pl.when(first_step)
    def _():
        prepare()
        _scores_stage(unit_q(0), *unit_k(0), s_bufs[0])

    for r in range(UNITS):
        cur, nxt = r % 2, (r + 1) % 2
        if r + 1 < UNITS:
            _scores_stage(unit_q(r + 1), *unit_k(r + 1), s_bufs[nxt])
        else:
            _scores_stage(next_tile_q(), *unit_k(0), s_bufs[nxt])
        o, l = _softmax_pv_stage(s_bufs[cur], p_bufs[cur], unit_v(r))
        finish(r, o, l)


def _pipeline_scratch(tq, lk, n_v):
    return ([pltpu.VMEM((tq, lk), F32)] * 2 + [pltpu.VMEM((tq, lk), BF16)] * 2
            + [pltpu.VMEM((n_v, lk, 2 * HEAD_DIM), BF16)])


def _gqa_kernel(q_ref, qn_ref, kl_ref, vl_ref, kc_ref, vc_ref, z_ref, o_ref, *scratch):
    def cols(r):
        return slice(r * HEAD_DIM, (r + 1) * HEAD_DIM)

    def finish(r, o, l):
        o_ref[0, :, cols(r)] = _silu_gate(o / l, z_ref[0, :, cols(r)]).astype(o_ref.dtype)

    v_ext_ref = scratch[-1]
    _pipelined_units(
        pl.program_id(2) == 0,
        lambda: _fill_v_ext(v_ext_ref.at[0], vl_ref[0], vc_ref[0]),
        lambda r: q_ref[0, :, cols(r)],
        lambda: qn_ref[0],
        lambda r: (kl_ref[0], kc_ref[0]),
        lambda r: v_ext_ref[0],
        finish, scratch)


def _gqa_call(p, pc):
    b, length, _ = p.shape
    lc = pc.shape[1]
    tq = _row_tile(length, ATTN_TQ)
    n_i = length // tq
    assert H_B // KV_B == UNITS
    gw = UNITS * HEAD_DIM
    kv = lambda n, off: pl.BlockSpec((1, n, HEAD_DIM), lambda bi, g, i: (bi, 0, off // HEAD_DIM + g))
    return pl.pallas_call(
        _gqa_kernel,
        grid=(b, KV_B, n_i),
        in_specs=[
            pl.BlockSpec((1, tq, gw), lambda bi, g, i: (bi, i, OFF_BQ // gw + g)),
            pl.BlockSpec((1, tq, HEAD_DIM),
                         lambda bi, g, i: (bi, jnp.minimum(i + 1, n_i - 1), OFF_BQ // HEAD_DIM + UNITS * g)),
            kv(length, OFF_BK), kv(length, OFF_BV), kv(lc, OFF_BK), kv(lc, OFF_BV),
            pl.BlockSpec((1, tq, gw), lambda bi, g, i: (bi, i, OFF_BZ // gw + g)),
        ],
        out_specs=pl.BlockSpec((1, tq, gw), lambda bi, g, i: (bi, i, g)),
        out_shape=jax.ShapeDtypeStruct((b, length, B_QW), BF16),
        scratch_shapes=_pipeline_scratch(tq, length + lc, 1),
        compiler_params=_params(3),
        name="gqa",
    )(p, p, p, p, pc, pc, p)


def _diff_core(q, k, v, lam, sw, li):
    q1, q2 = _split_maps(q)
    e1, l1 = _softmax1(_dot_nt(q1, k), jnp.exp2)
    e2, l2 = _softmax1(_dot_nt(q2, k), jnp.exp2)
    a = (e1 * (1.0 / l1) - e2 * (lam / l2)).astype(BF16)
    o = _head_rmsnorm(_dot(a, v), sw)
    return o * (1.0 - li)


def _diff_kernel(q_ref, qn_ref, kl_ref, vl_ref, kc_ref, vc_ref, z_ref, lq1_ref, lk1_ref, lq2_ref, lk2_ref, li_ref,
                 sw_ref, o_ref, *scratch):
    lam = _lam(lq1_ref, lk1_ref, lq2_ref, lk2_ref, li_ref)
    pending = {}

    def cols(hh):
        return slice(hh * HEAD_DIM, (hh + 1) * HEAD_DIM)

    def finish(r, o, l):
        hh = r // 2
        if r % 2 == 0:
            pending[hh] = o / l
            return
        o = pending.pop(hh) - o * (lam / l)
        o = _head_rmsnorm(o, sw_ref[...]) * (1.0 - li_ref[...])
        o_ref[0, :, cols(hh)] = _silu_gate(o, z_ref[0, :, cols(hh)]).astype(o_ref.dtype)

    v_ext_ref = scratch[-1]

    def prepare():
        for hh in range(UNITS // 2):
            _fill_v_ext(v_ext_ref.at[hh], vl_ref[0, :, cols(hh)], vc_ref[0, :, cols(hh)])

    _pipelined_units(
        pl.program_id(2) == 0,
        prepare,
        lambda r: _split_maps(q_ref[0, :, cols(r // 2)])[r % 2],
        lambda: _split_maps(qn_ref[0])[0],
        lambda r: (kl_ref[0, :, cols(r // 2)], kc_ref[0, :, cols(r // 2)]),
        lambda r: v_ext_ref[r // 2],
        finish, scratch)


def _diff_call(p, pc, lam_args, li, sw):
    b, length, _ = p.shape
    lc = pc.shape[1]
    tq = _row_tile(length, ATTN_TQ)
    n_i = length // tq
    heads = UNITS // 2
    assert H_C % heads == 0
    pw = heads * HEAD_DIM
    small = lambda w: pl.BlockSpec((1, w), lambda bi, g, i: (0, 0))
    kv = lambda n, off: pl.BlockSpec((1, n, pw), lambda bi, g, i: (bi, 0, off // pw + g))
    return pl.pallas_call(
        _diff_kernel,
        grid=(b, H_C // heads, n_i),
        in_specs=[
            pl.BlockSpec((1, tq, pw), lambda bi, g, i: (bi, i, OFF_CQ // pw + g)),
            pl.BlockSpec((1, tq, HEAD_DIM),
                         lambda bi, g, i: (bi, jnp.minimum(i + 1, n_i - 1), OFF_CQ // HEAD_DIM + heads * g)),
            kv(length, OFF_CK), kv(length, OFF_CV), kv(lc, OFF_CK), kv(lc, OFF_CV),
            pl.BlockSpec((1, tq, pw), lambda bi, g, i: (bi, i, OFF_CZ // pw + g)),
            small(DC), small(DC), small(DC), small(DC), small(HEAD_DIM), small(HEAD_DIM),
        ],
        out_specs=pl.BlockSpec((1, tq, pw), lambda bi, g, i: (bi, i, g)),
        out_shape=jax.ShapeDtypeStruct((b, length, C_W), BF16),
        scratch_shapes=_pipeline_scratch(tq, length + lc, heads),
        compiler_params=_params(3),
        name="diff",
    )(p, p, p, p, pc, pc, p, *lam_args, li, sw)


def _nbr_key_row0(t, rows):
    return jnp.clip(t * NBR_R - WIN_H // 2, 0, rows - NBR_KR)


def _nbr_kernel(q_ref, kl_ref, vl_ref, kc_ref, vc_ref, z_ref, bias_ref, o_ref, *, rows):
    t = pl.program_id(1)
    span = NBR_KR * GRID_W
    k_off = pl.multiple_of(_nbr_key_row0(t, rows) * GRID_W, NBR_R * GRID_W)
    for hh in range(H_A):
        cols = slice(hh * HEAD_DIM, (hh + 1) * HEAD_DIM)
        q = q_ref[0, :, cols]
        s_win = _dot_nt(q, kl_ref[0, pl.ds(k_off, span), cols]) + bias_ref[hh, 0]
        s_ctx = _dot_nt(q, kc_ref[0, :, cols])
        e_w, e_c, l = _softmax2(s_win, s_ctx)
        o = _dot(e_w.astype(BF16), vl_ref[0, pl.ds(k_off, span), cols]) + _dot(e_c.astype(BF16), vc_ref[0, :, cols])
        o_ref[0, :, cols] = _silu_gate(o / l, z_ref[0, :, cols]).astype(o_ref.dtype)


def _nbr_bias_selectors(rows):
    nt = rows // NBR_R
    kh = min(WIN_H, rows)
    n_dr, n_dc = 2 * WIN_H - 1, 2 * WIN_W - 1
    qc = np.arange(GRID_W)[:, None]
    kc = np.arange(GRID_W)[None, :]
    dc = np.clip(kc - qc, -(WIN_W - 1), WIN_W - 1) + (WIN_W - 1)
    cs = np.clip(qc - WIN_W // 2, 0, GRID_W - WIN_W)
    in_win = (kc >= cs) & (kc < cs + WIN_W)
    sel_dc = (dc[None] == np.arange(n_dc)[:, None, None]).astype(np.float32)
    i = np.arange(NBR_R)[:, None]
    jr = np.arange(NBR_KR)[None, :]
    sel_dr, in_band = [], []
    for t in (0, 1, nt - 1):
        r0 = NBR_R * t
        k0 = int(np.clip(r0 - WIN_H // 2, 0, rows - NBR_KR))
        rq = r0 + i
        rk = k0 + jr
        rs = np.clip(rq - kh // 2, 0, rows - kh)
        band = (rk >= rs) & (rk < rs + kh)
        dr = np.clip(rk - rq + (WIN_H - 1), 0, n_dr - 1)
        sel_dr.append((dr[..., None] == np.arange(n_dr)) & band[..., None])
        in_band.append(band)
    sel_dr = np.stack(sel_dr).astype(np.float32)
    valid = np.stack(in_band)[:, :, None, :, None] & in_win[None, None, :, None, :]
    return sel_dc, sel_dr, valid


def _nbr_bias_tables(rpb, rows):
    sel_dc, sel_dr, valid = _nbr_bias_selectors(rows)
    by_col = jnp.einsum("lhrd,dqk->lhrqk", rpb, sel_dc, precision=lax.Precision.HIGHEST)
    bias = jnp.einsum("tijr,lhrqk->lhtiqjk", sel_dr, by_col, precision=lax.Precision.HIGHEST)
    bias = jnp.where(valid[None, None], bias, NEG)
    return bias.reshape(rpb.shape[0], H_A, 3, NBR_R * GRID_W, NBR_KR * GRID_W)


def _nbr_call(p, pc, bias):
    b, length, _ = p.shape
    lc = pc.shape[1]
    rows = length // GRID_W
    assert rows % NBR_R == 0 and rows >= NBR_KR
    tq = NBR_R * GRID_W
    nt = rows // NBR_R
    blk = lambda n, col: pl.BlockSpec((1, n, A_W), lambda bi, t: (bi, 0, col))
    tile = lambda col: pl.BlockSpec((1, tq, A_W), lambda bi, t: (bi, t, col))
    bias_type = lambda bi, t: (0, jnp.where(t == 0, 0, jnp.where(t == nt - 1, 2, 1)), 0, 0)
    return pl.pallas_call(
        functools.partial(_nbr_kernel, rows=rows),
        grid=(b, nt),
        in_specs=[
            tile(OFF_AQ // A_W), blk(length, OFF_AK // A_W), blk(length, OFF_AV // A_W),
            blk(lc, OFF_AK // A_W), blk(lc, OFF_AV // A_W), tile(OFF_AZ // A_W),
            pl.BlockSpec((H_A, 1, tq, NBR_KR * GRID_W), bias_type),
        ],
        out_specs=pl.BlockSpec((1, tq, A_W), lambda bi, t: (bi, t, 0)),
        out_shape=jax.ShapeDtypeStruct((b, length, A_W), BF16),
        compiler_params=_params(2),
        name="nbr",
    )(p, p, p, pc, pc, p, bias)


def _ctx_attn_kernel(p_ref, lq1_ref, lk1_ref, lq2_ref, lk2_ref, li_ref, sw_ref, oa_ref, ob_ref, oc_ref):
    def col(off, hh):
        return p_ref[0, :, off + hh * HEAD_DIM: off + (hh + 1) * HEAD_DIM]

    def plain(q, k, v, exp):
        e, l = _softmax1(_dot_nt(q, k), exp)
        return _dot(e.astype(BF16), v) / l

    for hh in range(H_A):
        o = plain(col(OFF_AQ, hh), col(OFF_AK, hh), col(OFF_AV, hh), jnp.exp)
        oa_ref[0, :, hh * HEAD_DIM:(hh + 1) * HEAD_DIM] = _silu_gate(o, col(OFF_AZ, hh)).astype(oa_ref.dtype)
    rep = H_B // KV_B
    for hh in range(H_B):
        o = plain(col(OFF_BQ, hh), col(OFF_BK, hh // rep), col(OFF_BV, hh // rep), jnp.exp2)
        ob_ref[0, :, hh * HEAD_DIM:(hh + 1) * HEAD_DIM] = _silu_gate(o, col(OFF_BZ, hh)).astype(ob_ref.dtype)
    lam = _lam(lq1_ref, lk1_ref, lq2_ref, lk2_ref, li_ref)
    for hh in range(H_C):
        o = _diff_core(col(OFF_CQ, hh), col(OFF_CK, hh), col(OFF_CV, hh), lam, sw_ref[...], li_ref[...])
        oc_ref[0, :, hh * HEAD_DIM:(hh + 1) * HEAD_DIM] = _silu_gate(o, col(OFF_CZ, hh)).astype(oc_ref.dtype)


def _ctx_attn_call(pc, lam_args, li, sw):
    b, lc, _ = pc.shape
    small = lambda w: pl.BlockSpec((1, w), lambda bi: (0, 0))
    out = lambda w: pl.BlockSpec((1, lc, w), lambda bi: (bi, 0, 0))
    return pl.pallas_call(
        _ctx_attn_kernel,
        grid=(b,),
        in_specs=[pl.BlockSpec((1, lc, P_W), lambda bi: (bi, 0, 0)),
                  small(DC), small(DC), small(DC), small(DC), small(HEAD_DIM), small(HEAD_DIM)],
        out_specs=[out(A_W), out(B_QW), out(C_W)],
        out_shape=[jax.ShapeDtypeStruct((b, lc, A_W), BF16), jax.ShapeDtypeStruct((b, lc, B_QW), BF16),
                   jax.ShapeDtypeStruct((b, lc, C_W), BF16)],
        compiler_params=_params(1),
        name="ctx_attn",
    )(pc, *lam_args, li, sw)


def _merge_kernel(aa_ref, ab_ref, ac_ref, wa_ref, wb_ref, wc_ref, ga_ref, gb_ref, gc_ref, y_ref):
    y = (ga_ref[...].astype(F32) * _dot(aa_ref[...], wa_ref[...])
         + gb_ref[...].astype(F32) * _dot(ab_ref[...], wb_ref[...])
         + gc_ref[...].astype(F32) * _dot(ac_ref[...], wc_ref[...]))
    y_ref[...] = y.astype(y_ref.dtype)


def _merge_call(aa, ab, ac, wa, wb, wc, g):
    m = aa.shape[0]
    d = wa.shape[1]
    tm = _row_tile(m, 1024)
    nd = d // TN
    a_spec = lambda w: pl.BlockSpec((tm, w), lambda i, n: (i, 0))
    w_spec = lambda w: pl.BlockSpec((w, TN), lambda i, n: (0, n))
    g_spec = lambda k: pl.BlockSpec((tm, TN), lambda i, n: (i, k * nd + n))
    return pl.pallas_call(
        _merge_kernel,
        grid=(m // tm, nd),
        in_specs=[a_spec(A_W), a_spec(B_QW), a_spec(C_W), w_spec(A_W), w_spec(B_QW), w_spec(C_W),
                  g_spec(0), g_spec(1), g_spec(2)],
        out_specs=pl.BlockSpec((tm, TN), lambda i, n: (i, n)),
        out_shape=jax.ShapeDtypeStruct((m, d), BF16),
        compiler_params=_params(2),
        name="merge",
    )(aa, ab, ac, wa, wb, wc, g, g, g)


def _residual(y_ref, w_ref, x_ref, mod_ref):
    gate = mod_ref[0, 2:3, :]
    return x_ref[0] + gate * _dot(y_ref[0], w_ref[...])


def _outproj_next_kernel(y_ref, w_ref, x_ref, mod_ref, nw_ref, modn_ref, xo_ref, ho_ref):
    x = _residual(y_ref, w_ref, x_ref, mod_ref)
    xo_ref[0] = x
    ho_ref[0] = _modulated_norm(x, nw_ref[...], modn_ref[0]).astype(ho_ref.dtype)


def _outproj_final_kernel(y_ref, w_ref, x_ref, mod_ref, nw_ref, o_ref):
    o_ref[0] = _rms(_residual(y_ref, w_ref, x_ref, mod_ref)) * nw_ref[...]


def _outproj_call(y, w_out, x, mod, nw, mod_next):
    b, length, d = x.shape
    tm = _row_tile(length, 256)
    row_spec = pl.BlockSpec((1, tm, d), lambda bi, i: (bi, i, 0))
    in_specs = [row_spec, pl.BlockSpec((d, d), lambda bi, i: (0, 0)), row_spec, _mod_spec(mod),
                pl.BlockSpec((1, d), lambda bi, i: (0, 0))]
    args = [y, w_out, x, mod, nw.reshape(1, d)]
    if mod_next is None:
        return pl.pallas_call(
            _outproj_final_kernel, grid=(b, length // tm), in_specs=in_specs, out_specs=row_spec,
            out_shape=jax.ShapeDtypeStruct((b, length, d), F32), compiler_params=_params(2), name="outproj_final",
        )(*args)
    return pl.pallas_call(
        _outproj_next_kernel, grid=(b, length // tm), in_specs=in_specs + [_mod_spec(mod_next)],
        out_specs=[row_spec, row_spec],
        out_shape=[jax.ShapeDtypeStruct((b, length, d), F32), jax.ShapeDtypeStruct((b, length, d), BF16)],
        compiler_params=_params(2), name="outproj",
    )(*args, mod_next)


def _rope_tables(length):
    pos = jnp.arange(length)
    row = (pos // GRID_W).astype(F32)[:, None]
    col = (pos % GRID_W).astype(F32)[:, None]

    def cos_sin(rot_dim):
        n = rot_dim // 4
        inv_freq = ROPE_THETA ** (-jnp.arange(n, dtype=F32) / n)
        ang = jnp.concatenate([row * inv_freq, col * inv_freq], axis=-1)
        return jnp.cos(ang), jnp.sin(ang)

    cb, sb = cos_sin(HEAD_DIM)
    cc, sc = cos_sin(DC)
    return (jnp.concatenate([cb, cb], axis=-1), jnp.concatenate([-sb, sb], axis=-1),
            jnp.concatenate([cc, cc, cc, cc], axis=-1), jnp.concatenate([-sc, sc, -sc, sc], axis=-1))


def _identity_tables(length):
    one = jnp.ones((length, HEAD_DIM), F32)
    zero = jnp.zeros((length, HEAD_DIM), F32)
    return one, zero, one, zero


def kernel(x, c, ctx, c_ctx, norm_w, w_ada, b_ada, w_in, b_gate, rpb, q_norm_w, k_norm_w, lam_q1, lam_k1, lam_q2,
           lam_k2, subln_w, w_bo_a, w_bo_b, w_bo_c, w_out, final_norm_w):
    b, length, d = x.shape
    lc = ctx.shape[1]
    depth = w_in.shape[0]
    rows = length // GRID_W

    n_cond = b + 1
    pad = (-n_cond) % 8
    cc = jnp.concatenate([c, c_ctx[None], jnp.zeros((pad, d), F32)], axis=0)
    mod_all = _ada_call(cc, w_ada, b_ada).reshape(depth, n_cond + pad, 3, d)

    tabs_lat = _rope_tables(length)
    tabs_ctx = _identity_tables(_row_tile(b * lc, INPROJ_TM))
    nbr_bias = _nbr_bias_tables(rpb, rows)

    w_p_bf = w_in[:, :, :P_W].astype(BF16)
    w_g_bf = w_in[:, :, P_W:].astype(BF16)
    wa_bf, wb_bf, wc_bf, wo_bf = (w.astype(BF16) for w in (w_bo_a, w_bo_b, w_bo_c, w_out))

    xl, xc = x, ctx
    h = _norm_call(xl, norm_w[0], mod_all[0, :b])
    hc = _norm_call(xc, norm_w[0], mod_all[0, b:b + 1])
    out = None
    for l in range(depth):
        last = l == depth - 1
        mod_l = mod_all[l, :b]
        mod_c = mod_all[l, b:b + 1]
        p, g = _inproj_call(h.reshape(b * length, d), w_p_bf[l], w_g_bf[l], b_gate[l], q_norm_w[l], k_norm_w[l],
                            tabs_lat)
        pc, gc = _inproj_call(hc.reshape(b * lc, d), w_p_bf[l], w_g_bf[l], b_gate[l], q_norm_w[l], k_norm_w[l],
                              tabs_ctx)
        p = p.reshape(b, length, P_W)
        pc = pc.reshape(b, lc, P_W)

        lam_init = 0.8 - 0.6 * float(np.exp(-0.3 * l))
        li = jnp.full((1, HEAD_DIM), lam_init, F32)
        lam_args = [v[l].reshape(1, DC) for v in (lam_q1, lam_k1, lam_q2, lam_k2)]
        sw = subln_w[l].reshape(1, HEAD_DIM)

        o_a = _nbr_call(p, pc, nbr_bias[l])
        o_b = _gqa_call(p, pc)
        o_c = _diff_call(p, pc, lam_args, li, sw)
        y = _merge_call(o_a.reshape(b * length, A_W), o_b.reshape(b * length, B_QW), o_c.reshape(b * length, C_W),
                        wa_bf[l], wb_bf[l], wc_bf[l], g)
        if last:
            out = _outproj_call(y.reshape(b, length, d), wo_bf[l], xl, mod_l, final_norm_w, None)
        else:
            xl, h = _outproj_call(y.reshape(b, length, d), wo_bf[l], xl, mod_l, norm_w[l + 1], mod_all[l + 1, :b])
            oc_a, oc_b, oc_c = _ctx_attn_call(pc, lam_args, li, sw)
            yc = _merge_call(oc_a.reshape(b * lc, A_W), oc_b.reshape(b * lc, B_QW), oc_c.reshape(b * lc, C_W),
                             wa_bf[l], wb_bf[l], wc_bf[l], gc)
            xc, hc = _outproj_call(yc.reshape(b, lc, d), wo_bf[l], xc, mod_c, norm_w[l + 1],
                                   mod_all[l + 1, b:b + 1])
    return out
```

```python
import functools

import numpy as np
import jax
import jax.numpy as jnp
from jax import lax
from jax.experimental import pallas as pl
from jax.experimental.pallas import tpu as pltpu

F32 = jnp.float32
BF16 = jnp.bfloat16

GRID_W = 64
HEAD_DIM = 128
H_A = 4
WIN_H = 8
WIN_W = 16
H_B = 8
KV_B = 2
H_C = 4
DC = HEAD_DIM // 2
ROPE_THETA = 10000.0
EPS = 1e-6
NEG = -1e30
LOG2E = 1.4426950408889634

A_W = H_A * HEAD_DIM
B_QW = H_B * HEAD_DIM
B_KW = KV_B * HEAD_DIM
C_W = H_C * HEAD_DIM

OFF_AQ, OFF_AK, OFF_AV, OFF_AZ = 0, A_W, 2 * A_W, 3 * A_W
OFF_BQ = 4 * A_W
OFF_BK = OFF_BQ + B_QW
OFF_BV = OFF_BK + B_KW
OFF_BZ = OFF_BV + B_KW
OFF_CQ = OFF_BZ + B_QW
OFF_CK = OFF_CQ + C_W
OFF_CV = OFF_CK + C_W
OFF_CZ = OFF_CV + C_W
P_W = OFF_CZ + C_W

TN = 512
TN_GATES = 1024
GATES_TM = 512
INPROJ_TM = 256
NP_TILES = P_W // TN

NBR_R = 4
NBR_KR = 12

VMEM_LIMIT = 56 * 1024 * 1024


def _params(n_axes):
    return pltpu.CompilerParams(dimension_semantics=("arbitrary",) * n_axes, vmem_limit_bytes=VMEM_LIMIT)


def _sigmoid(x):
    return 1.0 / (1.0 + jnp.exp(-x))


def _dot(a, b):
    return jnp.dot(a, b, preferred_element_type=F32)


def _dot_nt(a, b):
    return lax.dot_general(a, b, (((1,), (1,)), ((), ())), preferred_element_type=F32)


def _head_rmsnorm(y, w):
    return y * lax.rsqrt(jnp.mean(y * y, axis=-1, keepdims=True) + EPS) * w


def _ada_kernel(c_ref, w_ref, b_ref, o_ref):
    a = c_ref[...]
    a = a * _sigmoid(a)
    o_ref[0] = _dot(a.astype(BF16), w_ref[0].astype(BF16)) + b_ref[0]


def _ada_call(cc, w_ada, b_ada):
    depth, d, n3 = w_ada.shape
    rows = cc.shape[0]
    tn = 1536
    return pl.pallas_call(
        _ada_kernel,
        grid=(depth, n3 // tn),
        in_specs=[
            pl.BlockSpec((rows, d), lambda l, j: (0, 0)),
            pl.BlockSpec((1, d, tn), lambda l, j: (l, 0, j)),
            pl.BlockSpec((1, 1, tn), lambda l, j: (l, 0, j)),
        ],
        out_specs=pl.BlockSpec((1, rows, tn), lambda l, j: (l, 0, j)),
        out_shape=jax.ShapeDtypeStruct((depth, rows, n3), F32),
        compiler_params=_params(2),
        name="ada",
    )(cc, w_ada, b_ada.reshape(depth, 1, n3))


def _rms(x):
    return x * lax.rsqrt(jnp.mean(x * x, axis=-1, keepdims=True) + EPS)


def _modulated_norm(x, w, mod):
    return _rms(x) * w * (1.0 + mod[1:2, :]) + mod[0:1, :]


def _norm_kernel(x_ref, w_ref, mod_ref, o_ref):
    o_ref[0] = _modulated_norm(x_ref[0], w_ref[...], mod_ref[0]).astype(o_ref.dtype)


def _row_tile(n, cap):
    t = min(n, cap)
    assert n % t == 0
    return t


def _mod_spec(mod):
    d = mod.shape[-1]
    return pl.BlockSpec((1, 3, d), (lambda bi, i: (bi, 0, 0)) if mod.shape[0] != 1 else (lambda bi, i: (0, 0, 0)))


def _norm_call(x, w, mod):
    b, length, d = x.shape
    tm = _row_tile(length, 512)
    row_spec = pl.BlockSpec((1, tm, d), lambda bi, i: (bi, i, 0))
    return pl.pallas_call(
        _norm_kernel,
        grid=(b, length // tm),
        in_specs=[row_spec, pl.BlockSpec((1, d), lambda bi, i: (0, 0)), _mod_spec(mod)],
        out_specs=row_spec,
        out_shape=jax.ShapeDtypeStruct((b, length, d), BF16),
        compiler_params=_params(2),
        name="norm",
    )(x, w.reshape(1, d), mod)


def _inproj_kernel(h_ref, w_ref, qw_ref, kw_ref, cb_ref, sb_ref, cc_ref, sc_ref, p_ref):
    h = h_ref[...]
    scale_a = HEAD_DIM ** -0.5
    scale_b = HEAD_DIM ** -0.5 * LOG2E
    scale_c = DC ** -0.5 * LOG2E
    heads = TN // HEAD_DIM

    def rope_b(y):
        return y * cb_ref[...] + pltpu.roll(y, HEAD_DIM // 2, 1) * sb_ref[...]

    def rope_c(y):
        lane = lax.broadcasted_iota(jnp.int32, y.shape, 1)
        first = (lane & (DC // 2)) == 0
        rot = jnp.where(first, pltpu.roll(y, HEAD_DIM - DC // 2, 1), pltpu.roll(y, DC // 2, 1))
        return y * cc_ref[...] + rot * sc_ref[...]

    def finish(col, y):
        if OFF_AQ <= col < OFF_AK:
            return y * scale_a
        if OFF_BQ <= col < OFF_BK:
            return rope_b(_head_rmsnorm(y, qw_ref[...])) * scale_b
        if OFF_BK <= col < OFF_BV:
            return rope_b(_head_rmsnorm(y, kw_ref[...]))
        if OFF_CQ <= col < OFF_CK:
            return rope_c(y) * scale_c
        if OFF_CK <= col < OFF_CV:
            return rope_c(y)
        return y

    for c0 in range(0, p_ref.shape[1], TN):
        acc = _dot(h, w_ref[:, c0:c0 + TN])
        for hh in range(heads):
            col = c0 + hh * HEAD_DIM
            y = finish(col, acc[:, hh * HEAD_DIM:(hh + 1) * HEAD_DIM])
            p_ref[:, col:col + HEAD_DIM] = y.astype(p_ref.dtype)


def _gates_kernel(h_ref, w_ref, bg_ref, g_ref):
    h = h_ref[...]
    for c0 in range(0, g_ref.shape[1], TN_GATES):
        cols = slice(c0, c0 + TN_GATES)
        g_ref[:, cols] = _sigmoid(_dot(h, w_ref[:, cols]) + bg_ref[:, cols]).astype(g_ref.dtype)


def _inproj_call(h, w_p, w_g, b_gate, qw, kw, tabs):
    m, d = h.shape
    gate_w = w_g.shape[1]
    assert w_p.shape[1] == P_W and P_W % TN == 0 and gate_w % TN_GATES == 0
    ltab = tabs[0].shape[0]
    tm = _row_tile(ltab, INPROJ_TM)
    assert m % tm == 0
    nt = ltab // tm
    tab_spec = pl.BlockSpec((tm, HEAD_DIM), lambda i: (i % nt, 0))
    vec_spec = pl.BlockSpec((1, HEAD_DIM), lambda i: (0, 0))
    p = pl.pallas_call(
        _inproj_kernel,
        grid=(m // tm,),
        in_specs=[pl.BlockSpec((tm, d), lambda i: (i, 0)),
                  pl.BlockSpec((d, P_W), lambda i: (0, 0), pipeline_mode=pl.Buffered(1)),
                  vec_spec, vec_spec, tab_spec, tab_spec, tab_spec, tab_spec],
        out_specs=pl.BlockSpec((tm, P_W), lambda i: (i, 0)),
        out_shape=jax.ShapeDtypeStruct((m, P_W), BF16),
        compiler_params=_params(1),
        name="inproj",
    )(h, w_p, qw.reshape(1, HEAD_DIM), kw.reshape(1, HEAD_DIM), *tabs)
    tg = _row_tile(m, GATES_TM)
    g = pl.pallas_call(
        _gates_kernel,
        grid=(m // tg,),
        in_specs=[pl.BlockSpec((tg, d), lambda i: (i, 0)),
                  pl.BlockSpec((d, gate_w), lambda i: (0, 0), pipeline_mode=pl.Buffered(1)),
                  pl.BlockSpec((1, gate_w), lambda i: (0, 0))],
        out_specs=pl.BlockSpec((tg, gate_w), lambda i: (i, 0)),
        out_shape=jax.ShapeDtypeStruct((m, gate_w), BF16),
        compiler_params=_params(1),
        name="gates",
    )(h, w_g, b_gate.reshape(1, gate_w))
    return p, g


def _softmax2(s_a, s_b):
    m = jnp.maximum(jnp.max(s_a, axis=-1, keepdims=True), jnp.max(s_b, axis=-1, keepdims=True))
    e_a = jnp.exp(s_a - m)
    e_b = jnp.exp(s_b - m)
    return e_a, e_b, jnp.sum(e_a, axis=-1, keepdims=True) + jnp.sum(e_b, axis=-1, keepdims=True)


def _softmax1(s, exp=jnp.exp):
    m = jnp.max(s, axis=-1, keepdims=True)
    e = exp(s - m)
    return e, jnp.sum(e, axis=-1, keepdims=True)


def _silu_gate(o, z):
    z = z.astype(F32)
    return o * (z * _sigmoid(z))


def _split_maps(q):
    lane = lax.broadcasted_iota(jnp.int32, q.shape, 1)
    zero = jnp.zeros_like(q)
    return jnp.where(lane < DC, q, zero), jnp.where(lane >= DC, q, zero)


def _lam(lq1_ref, lk1_ref, lq2_ref, lk2_ref, li_ref):
    return (jnp.exp(jnp.sum(lq1_ref[...] * lk1_ref[...], axis=-1, keepdims=True))
            - jnp.exp(jnp.sum(lq2_ref[...] * lk2_ref[...], axis=-1, keepdims=True)) + li_ref[:, 0:1])


UNITS = 4
ATTN_TQ = 512
SOFTMAX_ROWS = 128
MAX_CHAINS = 2


def _scores_stage(q, kl, kc, s_ref):
    length = kl.shape[0]
    s_ref[:, :length] = _dot_nt(q, kl)
    s_ref[:, length:] = _dot_nt(q, kc)


def _softmax_pv_stage(s_ref, p_ref, v_ext):
    tq, lk = s_ref.shape
    blocks = [slice(k * HEAD_DIM, (k + 1) * HEAD_DIM) for k in range(lk // HEAD_DIM)]
    for r0 in range(0, tq, SOFTMAX_ROWS):
        rows = slice(r0, r0 + SOFTMAX_ROWS)
        parts = [s_ref[rows, blk] for blk in blocks[:MAX_CHAINS]]
        for k, blk in enumerate(blocks[MAX_CHAINS:]):
            parts[k % MAX_CHAINS] = jnp.maximum(parts[k % MAX_CHAINS], s_ref[rows, blk])
        m = functools.reduce(jnp.maximum, parts)
        m = jnp.broadcast_to(jnp.max(m, axis=-1, keepdims=True), (SOFTMAX_ROWS, HEAD_DIM))
        for blk in blocks:
            p_ref[rows, blk] = jnp.exp2(s_ref[rows, blk] - m).astype(p_ref.dtype)
    o_ext = _dot(p_ref[...], v_ext)
    return o_ext[:, :HEAD_DIM], o_ext[:, HEAD_DIM:HEAD_DIM + 1]


def _fill_v_ext(v_ext_ref, vl, vc):
    length = vl.shape[0]
    v_ext_ref[:length, :HEAD_DIM] = vl
    v_ext_ref[length:, :HEAD_DIM] = vc
    v_ext_ref[:, HEAD_DIM:] = jnp.ones((v_ext_ref.shape[0], HEAD_DIM), v_ext_ref.dtype)


def _pipelined_units(first_step, prepare, unit_q, next_tile_q, unit_k, unit_v, finish, scratch):
    s_bufs, p_bufs = scratch[0:2], scratch[2:4]

    @pl.when(first_step)
    def _():
        prepare()
        _scores_stage(unit_q(0), *unit_k(0), s_bufs[0])

    for r in range(UNITS):
        cur, nxt = r % 2, (r + 1) % 2
        if r + 1 < UNITS:
            _scores_stage(unit_q(r + 1), *unit_k(r + 1), s_bufs[nxt])
        else:
            _scores_stage(next_tile_q(), *unit_k(0), s_bufs[nxt])
        o, l = _softmax_pv_stage(s_bufs[cur], p_bufs[cur], unit_v(r))
        finish(r, o, l)


def _pipeline_scratch(tq, lk, n_v):
    return ([pltpu.VMEM((tq, lk), F32)] * 2 + [pltpu.VMEM((tq, lk), BF16)] * 2
            + [pltpu.VMEM((n_v, lk, 2 * HEAD_DIM), BF16)])


def _gqa_kernel(q_ref, qn_ref, kl_ref, vl_ref, kc_ref, vc_ref, z_ref, o_ref, *scratch):
    def cols(r):
        return slice(r * HEAD_DIM, (r + 1) * HEAD_DIM)

    def finish(r, o, l):
        o_ref[0, :, cols(r)] = _silu_gate(o / l, z_ref[0, :, cols(r)]).astype(o_ref.dtype)

    v_ext_ref = scratch[-1]
    _pipelined_units(
        pl.program_id(2) == 0,
        lambda: _fill_v_ext(v_ext_ref.at[0], vl_ref[0], vc_ref[0]),
        lambda r: q_ref[0, :, cols(r)],
        lambda: qn_ref[0],
        lambda r: (kl_ref[0], kc_ref[0]),
        lambda r: v_ext_ref[0],
        finish, scratch)


def _gqa_call(p, pc):
    b, length, _ = p.shape
    lc = pc.shape[1]
    tq = _row_tile(length, ATTN_TQ)
    n_i = length // tq
    assert H_B // KV_B == UNITS
    gw = UNITS * HEAD_DIM
    kv = lambda n, off: pl.BlockSpec((1, n, HEAD_DIM), lambda bi, g, i: (bi, 0, off // HEAD_DIM + g))
    return pl.pallas_call(
        _gqa_kernel,
        grid=(b, KV_B, n_i),
        in_specs=[
            pl.BlockSpec((1, tq, gw), lambda bi, g, i: (bi, i, OFF_BQ // gw + g)),
            pl.BlockSpec((1, tq, HEAD_DIM),
                         lambda bi, g, i: (bi, jnp.minimum(i + 1, n_i - 1), OFF_BQ // HEAD_DIM + UNITS * g)),
            kv(length, OFF_BK), kv(length, OFF_BV), kv(lc, OFF_BK), kv(lc, OFF_BV),
            pl.BlockSpec((1, tq, gw), lambda bi, g, i: (bi, i, OFF_BZ // gw + g)),
        ],
        out_specs=pl.BlockSpec((1, tq, gw), lambda bi, g, i: (bi, i, g)),
        out_shape=jax.ShapeDtypeStruct((b, length, B_QW), BF16),
        scratch_shapes=_pipeline_scratch(tq, length + lc, 1),
        compiler_params=_params(3),
        name="gqa",
    )(p, p, p, p, pc, pc, p)


def _diff_core(q, k, v, lam, sw, li):
    q1, q2 = _split_maps(q)
    e1, l1 = _softmax1(_dot_nt(q1, k), jnp.exp2)
    e2, l2 = _softmax1(_dot_nt(q2, k), jnp.exp2)
    a = (e1 * (1.0 / l1) - e2 * (lam / l2)).astype(BF16)
    o = _head_rmsnorm(_dot(a, v), sw)
    return o * (1.0 - li)


def _diff_kernel(q_ref, qn_ref, kl_ref, vl_ref, kc_ref, vc_ref, z_ref, lq1_ref, lk1_ref, lq2_ref, lk2_ref, li_ref,
                 sw_ref, o_ref, *scratch):
    lam = _lam(lq1_ref, lk1_ref, lq2_ref, lk2_ref, li_ref)
    pending = {}

    def cols(hh):
        return slice(hh * HEAD_DIM, (hh + 1) * HEAD_DIM)

    def finish(r, o, l):
        hh = r // 2
        if r % 2 == 0:
            pending[hh] = o / l
            return
        o = pending.pop(hh) - o * (lam / l)
        o = _head_rmsnorm(o, sw_ref[...]) * (1.0 - li_ref[...])
        o_ref[0, :, cols(hh)] = _silu_gate(o, z_ref[0, :, cols(hh)]).astype(o_ref.dtype)

    v_ext_ref = scratch[-1]

    def prepare():
        for hh in range(UNITS // 2):
            _fill_v_ext(v_ext_ref.at[hh], vl_ref[0, :, cols(hh)], vc_ref[0, :, cols(hh)])

    _pipelined_units(
        pl.program_id(2) == 0,
        prepare,
        lambda r: _split_maps(q_ref[0, :, cols(r // 2)])[r % 2],
        lambda: _split_maps(qn_ref[0])[0],
        lambda r: (kl_ref[0, :, cols(r // 2)], kc_ref[0, :, cols(r // 2)]),
        lambda r: v_ext_ref[r // 2],
        finish, scratch)


def _diff_call(p, pc, lam_args, li, sw):
    b, length, _ = p.shape
    lc = pc.shape[1]
    tq = _row_tile(length, ATTN_TQ)
    n_i = length // tq
    heads = UNITS // 2
    assert H_C % heads == 0
    pw = heads * HEAD_DIM
    small = lambda w: pl.BlockSpec((1, w), lambda bi, g, i: (0, 0))
    kv = lambda n, off: pl.BlockSpec((1, n, pw), lambda bi, g, i: (bi, 0, off // pw + g))
    return pl.pallas_call(
        _diff_kernel,
        grid=(b, H_C // heads, n_i),
        in_specs=[
            pl.BlockSpec((1, tq, pw), lambda bi, g, i: (bi, i, OFF_CQ // pw + g)),
            pl.BlockSpec((1, tq, HEAD_DIM),
                         lambda bi, g, i: (bi, jnp.minimum(i + 1, n_i - 1), OFF_CQ // HEAD_DIM + heads * g)),
            kv(length, OFF_CK), kv(length, OFF_CV), kv(lc, OFF_CK), kv(lc, OFF_CV),
            pl.BlockSpec((1, tq, pw), lambda bi, g, i: (bi, i, OFF_CZ // pw + g)),
            small(DC), small(DC), small(DC), small(DC), small(HEAD_DIM), small(HEAD_DIM),
        ],
        out_specs=pl.BlockSpec((1, tq, pw), lambda bi, g, i: (bi, i, g)),
        out_shape=jax.ShapeDtypeStruct((b, length, C_W), BF16),
        scratch_shapes=_pipeline_scratch(tq, length + lc, heads),
        compiler_params=_params(3),
        name="diff",
    )(p, p, p, p, pc, pc, p, *lam_args, li, sw)


def _nbr_key_row0(t, rows):
    return jnp.clip(t * NBR_R - WIN_H // 2, 0, rows - NBR_KR)


def _nbr_kernel(q_ref, kl_ref, vl_ref, kc_ref, vc_ref, z_ref, bias_ref, o_ref, *, rows):
    t = pl.program_id(1)
    span = NBR_KR * GRID_W
    k_off = pl.multiple_of(_nbr_key_row0(t, rows) * GRID_W, NBR_R * GRID_W)
    for hh in range(H_A):
        cols = slice(hh * HEAD_DIM, (hh + 1) * HEAD_DIM)
        q = q_ref[0, :, cols]
        s_win = _dot_nt(q, kl_ref[0, pl.ds(k_off, span), cols]) + bias_ref[hh, 0]
        s_ctx = _dot_nt(q, kc_ref[0, :, cols])
        e_w, e_c, l = _softmax2(s_win, s_ctx)
        o = _dot(e_w.astype(BF16), vl_ref[0, pl.ds(k_off, span), cols]) + _dot(e_c.astype(BF16), vc_ref[0, :, cols])
        o_ref[0, :, cols] = _silu_gate(o / l, z_ref[0, :, cols]).astype(o_ref.dtype)


def _nbr_bias_selectors(rows):
    nt = rows // NBR_R
    kh = min(WIN_H, rows)
    n_dr, n_dc = 2 * WIN_H - 1, 2 * WIN_W - 1
    qc = np.arange(GRID_W)[:, None]
    kc = np.arange(GRID_W)[None, :]
    dc = np.clip(kc - qc, -(WIN_W - 1), WIN_W - 1) + (WIN_W - 1)
    cs = np.clip(qc - WIN_W // 2, 0, GRID_W - WIN_W)
    in_win = (kc >= cs) & (kc < cs + WIN_W)
    sel_dc = (dc[None] == np.arange(n_dc)[:, None, None]).astype(np.float32)
    i = np.arange(NBR_R)[:, None]
    jr = np.arange(NBR_KR)[None, :]
    sel_dr, in_band = [], []
    for t in (0, 1, nt - 1):
        r0 = NBR_R * t
        k0 = int(np.clip(r0 - WIN_H // 2, 0, rows - NBR_KR))
        rq = r0 + i
        rk = k0 + jr
        rs = np.clip(rq - kh // 2, 0, rows - kh)
        band = (rk >= rs) & (rk < rs + kh)
        dr = np.clip(rk - rq + (WIN_H - 1), 0, n_dr - 1)
        sel_dr.append((dr[..., None] == np.arange(n_dr)) & band[..., None])
        in_band.append(band)
    sel_dr = np.stack(sel_dr).astype(np.float32)
    valid = np.stack(in_band)[:, :, None, :, None] & in_win[None, None, :, None, :]
    return sel_dc, sel_dr, valid


def _nbr_bias_tables(rpb, rows):
    sel_dc, sel_dr, valid = _nbr_bias_selectors(rows)
    by_col = jnp.einsum("lhrd,dqk->lhrqk", rpb, sel_dc, precision=lax.Precision.HIGHEST)
    bias = jnp.einsum("tijr,lhrqk->lhtiqjk", sel_dr, by_col, precision=lax.Precision.HIGHEST)
    bias = jnp.where(valid[None, None], bias, NEG)
    return bias.reshape(rpb.shape[0], H_A, 3, NBR_R * GRID_W, NBR_KR * GRID_W)


def _nbr_call(p, pc, bias):
    b, length, _ = p.shape
    lc = pc.shape[1]
    rows = length // GRID_W
    assert rows % NBR_R == 0 and rows >= NBR_KR
    tq = NBR_R * GRID_W
    nt = rows // NBR_R
    blk = lambda n, col: pl.BlockSpec((1, n, A_W), lambda bi, t: (bi, 0, col))
    tile = lambda col: pl.BlockSpec((1, tq, A_W), lambda bi, t: (bi, t, col))
    bias_type = lambda bi, t: (0, jnp.where(t == 0, 0, jnp.where(t == nt - 1, 2, 1)), 0, 0)
    return pl.pallas_call(
        functools.partial(_nbr_kernel, rows=rows),
        grid=(b, nt),
        in_specs=[
            tile(OFF_AQ // A_W), blk(length, OFF_AK // A_W), blk(length, OFF_AV // A_W),
            blk(lc, OFF_AK // A_W), blk(lc, OFF_AV // A_W), tile(OFF_AZ // A_W),
            pl.BlockSpec((H_A, 1, tq, NBR_KR * GRID_W), bias_type),
        ],
        out_specs=pl.BlockSpec((1, tq, A_W), lambda bi, t: (bi, t, 0)),
        out_shape=jax.ShapeDtypeStruct((b, length, A_W), BF16),
        compiler_params=_params(2),
        name="nbr",
    )(p, p, p, pc, pc, p, bias)


def _ctx_attn_kernel(p_ref, lq1_ref, lk1_ref, lq2_ref, lk2_ref, li_ref, sw_ref, oa_ref, ob_ref, oc_ref):
    def col(off, hh):
        return p_ref[0, :, off + hh * HEAD_DIM: off + (hh + 1) * HEAD_DIM]

    def plain(q, k, v, exp):
        e, l = _softmax1(_dot_nt(q, k), exp)
        return _dot(e.astype(BF16), v) / l

    for hh in range(H_A):
        o = plain(col(OFF_AQ, hh), col(OFF_AK, hh), col(OFF_AV, hh), jnp.exp)
        oa_ref[0, :, hh * HEAD_DIM:(hh + 1) * HEAD_DIM] = _silu_gate(o, col(OFF_AZ, hh)).astype(oa_ref.dtype)
    rep = H_B // KV_B
    for hh in range(H_B):
        o = plain(col(OFF_BQ, hh), col(OFF_BK, hh // rep), col(OFF_BV, hh // rep), jnp.exp2)
        ob_ref[0, :, hh * HEAD_DIM:(hh + 1) * HEAD_DIM] = _silu_gate(o, col(OFF_BZ, hh)).astype(ob_ref.dtype)
    lam = _lam(lq1_ref, lk1_ref, lq2_ref, lk2_ref, li_ref)
    for hh in range(H_C):
        o = _diff_core(col(OFF_CQ, hh), col(OFF_CK, hh), col(OFF_CV, hh), lam, sw_ref[...], li_ref[...])
        oc_ref[0, :, hh * HEAD_DIM:(hh + 1) * HEAD_DIM] = _silu_gate(o, col(OFF_CZ, hh)).astype(oc_ref.dtype)


def _ctx_attn_call(pc, lam_args, li, sw):
    b, lc, _ = pc.shape
    small = lambda w: pl.BlockSpec((1, w), lambda bi: (0, 0))
    out = lambda w: pl.BlockSpec((1, lc, w), lambda bi: (bi, 0, 0))
    return pl.pallas_call(
        _ctx_attn_kernel,
        grid=(b,),
        in_specs=[pl.BlockSpec((1, lc, P_W), lambda bi: (bi, 0, 0)),
                  small(DC), small(DC), small(DC), small(DC), small(HEAD_DIM), small(HEAD_DIM)],
        out_specs=[out(A_W), out(B_QW), out(C_W)],
        out_shape=[jax.ShapeDtypeStruct((b, lc, A_W), BF16), jax.ShapeDtypeStruct((b, lc, B_QW), BF16),
                   jax.ShapeDtypeStruct((b, lc, C_W), BF16)],
        compiler_params=_params(1),
        name="ctx_attn",
    )(pc, *lam_args, li, sw)


def _residual(aa_ref, ab_ref, ac_ref, g_ref, wa_ref, wb_ref, wc_ref, wo_ref, x_ref, mod_ref):
    d = wo_ref.shape[0]
    y = None
    for k, (a_ref, w_ref) in enumerate(((aa_ref, wa_ref), (ab_ref, wb_ref), (ac_ref, wc_ref))):
        t = g_ref[0, :, k * d:(k + 1) * d].astype(F32) * _dot(a_ref[0], w_ref[...])
        y = t if y is None else y + t
    gate = mod_ref[0, 2:3, :]
    return x_ref[0] + gate * _dot(y.astype(BF16), wo_ref[...])


def _outproj_next_kernel(*refs):
    nw_ref, modn_ref, xo_ref, ho_ref = refs[-4:]
    x = _residual(*refs[:-4])
    xo_ref[0] = x
    ho_ref[0] = _modulated_norm(x, nw_ref[...], modn_ref[0]).astype(ho_ref.dtype)


def _outproj_final_kernel(*refs):
    nw_ref, o_ref = refs[-2:]
    o_ref[0] = _rms(_residual(*refs[:-2])) * nw_ref[...]


def _outproj_call(branches, g, weights, x, mod, nw, mod_next):
    b, length, d = x.shape
    tm = _row_tile(length, 256)
    rows = lambda w: pl.BlockSpec((1, tm, w), lambda bi, i: (bi, i, 0))
    resident = lambda w: pl.BlockSpec(w.shape, lambda bi, i: (0, 0), pipeline_mode=pl.Buffered(1))
    row_spec = rows(d)
    in_specs = ([rows(a.shape[-1]) for a in branches] + [rows(g.shape[-1])] + [resident(w) for w in weights]
                + [row_spec, _mod_spec(mod), pl.BlockSpec((1, d), lambda bi, i: (0, 0))])
    args = [*branches, g, *weights, x, mod, nw.reshape(1, d)]
    if mod_next is None:
        return pl.pallas_call(
            _outproj_final_kernel, grid=(b, length // tm), in_specs=in_specs, out_specs=row_spec,
            out_shape=jax.ShapeDtypeStruct((b, length, d), F32), compiler_params=_params(2), name="outproj_final",
        )(*args)
    return pl.pallas_call(
        _outproj_next_kernel, grid=(b, length // tm), in_specs=in_specs + [_mod_spec(mod_next)],
        out_specs=[row_spec, row_spec],
        out_shape=[jax.ShapeDtypeStruct((b, length, d), F32), jax.ShapeDtypeStruct((b, length, d), BF16)],
        compiler_params=_params(2), name="outproj",
    )(*args, mod_next)


def _rope_tables(length):
    pos = jnp.arange(length)
    row = (pos // GRID_W).astype(F32)[:, None]
    col = (pos % GRID_W).astype(F32)[:, None]

    def cos_sin(rot_dim):
        n = rot_dim // 4
        inv_freq = ROPE_THETA ** (-jnp.arange(n, dtype=F32) / n)
        ang = jnp.concatenate([row * inv_freq, col * inv_freq], axis=-1)
        return jnp.cos(ang), jnp.sin(ang)

    cb, sb = cos_sin(HEAD_DIM)
    cc, sc = cos_sin(DC)
    return (jnp.concatenate([cb, cb], axis=-1), jnp.concatenate([-sb, sb], axis=-1),
            jnp.concatenate([cc, cc, cc, cc], axis=-1), jnp.concatenate([-sc, sc, -sc, sc], axis=-1))


def _identity_tables(length):
    one = jnp.ones((length, HEAD_DIM), F32)
    zero = jnp.zeros((length, HEAD_DIM), F32)
    return one, zero, one, zero


def kernel(x, c, ctx, c_ctx, norm_w, w_ada, b_ada, w_in, b_gate, rpb, q_norm_w, k_norm_w, lam_q1, lam_k1, lam_q2,
           lam_k2, subln_w, w_bo_a, w_bo_b, w_bo_c, w_out, final_norm_w):
    b, length, d = x.shape
    lc = ctx.shape[1]
    depth = w_in.shape[0]
    rows = length // GRID_W

    n_cond = b + 1
    pad = (-n_cond) % 8
    cc = jnp.concatenate([c, c_ctx[None], jnp.zeros((pad, d), F32)], axis=0)
    mod_all = _ada_call(cc, w_ada, b_ada).reshape(depth, n_cond + pad, 3, d)

    tabs_lat = _rope_tables(length)
    tabs_ctx = _identity_tables(_row_tile(b * lc, INPROJ_TM))
    nbr_bias = _nbr_bias_tables(rpb, rows)

    w_p_bf = w_in[:, :, :P_W].astype(BF16)
    w_g_bf = w_in[:, :, P_W:].astype(BF16)
    wa_bf, wb_bf, wc_bf, wo_bf = (w.astype(BF16) for w in (w_bo_a, w_bo_b, w_bo_c, w_out))

    xl, xc = x, ctx
    h = _norm_call(xl, norm_w[0], mod_all[0, :b])
    hc = _norm_call(xc, norm_w[0], mod_all[0, b:b + 1])
    out = None
    for l in range(depth):
        last = l == depth - 1
        mod_l = mod_all[l, :b]
        mod_c = mod_all[l, b:b + 1]
        p, g = _inproj_call(h.reshape(b * length, d), w_p_bf[l], w_g_bf[l], b_gate[l], q_norm_w[l], k_norm_w[l],
                            tabs_lat)
        pc, gc = _inproj_call(hc.reshape(b * lc, d), w_p_bf[l], w_g_bf[l], b_gate[l], q_norm_w[l], k_norm_w[l],
                              tabs_ctx)
        p = p.reshape(b, length, P_W)
        pc = pc.reshape(b, lc, P_W)

        lam_init = 0.8 - 0.6 * float(np.exp(-0.3 * l))
        li = jnp.full((1, HEAD_DIM), lam_init, F32)
        lam_args = [v[l].reshape(1, DC) for v in (lam_q1, lam_k1, lam_q2, lam_k2)]
        sw = subln_w[l].reshape(1, HEAD_DIM)

        o_a = _nbr_call(p, pc, nbr_bias[l])
        o_b = _gqa_call(p, pc)
        o_c = _diff_call(p, pc, lam_args, li, sw)
        weights = (wa_bf[l], wb_bf[l], wc_bf[l], wo_bf[l])
        if last:
            out = _outproj_call((o_a, o_b, o_c), g.reshape(b, length, -1), weights, xl, mod_l, final_norm_w, None)
        else:
            xl, h = _outproj_call((o_a, o_b, o_c), g.reshape(b, length, -1), weights, xl, mod_l, norm_w[l + 1],
                                  mod_all[l + 1, :b])
            xc, hc = _outproj_call(_ctx_attn_call(pc, lam_args, li, sw), gc.reshape(b, lc, -1), weights, xc, mod_c,
                                   norm_w[l + 1], mod_all[l + 1, b:b + 1])
    return out
```

```python
import functools

import numpy as np
import jax
import jax.numpy as jnp
from jax import lax
from jax.experimental import pallas as pl
from jax.experimental.pallas import tpu as pltpu

F32 = jnp.float32
BF16 = jnp.bfloat16

GRID_W = 64
HEAD_DIM = 128
H_A = 4
WIN_H = 8
WIN_W = 16
H_B = 8
KV_B = 2
H_C = 4
DC = HEAD_DIM // 2
ROPE_THETA = 10000.0
EPS = 1e-6
NEG = -1e30
LOG2E = 1.4426950408889634

A_W = H_A * HEAD_DIM
B_QW = H_B * HEAD_DIM
B_KW = KV_B * HEAD_DIM
C_W = H_C * HEAD_DIM

OFF_AQ, OFF_AK, OFF_AV, OFF_AZ = 0, A_W, 2 * A_W, 3 * A_W
OFF_BQ = 4 * A_W
OFF_BK = OFF_BQ + B_QW
OFF_BV = OFF_BK + B_KW
OFF_BZ = OFF_BV + B_KW
OFF_CQ = OFF_BZ + B_QW
OFF_CK = OFF_CQ + C_W
OFF_CV = OFF_CK + C_W
OFF_CZ = OFF_CV + C_W
P_W = OFF_CZ + C_W

TN = 512
TN_GATES = 1024
GATES_TM = 256
INPROJ_TM = 256

NBR_R = 4
NBR_KR = 12

VMEM_LIMIT = 56 * 1024 * 1024


def _params(n_axes):
    return pltpu.CompilerParams(dimension_semantics=("arbitrary",) * n_axes, vmem_limit_bytes=VMEM_LIMIT)


def _sigmoid(x):
    return 1.0 / (1.0 + jnp.exp(-x))


def _dot(a, b):
    return jnp.dot(a, b, preferred_element_type=F32)


def _dot_nt(a, b):
    return lax.dot_general(a, b, (((1,), (1,)), ((), ())), preferred_element_type=F32)


def _head_rmsnorm(y, w):
    return y * lax.rsqrt(jnp.mean(y * y, axis=-1, keepdims=True) + EPS) * w


def _ada_kernel(c_ref, w_ref, b_ref, o_ref):
    a = c_ref[...]
    a = a * _sigmoid(a)
    o_ref[0] = _dot(a.astype(BF16), w_ref[0].astype(BF16)) + b_ref[0]


def _ada_call(cc, w_ada, b_ada):
    depth, d, n3 = w_ada.shape
    rows = cc.shape[0]
    tn = 1536
    return pl.pallas_call(
        _ada_kernel,
        grid=(depth, n3 // tn),
        in_specs=[
            pl.BlockSpec((rows, d), lambda l, j: (0, 0)),
            pl.BlockSpec((1, d, tn), lambda l, j: (l, 0, j)),
            pl.BlockSpec((1, 1, tn), lambda l, j: (l, 0, j)),
        ],
        out_specs=pl.BlockSpec((1, rows, tn), lambda l, j: (l, 0, j)),
        out_shape=jax.ShapeDtypeStruct((depth, rows, n3), F32),
        compiler_params=_params(2),
        name="ada",
    )(cc, w_ada, b_ada.reshape(depth, 1, n3))


def _rms(x):
    return x * lax.rsqrt(jnp.mean(x * x, axis=-1, keepdims=True) + EPS)


def _modulated_norm(x, w, mod):
    return _rms(x) * w * (1.0 + mod[1:2, :]) + mod[0:1, :]


def _norm_kernel(x_ref, w_ref, mod_ref, o_ref):
    o_ref[0] = _modulated_norm(x_ref[0], w_ref[...], mod_ref[0]).astype(o_ref.dtype)


def _row_tile(n, cap):
    t = min(n, cap)
    assert n % t == 0
    return t


def _mod_spec(mod):
    d = mod.shape[-1]
    return pl.BlockSpec((1, 3, d), (lambda bi, i: (bi, 0, 0)) if mod.shape[0] != 1 else (lambda bi, i: (0, 0, 0)))


def _norm_call(x, w, mod):
    b, length, d = x.shape
    tm = _row_tile(length, 512)
    row_spec = pl.BlockSpec((1, tm, d), lambda bi, i: (bi, i, 0))
    return pl.pallas_call(
        _norm_kernel,
        grid=(b, length // tm),
        in_specs=[row_spec, pl.BlockSpec((1, d), lambda bi, i: (0, 0)), _mod_spec(mod)],
        out_specs=row_spec,
        out_shape=jax.ShapeDtypeStruct((b, length, d), BF16),
        compiler_params=_params(2),
        name="norm",
    )(x, w.reshape(1, d), mod)


def _inproj_kernel(h_ref, w_ref, qw_ref, kw_ref, cb_ref, sb_ref, cc_ref, sc_ref, p_ref):
    h = h_ref[...]
    scale_a = HEAD_DIM ** -0.5 * LOG2E
    scale_b = HEAD_DIM ** -0.5 * LOG2E
    scale_c = DC ** -0.5 * LOG2E
    heads = TN // HEAD_DIM

    def rope_b(y):
        return y * cb_ref[...] + pltpu.roll(y, HEAD_DIM // 2, 1) * sb_ref[...]

    def rope_c(y):
        lane = lax.broadcasted_iota(jnp.int32, y.shape, 1)
        first = (lane & (DC // 2)) == 0
        rot = jnp.where(first, pltpu.roll(y, HEAD_DIM - DC // 2, 1), pltpu.roll(y, DC // 2, 1))
        return y * cc_ref[...] + rot * sc_ref[...]

    def finish(col, y):
        if OFF_AQ <= col < OFF_AK:
            return y * scale_a
        if OFF_BQ <= col < OFF_BK:
            return rope_b(_head_rmsnorm(y, qw_ref[...])) * scale_b
        if OFF_BK <= col < OFF_BV:
            return rope_b(_head_rmsnorm(y, kw_ref[...]))
        if OFF_CQ <= col < OFF_CK:
            return rope_c(y) * scale_c
        if OFF_CK <= col < OFF_CV:
            return rope_c(y)
        return y

    for c0 in range(0, p_ref.shape[1], TN):
        acc = _dot(h, w_ref[:, c0:c0 + TN])
        for hh in range(heads):
            col = c0 + hh * HEAD_DIM
            y = finish(col, acc[:, hh * HEAD_DIM:(hh + 1) * HEAD_DIM])
            p_ref[:, col:col + HEAD_DIM] = y.astype(p_ref.dtype)


def _gates_kernel(h_ref, w_ref, bg_ref, g_ref):
    h = h_ref[...]
    for c0 in range(0, g_ref.shape[1], TN_GATES):
        cols = slice(c0, c0 + TN_GATES)
        g_ref[:, cols] = _sigmoid(_dot(h, w_ref[:, cols]) + bg_ref[:, cols]).astype(g_ref.dtype)


def _inproj_call(h, w_p, w_g, b_gate, qw, kw, tabs):
    m, d = h.shape
    gate_w = w_g.shape[1]
    assert w_p.shape[1] == P_W and P_W % TN == 0 and gate_w % TN_GATES == 0
    ltab = tabs[0].shape[0]
    tm = _row_tile(ltab, INPROJ_TM)
    assert m % tm == 0
    nt = ltab // tm
    tab_spec = pl.BlockSpec((tm, HEAD_DIM), lambda i: (i % nt, 0))
    vec_spec = pl.BlockSpec((1, HEAD_DIM), lambda i: (0, 0))
    p = pl.pallas_call(
        _inproj_kernel,
        grid=(m // tm,),
        in_specs=[pl.BlockSpec((tm, d), lambda i: (i, 0)),
                  pl.BlockSpec((d, P_W), lambda i: (0, 0), pipeline_mode=pl.Buffered(1)),
                  vec_spec, vec_spec, tab_spec, tab_spec, tab_spec, tab_spec],
        out_specs=pl.BlockSpec((tm, P_W), lambda i: (i, 0)),
        out_shape=jax.ShapeDtypeStruct((m, P_W), BF16),
        compiler_params=_params(1),
        name="inproj",
    )(h, w_p, qw.reshape(1, HEAD_DIM), kw.reshape(1, HEAD_DIM), *tabs)
    tg = _row_tile(m, GATES_TM)
    g = pl.pallas_call(
        _gates_kernel,
        grid=(m // tg,),
        in_specs=[pl.BlockSpec((tg, d), lambda i: (i, 0)),
                  pl.BlockSpec((d, gate_w), lambda i: (0, 0), pipeline_mode=pl.Buffered(1)),
                  pl.BlockSpec((1, gate_w), lambda i: (0, 0))],
        out_specs=pl.BlockSpec((tg, gate_w), lambda i: (i, 0)),
        out_shape=jax.ShapeDtypeStruct((m, gate_w), BF16),
        compiler_params=_params(1),
        name="gates",
    )(h, w_g, b_gate.reshape(1, gate_w))
    return p, g


def _softmax2(s_a, s_b):
    m = jnp.maximum(jnp.max(s_a, axis=-1, keepdims=True), jnp.max(s_b, axis=-1, keepdims=True))
    e_a = jnp.exp2(s_a - m)
    e_b = jnp.exp2(s_b - m)
    return e_a, e_b, jnp.sum(e_a, axis=-1, keepdims=True) + jnp.sum(e_b, axis=-1, keepdims=True)


def _softmax1(s):
    m = jnp.max(s, axis=-1, keepdims=True)
    e = jnp.exp2(s - m)
    return e, jnp.sum(e, axis=-1, keepdims=True)


def _silu_gate(o, z):
    z = z.astype(F32)
    return o * (z * _sigmoid(z))


def _split_maps(q):
    lane = lax.broadcasted_iota(jnp.int32, q.shape, 1)
    zero = jnp.zeros_like(q)
    return jnp.where(lane < DC, q, zero), jnp.where(lane >= DC, q, zero)


def _lam(lq1_ref, lk1_ref, lq2_ref, lk2_ref, li_ref):
    return (jnp.exp(jnp.sum(lq1_ref[...] * lk1_ref[...], axis=-1, keepdims=True))
            - jnp.exp(jnp.sum(lq2_ref[...] * lk2_ref[...], axis=-1, keepdims=True)) + li_ref[:, 0:1])


UNITS = 4
ATTN_TQ = 512
SOFTMAX_ROWS = 128
MAX_CHAINS = 2


def _scores_stage(q, kl, kc, s_ref):
    length = kl.shape[0]
    s_ref[:, :length] = _dot_nt(q, kl)
    s_ref[:, length:] = _dot_nt(q, kc)


def _softmax_pv_stage(s_ref, p_ref, v_ext):
    tq, lk = s_ref.shape
    blocks = [slice(k * HEAD_DIM, (k + 1) * HEAD_DIM) for k in range(lk // HEAD_DIM)]
    for r0 in range(0, tq, SOFTMAX_ROWS):
        rows = slice(r0, r0 + SOFTMAX_ROWS)
        parts = [s_ref[rows, blk] for blk in blocks[:MAX_CHAINS]]
        for k, blk in enumerate(blocks[MAX_CHAINS:]):
            parts[k % MAX_CHAINS] = jnp.maximum(parts[k % MAX_CHAINS], s_ref[rows, blk])
        m = functools.reduce(jnp.maximum, parts)
        m = jnp.broadcast_to(jnp.max(m, axis=-1, keepdims=True), (SOFTMAX_ROWS, HEAD_DIM))
        for blk in blocks:
            p_ref[rows, blk] = jnp.exp2(s_ref[rows, blk] - m).astype(p_ref.dtype)
    o_ext = _dot(p_ref[...], v_ext)
    return o_ext[:, :HEAD_DIM], o_ext[:, HEAD_DIM:HEAD_DIM + 1]


def _fill_v_ext(v_ext_ref, vl, vc):
    length = vl.shape[0]
    v_ext_ref[:length, :HEAD_DIM] = vl
    v_ext_ref[length:, :HEAD_DIM] = vc
    v_ext_ref[:, HEAD_DIM:] = jnp.ones((v_ext_ref.shape[0], HEAD_DIM), v_ext_ref.dtype)


def _pipelined_units(first_step, prepare, unit_q, next_tile_q, unit_k, unit_v, finish, scratch):
    s_bufs, p_bufs = scratch[0:2], scratch[2:4]

    @pl.when(first_step)
    def _():
        prepare()
        _scores_stage(unit_q(0), *unit_k(0), s_bufs[0])

    for r in range(UNITS):
        cur, nxt = r % 2, (r + 1) % 2
        if r + 1 < UNITS:
            _scores_stage(unit_q(r + 1), *unit_k(r + 1), s_bufs[nxt])
        else:
            _scores_stage(next_tile_q(), *unit_k(0), s_bufs[nxt])
        o, l = _softmax_pv_stage(s_bufs[cur], p_bufs[cur], unit_v(r))
        finish(r, o, l)


def _pipeline_scratch(tq, lk, n_v):
    return ([pltpu.VMEM((tq, lk), F32)] * 2 + [pltpu.VMEM((tq, lk), BF16)] * 2
            + [pltpu.VMEM((n_v, lk, 2 * HEAD_DIM), BF16)])


def _gqa_kernel(q_ref, qn_ref, kl_ref, vl_ref, kc_ref, vc_ref, z_ref, o_ref, *scratch):
    def cols(r):
        return slice(r * HEAD_DIM, (r + 1) * HEAD_DIM)

    def finish(r, o, l):
        o_ref[0, :, cols(r)] = _silu_gate(o / l, z_ref[0, :, cols(r)]).astype(o_ref.dtype)

    v_ext_ref = scratch[-1]
    _pipelined_units(
        pl.program_id(2) == 0,
        lambda: _fill_v_ext(v_ext_ref.at[0], vl_ref[0], vc_ref[0]),
        lambda r: q_ref[0, :, cols(r)],
        lambda: qn_ref[0],
        lambda r: (kl_ref[0], kc_ref[0]),
        lambda r: v_ext_ref[0],
        finish, scratch)


def _gqa_call(p, pc):
    b, length, _ = p.shape
    lc = pc.shape[1]
    tq = _row_tile(length, ATTN_TQ)
    n_i = length // tq
    assert H_B // KV_B == UNITS
    gw = UNITS * HEAD_DIM
    kv = lambda n, off: pl.BlockSpec((1, n, HEAD_DIM), lambda bi, g, i: (bi, 0, off // HEAD_DIM + g))
    return pl.pallas_call(
        _gqa_kernel,
        grid=(b, KV_B, n_i),
        in_specs=[
            pl.BlockSpec((1, tq, gw), lambda bi, g, i: (bi, i, OFF_BQ // gw + g)),
            pl.BlockSpec((1, tq, HEAD_DIM),
                         lambda bi, g, i: (bi, jnp.minimum(i + 1, n_i - 1), OFF_BQ // HEAD_DIM + UNITS * g)),
            kv(length, OFF_BK), kv(length, OFF_BV), kv(lc, OFF_BK), kv(lc, OFF_BV),
            pl.BlockSpec((1, tq, gw), lambda bi, g, i: (bi, i, OFF_BZ // gw + g)),
        ],
        out_specs=pl.BlockSpec((1, tq, gw), lambda bi, g, i: (bi, i, g)),
        out_shape=jax.ShapeDtypeStruct((b, length, B_QW), BF16),
        scratch_shapes=_pipeline_scratch(tq, length + lc, 1),
        compiler_params=_params(3),
        name="gqa",
    )(p, p, p, p, pc, pc, p)


def _diff_core(q, k, v, lam, sw, li):
    q1, q2 = _split_maps(q)
    e1, l1 = _softmax1(_dot_nt(q1, k))
    e2, l2 = _softmax1(_dot_nt(q2, k))
    a = (e1 * (1.0 / l1) - e2 * (lam / l2)).astype(BF16)
    o = _head_rmsnorm(_dot(a, v), sw)
    return o * (1.0 - li)


def _diff_kernel(q_ref, qn_ref, kl_ref, vl_ref, kc_ref, vc_ref, z_ref, lq1_ref, lk1_ref, lq2_ref, lk2_ref, li_ref,
                 sw_ref, o_ref, *scratch):
    lam = _lam(lq1_ref, lk1_ref, lq2_ref, lk2_ref, li_ref)
    pending = {}

    def cols(hh):
        return slice(hh * HEAD_DIM, (hh + 1) * HEAD_DIM)

    def finish(r, o, l):
        hh = r // 2
        if r % 2 == 0:
            pending[hh] = o / l
            return
        o = pending.pop(hh) - o * (lam / l)
        o = _head_rmsnorm(o, sw_ref[...]) * (1.0 - li_ref[...])
        o_ref[0, :, cols(hh)] = _silu_gate(o, z_ref[0, :, cols(hh)]).astype(o_ref.dtype)

    v_ext_ref = scratch[-1]

    def prepare():
        for hh in range(UNITS // 2):
            _fill_v_ext(v_ext_ref.at[hh], vl_ref[0, :, cols(hh)], vc_ref[0, :, cols(hh)])

    _pipelined_units(
        pl.program_id(2) == 0,
        prepare,
        lambda r: _split_maps(q_ref[0, :, cols(r // 2)])[r % 2],
        lambda: _split_maps(qn_ref[0])[0],
        lambda r: (kl_ref[0, :, cols(r // 2)], kc_ref[0, :, cols(r // 2)]),
        lambda r: v_ext_ref[r // 2],
        finish, scratch)


def _diff_call(p, pc, lam_args, li, sw):
    b, length, _ = p.shape
    lc = pc.shape[1]
    tq = _row_tile(length, ATTN_TQ)
    n_i = length // tq
    heads = UNITS // 2
    assert H_C % heads == 0
    pw = heads * HEAD_DIM
    small = lambda w: pl.BlockSpec((1, w), lambda bi, g, i: (0, 0))
    kv = lambda n, off: pl.BlockSpec((1, n, pw), lambda bi, g, i: (bi, 0, off // pw + g))
    return pl.pallas_call(
        _diff_kernel,
        grid=(b, H_C // heads, n_i),
        in_specs=[
            pl.BlockSpec((1, tq, pw), lambda bi, g, i: (bi, i, OFF_CQ // pw + g)),
            pl.BlockSpec((1, tq, HEAD_DIM),
                         lambda bi, g, i: (bi, jnp.minimum(i + 1, n_i - 1), OFF_CQ // HEAD_DIM + heads * g)),
            kv(length, OFF_CK), kv(length, OFF_CV), kv(lc, OFF_CK), kv(lc, OFF_CV),
            pl.BlockSpec((1, tq, pw), lambda bi, g, i: (bi, i, OFF_CZ // pw + g)),
            small(DC), small(DC), small(DC), small(DC), small(HEAD_DIM), small(HEAD_DIM),
        ],
        out_specs=pl.BlockSpec((1, tq, pw), lambda bi, g, i: (bi, i, g)),
        out_shape=jax.ShapeDtypeStruct((b, length, C_W), BF16),
        scratch_shapes=_pipeline_scratch(tq, length + lc, heads),
        compiler_params=_params(3),
        name="diff",
    )(p, p, p, p, pc, pc, p, *lam_args, li, sw)


def _nbr_key_row0(t, rows):
    return jnp.clip(t * NBR_R - WIN_H // 2, 0, rows - NBR_KR)


def _nbr_kernel(q_ref, kl_ref, vl_ref, kc_ref, vc_ref, z_ref, bias_ref, o_ref, *, rows):
    t = pl.program_id(1)
    span = NBR_KR * GRID_W
    k_off = pl.multiple_of(_nbr_key_row0(t, rows) * GRID_W, NBR_R * GRID_W)
    for hh in range(H_A):
        cols = slice(hh * HEAD_DIM, (hh + 1) * HEAD_DIM)
        q = q_ref[0, :, cols]
        s_win = _dot_nt(q, kl_ref[0, pl.ds(k_off, span), cols]) + bias_ref[hh, 0]
        s_ctx = _dot_nt(q, kc_ref[0, :, cols])
        e_w, e_c, l = _softmax2(s_win, s_ctx)
        o = _dot(e_w.astype(BF16), vl_ref[0, pl.ds(k_off, span), cols]) + _dot(e_c.astype(BF16), vc_ref[0, :, cols])
        o_ref[0, :, cols] = _silu_gate(o / l, z_ref[0, :, cols]).astype(o_ref.dtype)


def _nbr_bias_selectors(rows):
    nt = rows // NBR_R
    kh = min(WIN_H, rows)
    n_dr, n_dc = 2 * WIN_H - 1, 2 * WIN_W - 1
    qc = np.arange(GRID_W)[:, None]
    kc = np.arange(GRID_W)[None, :]
    dc = np.clip(kc - qc, -(WIN_W - 1), WIN_W - 1) + (WIN_W - 1)
    cs = np.clip(qc - WIN_W // 2, 0, GRID_W - WIN_W)
    in_win = (kc >= cs) & (kc < cs + WIN_W)
    sel_dc = (dc[None] == np.arange(n_dc)[:, None, None]).astype(np.float32)
    i = np.arange(NBR_R)[:, None]
    jr = np.arange(NBR_KR)[None, :]
    sel_dr, in_band = [], []
    for t in (0, 1, nt - 1):
        r0 = NBR_R * t
        k0 = int(np.clip(r0 - WIN_H // 2, 0, rows - NBR_KR))
        rq = r0 + i
        rk = k0 + jr
        rs = np.clip(rq - kh // 2, 0, rows - kh)
        band = (rk >= rs) & (rk < rs + kh)
        dr = np.clip(rk - rq + (WIN_H - 1), 0, n_dr - 1)
        sel_dr.append((dr[..., None] == np.arange(n_dr)) & band[..., None])
        in_band.append(band)
    sel_dr = np.stack(sel_dr).astype(np.float32)
    valid = np.stack(in_band)[:, :, None, :, None] & in_win[None, None, :, None, :]
    return sel_dc, sel_dr, valid


def _nbr_bias_tables(rpb, rows):
    sel_dc, sel_dr, valid = _nbr_bias_selectors(rows)
    by_col = jnp.einsum("lhrd,dqk->lhrqk", rpb, sel_dc, precision=lax.Precision.HIGHEST)
    bias = jnp.einsum("tijr,lhrqk->lhtiqjk", sel_dr, by_col, precision=lax.Precision.HIGHEST)
    bias = jnp.where(valid[None, None], bias * LOG2E, NEG)
    return bias.reshape(rpb.shape[0], H_A, 3, NBR_R * GRID_W, NBR_KR * GRID_W)


def _nbr_call(p, pc, bias):
    b, length, _ = p.shape
    lc = pc.shape[1]
    rows = length // GRID_W
    assert rows % NBR_R == 0 and rows >= NBR_KR
    tq = NBR_R * GRID_W
    nt = rows // NBR_R
    blk = lambda n, col: pl.BlockSpec((1, n, A_W), lambda bi, t: (bi, 0, col))
    tile = lambda col: pl.BlockSpec((1, tq, A_W), lambda bi, t: (bi, t, col))
    bias_type = lambda bi, t: (0, jnp.where(t == 0, 0, jnp.where(t == nt - 1, 2, 1)), 0, 0)
    return pl.pallas_call(
        functools.partial(_nbr_kernel, rows=rows),
        grid=(b, nt),
        in_specs=[
            tile(OFF_AQ // A_W), blk(length, OFF_AK // A_W), blk(length, OFF_AV // A_W),
            blk(lc, OFF_AK // A_W), blk(lc, OFF_AV // A_W), tile(OFF_AZ // A_W),
            pl.BlockSpec((H_A, 1, tq, NBR_KR * GRID_W), bias_type),
        ],
        out_specs=pl.BlockSpec((1, tq, A_W), lambda bi, t: (bi, t, 0)),
        out_shape=jax.ShapeDtypeStruct((b, length, A_W), BF16),
        compiler_params=_params(2),
        name="nbr",
    )(p, p, p, pc, pc, p, bias)


def _ctx_attn_kernel(p_ref, lq1_ref, lk1_ref, lq2_ref, lk2_ref, li_ref, sw_ref, oa_ref, ob_ref, oc_ref):
    def col(off, hh):
        return p_ref[0, :, off + hh * HEAD_DIM: off + (hh + 1) * HEAD_DIM]

    def plain(q, k, v):
        e, l = _softmax1(_dot_nt(q, k))
        return _dot(e.astype(BF16), v) / l

    for hh in range(H_A):
        o = plain(col(OFF_AQ, hh), col(OFF_AK, hh), col(OFF_AV, hh))
        oa_ref[0, :, hh * HEAD_DIM:(hh + 1) * HEAD_DIM] = _silu_gate(o, col(OFF_AZ, hh)).astype(oa_ref.dtype)
    rep = H_B // KV_B
    for hh in range(H_B):
        o = plain(col(OFF_BQ, hh), col(OFF_BK, hh // rep), col(OFF_BV, hh // rep))
        ob_ref[0, :, hh * HEAD_DIM:(hh + 1) * HEAD_DIM] = _silu_gate(o, col(OFF_BZ, hh)).astype(ob_ref.dtype)
    lam = _lam(lq1_ref, lk1_ref, lq2_ref, lk2_ref, li_ref)
    for hh in range(H_C):
        o = _diff_core(col(OFF_CQ, hh), col(OFF_CK, hh), col(OFF_CV, hh), lam, sw_ref[...], li_ref[...])
        oc_ref[0, :, hh * HEAD_DIM:(hh + 1) * HEAD_DIM] = _silu_gate(o, col(OFF_CZ, hh)).astype(oc_ref.dtype)


def _ctx_attn_call(pc, lam_args, li, sw):
    b, lc, _ = pc.shape
    small = lambda w: pl.BlockSpec((1, w), lambda bi: (0, 0))
    out = lambda w: pl.BlockSpec((1, lc, w), lambda bi: (bi, 0, 0))
    return pl.pallas_call(
        _ctx_attn_kernel,
        grid=(b,),
        in_specs=[pl.BlockSpec((1, lc, P_W), lambda bi: (bi, 0, 0)),
                  small(DC), small(DC), small(DC), small(DC), small(HEAD_DIM), small(HEAD_DIM)],
        out_specs=[out(A_W), out(B_QW), out(C_W)],
        out_shape=[jax.ShapeDtypeStruct((b, lc, A_W), BF16), jax.ShapeDtypeStruct((b, lc, B_QW), BF16),
                   jax.ShapeDtypeStruct((b, lc, C_W), BF16)],
        compiler_params=_params(1),
        name="ctx_attn",
    )(pc, *lam_args, li, sw)


def _residual(aa_ref, ab_ref, ac_ref, g_ref, wa_ref, wb_ref, wc_ref, wo_ref, x_ref, mod_ref):
    d = wo_ref.shape[0]
    y = None
    for k, (a_ref, w_ref) in enumerate(((aa_ref, wa_ref), (ab_ref, wb_ref), (ac_ref, wc_ref))):
        t = g_ref[0, :, k * d:(k + 1) * d].astype(F32) * _dot(a_ref[0], w_ref[...])
        y = t if y is None else y + t
    gate = mod_ref[0, 2:3, :]
    return x_ref[0] + gate * _dot(y.astype(BF16), wo_ref[...])


def _outproj_next_kernel(*refs):
    nw_ref, modn_ref, xo_ref, ho_ref = refs[-4:]
    x = _residual(*refs[:-4])
    xo_ref[0] = x
    ho_ref[0] = _modulated_norm(x, nw_ref[...], modn_ref[0]).astype(ho_ref.dtype)


def _outproj_final_kernel(*refs):
    nw_ref, o_ref = refs[-2:]
    o_ref[0] = _rms(_residual(*refs[:-2])) * nw_ref[...]


def _outproj_call(branches, g, weights, x, mod, nw, mod_next):
    b, length, d = x.shape
    tm = _row_tile(length, 256)
    rows = lambda w: pl.BlockSpec((1, tm, w), lambda bi, i: (bi, i, 0))
    resident = lambda w: pl.BlockSpec(w.shape, lambda bi, i: (0, 0), pipeline_mode=pl.Buffered(1))
    row_spec = rows(d)
    in_specs = ([rows(a.shape[-1]) for a in branches] + [rows(g.shape[-1])] + [resident(w) for w in weights]
                + [row_spec, _mod_spec(mod), pl.BlockSpec((1, d), lambda bi, i: (0, 0))])
    args = [*branches, g, *weights, x, mod, nw.reshape(1, d)]
    if mod_next is None:
        return pl.pallas_call(
            _outproj_final_kernel, grid=(b, length // tm), in_specs=in_specs, out_specs=row_spec,
            out_shape=jax.ShapeDtypeStruct((b, length, d), F32), compiler_params=_params(2), name="outproj_final",
        )(*args)
    return pl.pallas_call(
        _outproj_next_kernel, grid=(b, length // tm), in_specs=in_specs + [_mod_spec(mod_next)],
        out_specs=[row_spec, row_spec],
        out_shape=[jax.ShapeDtypeStruct((b, length, d), F32), jax.ShapeDtypeStruct((b, length, d), BF16)],
        compiler_params=_params(2), name="outproj",
    )(*args, mod_next)


def _rope_tables(length):
    pos = jnp.arange(length)
    row = (pos // GRID_W).astype(F32)[:, None]
    col = (pos % GRID_W).astype(F32)[:, None]

    def cos_sin(rot_dim):
        n = rot_dim // 4
        inv_freq = ROPE_THETA ** (-jnp.arange(n, dtype=F32) / n)
        ang = jnp.concatenate([row * inv_freq, col * inv_freq], axis=-1)
        return jnp.cos(ang), jnp.sin(ang)

    cb, sb = cos_sin(HEAD_DIM)
    cc, sc = cos_sin(DC)
    return (jnp.concatenate([cb, cb], axis=-1), jnp.concatenate([-sb, sb], axis=-1),
            jnp.concatenate([cc, cc, cc, cc], axis=-1), jnp.concatenate([-sc, sc, -sc, sc], axis=-1))


def _identity_tables(length):
    one = jnp.ones((length, HEAD_DIM), F32)
    zero = jnp.zeros((length, HEAD_DIM), F32)
    return one, zero, one, zero


def kernel(x, c, ctx, c_ctx, norm_w, w_ada, b_ada, w_in, b_gate, rpb, q_norm_w, k_norm_w, lam_q1, lam_k1, lam_q2,
           lam_k2, subln_w, w_bo_a, w_bo_b, w_bo_c, w_out, final_norm_w):
    b, length, d = x.shape
    lc = ctx.shape[1]
    depth = w_in.shape[0]
    rows = length // GRID_W

    n_cond = b + 1
    pad = (-n_cond) % 8
    cc = jnp.concatenate([c, c_ctx[None], jnp.zeros((pad, d), F32)], axis=0)
    mod_all = _ada_call(cc, w_ada, b_ada).reshape(depth, n_cond + pad, 3, d)

    tabs_lat = _rope_tables(length)
    tabs_ctx = _identity_tables(_row_tile(b * lc, INPROJ_TM))
    nbr_bias = _nbr_bias_tables(rpb, rows)

    w_p_bf = w_in[:, :, :P_W].astype(BF16)
    w_g_bf = w_in[:, :, P_W:].astype(BF16)
    wa_bf, wb_bf, wc_bf, wo_bf = (w.astype(BF16) for w in (w_bo_a, w_bo_b, w_bo_c, w_out))

    xl, xc = x, ctx
    h = _norm_call(xl, norm_w[0], mod_all[0, :b])
    hc = _norm_call(xc, norm_w[0], mod_all[0, b:b + 1])
    out = None
    for l in range(depth):
        last = l == depth - 1
        mod_l = mod_all[l, :b]
        mod_c = mod_all[l, b:b + 1]
        p, g = _inproj_call(h.reshape(b * length, d), w_p_bf[l], w_g_bf[l], b_gate[l], q_norm_w[l], k_norm_w[l],
                            tabs_lat)
        pc, gc = _inproj_call(hc.reshape(b * lc, d), w_p_bf[l], w_g_bf[l], b_gate[l], q_norm_w[l], k_norm_w[l],
                              tabs_ctx)
        p = p.reshape(b, length, P_W)
        pc = pc.reshape(b, lc, P_W)

        lam_init = 0.8 - 0.6 * float(np.exp(-0.3 * l))
        li = jnp.full((1, HEAD_DIM), lam_init, F32)
        lam_args = [v[l].reshape(1, DC) for v in (lam_q1, lam_k1, lam_q2, lam_k2)]
        sw = subln_w[l].reshape(1, HEAD_DIM)

        o_a = _nbr_call(p, pc, nbr_bias[l])
        o_b = _gqa_call(p, pc)
        o_c = _diff_call(p, pc, lam_args, li, sw)
        weights = (wa_bf[l], wb_bf[l], wc_bf[l], wo_bf[l])
        if last:
            out = _outproj_call((o_a, o_b, o_c), g.reshape(b, length, -1), weights, xl, mod_l, final_norm_w, None)
        else:
            xl, h = _outproj_call((o_a, o_b, o_c), g.reshape(b, length, -1), weights, xl, mod_l, norm_w[l + 1],
                                  mod_all[l + 1, :b])
            xc, hc = _outproj_call(_ctx_attn_call(pc, lam_args, li, sw), gc.reshape(b, lc, -1), weights, xc, mod_c,
                                   norm_w[l + 1], mod_all[l + 1, b:b + 1])
    return out
```

```python
import functools

import numpy as np
import jax
import jax.numpy as jnp
from jax import lax
from jax.experimental import pallas as pl
from jax.experimental.pallas import tpu as pltpu

F32 = jnp.float32
BF16 = jnp.bfloat16

GRID_W = 64
HEAD_DIM = 128
H_A = 4
WIN_H = 8
WIN_W = 16
H_B = 8
KV_B = 2
H_C = 4
DC = HEAD_DIM // 2
ROPE_THETA = 10000.0
EPS = 1e-6
NEG = -1e30
LOG2E = 1.4426950408889634

A_W = H_A * HEAD_DIM
B_QW = H_B * HEAD_DIM
B_KW = KV_B * HEAD_DIM
C_W = H_C * HEAD_DIM

OFF_AQ, OFF_AK, OFF_AV, OFF_AZ = 0, A_W, 2 * A_W, 3 * A_W
OFF_BQ = 4 * A_W
OFF_BK = OFF_BQ + B_QW
OFF_BV = OFF_BK + B_KW
OFF_BZ = OFF_BV + B_KW
OFF_CQ = OFF_BZ + B_QW
OFF_CK = OFF_CQ + C_W
OFF_CV = OFF_CK + C_W
OFF_CZ = OFF_CV + C_W
P_W = OFF_CZ + C_W

TN = 512
TN_GATES = 1024
GATES_TM = 256
INPROJ_TM = 256

NBR_R = 4
NBR_KR = 12

VMEM_LIMIT = 56 * 1024 * 1024


def _params(n_axes):
    return pltpu.CompilerParams(dimension_semantics=("arbitrary",) * n_axes, vmem_limit_bytes=VMEM_LIMIT)


def _sigmoid(x):
    return 1.0 / (1.0 + jnp.exp(-x))


def _dot(a, b):
    return jnp.dot(a, b, preferred_element_type=F32)


def _dot_nt(a, b):
    return lax.dot_general(a, b, (((1,), (1,)), ((), ())), preferred_element_type=F32)


def _head_rmsnorm(y, w):
    return y * lax.rsqrt(jnp.mean(y * y, axis=-1, keepdims=True) + EPS) * w


def _ada_kernel(c_ref, w_ref, b_ref, o_ref):
    a = c_ref[...]
    a = a * _sigmoid(a)
    o_ref[0] = _dot(a.astype(BF16), w_ref[0].astype(BF16)) + b_ref[0]


def _ada_call(cc, w_ada, b_ada):
    depth, d, n3 = w_ada.shape
    rows = cc.shape[0]
    tn = 1536
    return pl.pallas_call(
        _ada_kernel,
        grid=(depth, n3 // tn),
        in_specs=[
            pl.BlockSpec((rows, d), lambda l, j: (0, 0)),
            pl.BlockSpec((1, d, tn), lambda l, j: (l, 0, j)),
            pl.BlockSpec((1, 1, tn), lambda l, j: (l, 0, j)),
        ],
        out_specs=pl.BlockSpec((1, rows, tn), lambda l, j: (l, 0, j)),
        out_shape=jax.ShapeDtypeStruct((depth, rows, n3), F32),
        compiler_params=_params(2),
        name="ada",
    )(cc, w_ada, b_ada.reshape(depth, 1, n3))


def _rms(x):
    return x * lax.rsqrt(jnp.mean(x * x, axis=-1, keepdims=True) + EPS)


def _modulated_norm(x, w, mod):
    return _rms(x) * w * (1.0 + mod[1:2, :]) + mod[0:1, :]


def _norm_kernel(x_ref, w_ref, mod_ref, o_ref):
    o_ref[0] = _modulated_norm(x_ref[0], w_ref[...], mod_ref[0]).astype(o_ref.dtype)


def _row_tile(n, cap):
    t = min(n, cap)
    assert n % t == 0
    return t


def _mod_spec(mod):
    d = mod.shape[-1]
    return pl.BlockSpec((1, 3, d), (lambda bi, i: (bi, 0, 0)) if mod.shape[0] != 1 else (lambda bi, i: (0, 0, 0)))


def _norm_call(x, w, mod):
    b, length, d = x.shape
    tm = _row_tile(length, 512)
    row_spec = pl.BlockSpec((1, tm, d), lambda bi, i: (bi, i, 0))
    return pl.pallas_call(
        _norm_kernel,
        grid=(b, length // tm),
        in_specs=[row_spec, pl.BlockSpec((1, d), lambda bi, i: (0, 0)), _mod_spec(mod)],
        out_specs=row_spec,
        out_shape=jax.ShapeDtypeStruct((b, length, d), BF16),
        compiler_params=_params(2),
        name="norm",
    )(x, w.reshape(1, d), mod)


def _inproj_kernel(h_ref, w_ref, qw_ref, kw_ref, cb_ref, sb_ref, cc_ref, sc_ref, p_ref):
    h = h_ref[...]
    scale_a = HEAD_DIM ** -0.5 * LOG2E
    scale_b = HEAD_DIM ** -0.5 * LOG2E
    scale_c = DC ** -0.5 * LOG2E
    heads = TN // HEAD_DIM

    def rope_b(y):
        return y * cb_ref[...] + pltpu.roll(y, HEAD_DIM // 2, 1) * sb_ref[...]

    def rope_c(y):
        lane = lax.broadcasted_iota(jnp.int32, y.shape, 1)
        first = (lane & (DC // 2)) == 0
        rot = jnp.where(first, pltpu.roll(y, HEAD_DIM - DC // 2, 1), pltpu.roll(y, DC // 2, 1))
        return y * cc_ref[...] + rot * sc_ref[...]

    def finish(col, y):
        if OFF_AQ <= col < OFF_AK:
            return y * scale_a
        if OFF_BQ <= col < OFF_BK:
            return rope_b(_head_rmsnorm(y, qw_ref[...])) * scale_b
        if OFF_BK <= col < OFF_BV:
            return rope_b(_head_rmsnorm(y, kw_ref[...]))
        if OFF_CQ <= col < OFF_CK:
            return rope_c(y) * scale_c
        if OFF_CK <= col < OFF_CV:
            return rope_c(y)
        return y

    for c0 in range(0, p_ref.shape[1], TN):
        acc = _dot(h, w_ref[:, c0:c0 + TN])
        for hh in range(heads):
            col = c0 + hh * HEAD_DIM
            y = finish(col, acc[:, hh * HEAD_DIM:(hh + 1) * HEAD_DIM])
            p_ref[:, col:col + HEAD_DIM] = y.astype(p_ref.dtype)


def _gates_kernel(h_ref, w_ref, bg_ref, g_ref):
    h = h_ref[...]
    for c0 in range(0, g_ref.shape[1], TN_GATES):
        cols = slice(c0, c0 + TN_GATES)
        g_ref[:, cols] = _sigmoid(_dot(h, w_ref[:, cols]) + bg_ref[:, cols]).astype(g_ref.dtype)


def _layer_weight_spec(w, layer, n_grid_axes):
    index = (lambda i: (layer, 0, 0)) if n_grid_axes == 1 else (lambda bi, i: (layer, 0, 0))
    return pl.BlockSpec((None,) + w.shape[1:], index, pipeline_mode=pl.Buffered(1))


def _inproj_call(h, w_p, w_g, layer, b_gate, qw, kw, tabs):
    m, d = h.shape
    gate_w = w_g.shape[2]
    assert w_p.shape[2] == P_W and P_W % TN == 0 and gate_w % TN_GATES == 0
    ltab = tabs[0].shape[0]
    tm = _row_tile(ltab, INPROJ_TM)
    assert m % tm == 0
    nt = ltab // tm
    tab_spec = pl.BlockSpec((tm, HEAD_DIM), lambda i: (i % nt, 0))
    vec_spec = pl.BlockSpec((1, HEAD_DIM), lambda i: (0, 0))
    p = pl.pallas_call(
        _inproj_kernel,
        grid=(m // tm,),
        in_specs=[pl.BlockSpec((tm, d), lambda i: (i, 0)), _layer_weight_spec(w_p, layer, 1),
                  vec_spec, vec_spec, tab_spec, tab_spec, tab_spec, tab_spec],
        out_specs=pl.BlockSpec((tm, P_W), lambda i: (i, 0)),
        out_shape=jax.ShapeDtypeStruct((m, P_W), BF16),
        compiler_params=_params(1),
        name="inproj",
    )(h, w_p, qw.reshape(1, HEAD_DIM), kw.reshape(1, HEAD_DIM), *tabs)
    tg = _row_tile(m, GATES_TM)
    g = pl.pallas_call(
        _gates_kernel,
        grid=(m // tg,),
        in_specs=[pl.BlockSpec((tg, d), lambda i: (i, 0)), _layer_weight_spec(w_g, layer, 1),
                  pl.BlockSpec((1, gate_w), lambda i: (0, 0))],
        out_specs=pl.BlockSpec((tg, gate_w), lambda i: (i, 0)),
        out_shape=jax.ShapeDtypeStruct((m, gate_w), BF16),
        compiler_params=_params(1),
        name="gates",
    )(h, w_g, b_gate.reshape(1, gate_w))
    return p, g


def _softmax2(s_a, s_b):
    m = jnp.maximum(jnp.max(s_a, axis=-1, keepdims=True), jnp.max(s_b, axis=-1, keepdims=True))
    e_a = jnp.exp2(s_a - m)
    e_b = jnp.exp2(s_b - m)
    return e_a, e_b, jnp.sum(e_a, axis=-1, keepdims=True) + jnp.sum(e_b, axis=-1, keepdims=True)


def _softmax1(s):
    m = jnp.max(s, axis=-1, keepdims=True)
    e = jnp.exp2(s - m)
    return e, jnp.sum(e, axis=-1, keepdims=True)


def _silu_gate(o, z):
    z = z.astype(F32)
    return o * (z * _sigmoid(z))


def _split_maps(q):
    lane = lax.broadcasted_iota(jnp.int32, q.shape, 1)
    zero = jnp.zeros_like(q)
    return jnp.where(lane < DC, q, zero), jnp.where(lane >= DC, q, zero)


def _lam(lq1_ref, lk1_ref, lq2_ref, lk2_ref, li_ref):
    return (jnp.exp(jnp.sum(lq1_ref[...] * lk1_ref[...], axis=-1, keepdims=True))
            - jnp.exp(jnp.sum(lq2_ref[...] * lk2_ref[...], axis=-1, keepdims=True)) + li_ref[:, 0:1])


UNITS = 4
ATTN_TQ = 512
SOFTMAX_ROWS = 128
MAX_CHAINS = 2


def _scores_stage(q, kl, kc, s_ref):
    length = kl.shape[0]
    s_ref[:, :length] = _dot_nt(q, kl)
    s_ref[:, length:] = _dot_nt(q, kc)


def _softmax_pv_stage(s_ref, p_ref, v_ext):
    tq, lk = s_ref.shape
    blocks = [slice(k * HEAD_DIM, (k + 1) * HEAD_DIM) for k in range(lk // HEAD_DIM)]
    for r0 in range(0, tq, SOFTMAX_ROWS):
        rows = slice(r0, r0 + SOFTMAX_ROWS)
        parts = [s_ref[rows, blk] for blk in blocks[:MAX_CHAINS]]
        for k, blk in enumerate(blocks[MAX_CHAINS:]):
            parts[k % MAX_CHAINS] = jnp.maximum(parts[k % MAX_CHAINS], s_ref[rows, blk])
        m = functools.reduce(jnp.maximum, parts)
        m = jnp.broadcast_to(jnp.max(m, axis=-1, keepdims=True), (SOFTMAX_ROWS, HEAD_DIM))
        for blk in blocks:
            p_ref[rows, blk] = jnp.exp2(s_ref[rows, blk] - m).astype(p_ref.dtype)
    o_ext = _dot(p_ref[...], v_ext)
    return o_ext[:, :HEAD_DIM], o_ext[:, HEAD_DIM:HEAD_DIM + 1]


def _fill_v_ext(v_ext_ref, vl, vc):
    length = vl.shape[0]
    v_ext_ref[:length, :HEAD_DIM] = vl
    v_ext_ref[length:, :HEAD_DIM] = vc
    v_ext_ref[:, HEAD_DIM:] = jnp.ones((v_ext_ref.shape[0], HEAD_DIM), v_ext_ref.dtype)


def _pipelined_units(first_step, prepare, unit_q, next_tile_q, unit_k, unit_v, finish, scratch):
    s_bufs, p_bufs = scratch[0:2], scratch[2:4]

    @pl.when(first_step)
    def _():
        prepare()
        _scores_stage(unit_q(0), *unit_k(0), s_bufs[0])

    for r in range(UNITS):
        cur, nxt = r % 2, (r + 1) % 2
        if r + 1 < UNITS:
            _scores_stage(unit_q(r + 1), *unit_k(r + 1), s_bufs[nxt])
        else:
            _scores_stage(next_tile_q(), *unit_k(0), s_bufs[nxt])
        o, l = _softmax_pv_stage(s_bufs[cur], p_bufs[cur], unit_v(r))
        finish(r, o, l)


def _pipeline_scratch(tq, lk, n_v):
    return ([pltpu.VMEM((tq, lk), F32)] * 2 + [pltpu.VMEM((tq, lk), BF16)] * 2
            + [pltpu.VMEM((n_v, lk, 2 * HEAD_DIM), BF16)])


def _gqa_kernel(q_ref, qn_ref, kl_ref, vl_ref, kc_ref, vc_ref, z_ref, o_ref, *scratch):
    def cols(r):
        return slice(r * HEAD_DIM, (r + 1) * HEAD_DIM)

    def finish(r, o, l):
        o_ref[0, :, cols(r)] = _silu_gate(o / l, z_ref[0, :, cols(r)]).astype(o_ref.dtype)

    v_ext_ref = scratch[-1]
    _pipelined_units(
        pl.program_id(2) == 0,
        lambda: _fill_v_ext(v_ext_ref.at[0], vl_ref[0], vc_ref[0]),
        lambda r: q_ref[0, :, cols(r)],
        lambda: qn_ref[0],
        lambda r: (kl_ref[0], kc_ref[0]),
        lambda r: v_ext_ref[0],
        finish, scratch)


def _gqa_call(p, pc):
    b, length, _ = p.shape
    lc = pc.shape[1]
    tq = _row_tile(length, ATTN_TQ)
    n_i = length // tq
    assert H_B // KV_B == UNITS
    gw = UNITS * HEAD_DIM
    kv = lambda n, off: pl.BlockSpec((1, n, HEAD_DIM), lambda bi, g, i: (bi, 0, off // HEAD_DIM + g))
    return pl.pallas_call(
        _gqa_kernel,
        grid=(b, KV_B, n_i),
        in_specs=[
            pl.BlockSpec((1, tq, gw), lambda bi, g, i: (bi, i, OFF_BQ // gw + g)),
            pl.BlockSpec((1, tq, HEAD_DIM),
                         lambda bi, g, i: (bi, jnp.minimum(i + 1, n_i - 1), OFF_BQ // HEAD_DIM + UNITS * g)),
            kv(length, OFF_BK), kv(length, OFF_BV), kv(lc, OFF_BK), kv(lc, OFF_BV),
            pl.BlockSpec((1, tq, gw), lambda bi, g, i: (bi, i, OFF_BZ // gw + g)),
        ],
        out_specs=pl.BlockSpec((1, tq, gw), lambda bi, g, i: (bi, i, g)),
        out_shape=jax.ShapeDtypeStruct((b, length, B_QW), BF16),
        scratch_shapes=_pipeline_scratch(tq, length + lc, 1),
        compiler_params=_params(3),
        name="gqa",
    )(p, p, p, p, pc, pc, p)


def _diff_core(q, k, v, lam, sw, li):
    q1, q2 = _split_maps(q)
    e1, l1 = _softmax1(_dot_nt(q1, k))
    e2, l2 = _softmax1(_dot_nt(q2, k))
    a = (e1 * (1.0 / l1) - e2 * (lam / l2)).astype(BF16)
    o = _head_rmsnorm(_dot(a, v), sw)
    return o * (1.0 - li)


def _diff_kernel(q_ref, qn_ref, kl_ref, vl_ref, kc_ref, vc_ref, z_ref, lq1_ref, lk1_ref, lq2_ref, lk2_ref, li_ref,
                 sw_ref, o_ref, *scratch):
    lam = _lam(lq1_ref, lk1_ref, lq2_ref, lk2_ref, li_ref)
    pending = {}

    def cols(hh):
        return slice(hh * HEAD_DIM, (hh + 1) * HEAD_DIM)

    def finish(r, o, l):
        hh = r // 2
        if r % 2 == 0:
            pending[hh] = o / l
            return
        o = pending.pop(hh) - o * (lam / l)
        o = _head_rmsnorm(o, sw_ref[...]) * (1.0 - li_ref[...])
        o_ref[0, :, cols(hh)] = _silu_gate(o, z_ref[0, :, cols(hh)]).astype(o_ref.dtype)

    v_ext_ref = scratch[-1]

    def prepare():
        for hh in range(UNITS // 2):
            _fill_v_ext(v_ext_ref.at[hh], vl_ref[0, :, cols(hh)], vc_ref[0, :, cols(hh)])

    _pipelined_units(
        pl.program_id(2) == 0,
        prepare,
        lambda r: _split_maps(q_ref[0, :, cols(r // 2)])[r % 2],
        lambda: _split_maps(qn_ref[0])[0],
        lambda r: (kl_ref[0, :, cols(r // 2)], kc_ref[0, :, cols(r // 2)]),
        lambda r: v_ext_ref[r // 2],
        finish, scratch)


def _diff_call(p, pc, lam_args, li, sw):
    b, length, _ = p.shape
    lc = pc.shape[1]
    tq = _row_tile(length, ATTN_TQ)
    n_i = length // tq
    heads = UNITS // 2
    assert H_C % heads == 0
    pw = heads * HEAD_DIM
    small = lambda w: pl.BlockSpec((1, w), lambda bi, g, i: (0, 0))
    kv = lambda n, off: pl.BlockSpec((1, n, pw), lambda bi, g, i: (bi, 0, off // pw + g))
    return pl.pallas_call(
        _diff_kernel,
        grid=(b, H_C // heads, n_i),
        in_specs=[
            pl.BlockSpec((1, tq, pw), lambda bi, g, i: (bi, i, OFF_CQ // pw + g)),
            pl.BlockSpec((1, tq, HEAD_DIM),
                         lambda bi, g, i: (bi, jnp.minimum(i + 1, n_i - 1), OFF_CQ // HEAD_DIM + heads * g)),
            kv(length, OFF_CK), kv(length, OFF_CV), kv(lc, OFF_CK), kv(lc, OFF_CV),
            pl.BlockSpec((1, tq, pw), lambda bi, g, i: (bi, i, OFF_CZ // pw + g)),
            small(DC), small(DC), small(DC), small(DC), small(HEAD_DIM), small(HEAD_DIM),
        ],
        out_specs=pl.BlockSpec((1, tq, pw), lambda bi, g, i: (bi, i, g)),
        out_shape=jax.ShapeDtypeStruct((b, length, C_W), BF16),
        scratch_shapes=_pipeline_scratch(tq, length + lc, heads),
        compiler_params=_params(3),
        name="diff",
    )(p, p, p, p, pc, pc, p, *lam_args, li, sw)


def _nbr_key_row0(t, rows):
    return jnp.clip(t * NBR_R - WIN_H // 2, 0, rows - NBR_KR)


def _nbr_kernel(q_ref, kl_ref, vl_ref, kc_ref, vc_ref, z_ref, bias_ref, o_ref, *, rows):
    t = pl.program_id(1)
    span = NBR_KR * GRID_W
    k_off = pl.multiple_of(_nbr_key_row0(t, rows) * GRID_W, NBR_R * GRID_W)
    for hh in range(H_A):
        cols = slice(hh * HEAD_DIM, (hh + 1) * HEAD_DIM)
        q = q_ref[0, :, cols]
        s_win = _dot_nt(q, kl_ref[0, pl.ds(k_off, span), cols]) + bias_ref[hh, 0]
        s_ctx = _dot_nt(q, kc_ref[0, :, cols])
        e_w, e_c, l = _softmax2(s_win, s_ctx)
        o = _dot(e_w.astype(BF16), vl_ref[0, pl.ds(k_off, span), cols]) + _dot(e_c.astype(BF16), vc_ref[0, :, cols])
        o_ref[0, :, cols] = _silu_gate(o / l, z_ref[0, :, cols]).astype(o_ref.dtype)


def _nbr_bias_selectors(rows):
    nt = rows // NBR_R
    kh = min(WIN_H, rows)
    n_dr, n_dc = 2 * WIN_H - 1, 2 * WIN_W - 1
    qc = np.arange(GRID_W)[:, None]
    kc = np.arange(GRID_W)[None, :]
    dc = np.clip(kc - qc, -(WIN_W - 1), WIN_W - 1) + (WIN_W - 1)
    cs = np.clip(qc - WIN_W // 2, 0, GRID_W - WIN_W)
    in_win = (kc >= cs) & (kc < cs + WIN_W)
    sel_dc = (dc[None] == np.arange(n_dc)[:, None, None]).astype(np.float32)
    i = np.arange(NBR_R)[:, None]
    jr = np.arange(NBR_KR)[None, :]
    sel_dr, in_band = [], []
    for t in (0, 1, nt - 1):
        r0 = NBR_R * t
        k0 = int(np.clip(r0 - WIN_H // 2, 0, rows - NBR_KR))
        rq = r0 + i
        rk = k0 + jr
        rs = np.clip(rq - kh // 2, 0, rows - kh)
        band = (rk >= rs) & (rk < rs + kh)
        dr = np.clip(rk - rq + (WIN_H - 1), 0, n_dr - 1)
        sel_dr.append((dr[..., None] == np.arange(n_dr)) & band[..., None])
        in_band.append(band)
    sel_dr = np.stack(sel_dr).astype(np.float32)
    valid = np.stack(in_band)[:, :, None, :, None] & in_win[None, None, :, None, :]
    return sel_dc, sel_dr, valid


def _nbr_bias_tables(rpb, rows):
    sel_dc, sel_dr, valid = _nbr_bias_selectors(rows)
    by_col = jnp.einsum("lhrd,dqk->lhrqk", rpb, sel_dc, precision=lax.Precision.HIGHEST)
    bias = jnp.einsum("tijr,lhrqk->lhtiqjk", sel_dr, by_col, precision=lax.Precision.HIGHEST)
    bias = jnp.where(valid[None, None], bias * LOG2E, NEG)
    return bias.reshape(rpb.shape[0], H_A, 3, NBR_R * GRID_W, NBR_KR * GRID_W)


def _nbr_call(p, pc, bias, layer):
    b, length, _ = p.shape
    lc = pc.shape[1]
    rows = length // GRID_W
    assert rows % NBR_R == 0 and rows >= NBR_KR
    tq = NBR_R * GRID_W
    nt = rows // NBR_R
    blk = lambda n, col: pl.BlockSpec((1, n, A_W), lambda bi, t: (bi, 0, col))
    tile = lambda col: pl.BlockSpec((1, tq, A_W), lambda bi, t: (bi, t, col))
    bias_type = lambda bi, t: (layer, 0, jnp.where(t == 0, 0, jnp.where(t == nt - 1, 2, 1)), 0, 0)
    return pl.pallas_call(
        functools.partial(_nbr_kernel, rows=rows),
        grid=(b, nt),
        in_specs=[
            tile(OFF_AQ // A_W), blk(length, OFF_AK // A_W), blk(length, OFF_AV // A_W),
            blk(lc, OFF_AK // A_W), blk(lc, OFF_AV // A_W), tile(OFF_AZ // A_W),
            pl.BlockSpec((None, H_A, 1, tq, NBR_KR * GRID_W), bias_type),
        ],
        out_specs=pl.BlockSpec((1, tq, A_W), lambda bi, t: (bi, t, 0)),
        out_shape=jax.ShapeDtypeStruct((b, length, A_W), BF16),
        compiler_params=_params(2),
        name="nbr",
    )(p, p, p, pc, pc, p, bias)


def _ctx_attn_kernel(p_ref, lq1_ref, lk1_ref, lq2_ref, lk2_ref, li_ref, sw_ref, oa_ref, ob_ref, oc_ref):
    def col(off, hh):
        return p_ref[0, :, off + hh * HEAD_DIM: off + (hh + 1) * HEAD_DIM]

    def plain(q, k, v):
        e, l = _softmax1(_dot_nt(q, k))
        return _dot(e.astype(BF16), v) / l

    for hh in range(H_A):
        o = plain(col(OFF_AQ, hh), col(OFF_AK, hh), col(OFF_AV, hh))
        oa_ref[0, :, hh * HEAD_DIM:(hh + 1) * HEAD_DIM] = _silu_gate(o, col(OFF_AZ, hh)).astype(oa_ref.dtype)
    rep = H_B // KV_B
    for hh in range(H_B):
        o = plain(col(OFF_BQ, hh), col(OFF_BK, hh // rep), col(OFF_BV, hh // rep))
        ob_ref[0, :, hh * HEAD_DIM:(hh + 1) * HEAD_DIM] = _silu_gate(o, col(OFF_BZ, hh)).astype(ob_ref.dtype)
    lam = _lam(lq1_ref, lk1_ref, lq2_ref, lk2_ref, li_ref)
    for hh in range(H_C):
        o = _diff_core(col(OFF_CQ, hh), col(OFF_CK, hh), col(OFF_CV, hh), lam, sw_ref[...], li_ref[...])
        oc_ref[0, :, hh * HEAD_DIM:(hh + 1) * HEAD_DIM] = _silu_gate(o, col(OFF_CZ, hh)).astype(oc_ref.dtype)


def _ctx_attn_call(pc, lam_args, li, sw):
    b, lc, _ = pc.shape
    small = lambda w: pl.BlockSpec((1, w), lambda bi: (0, 0))
    out = lambda w: pl.BlockSpec((1, lc, w), lambda bi: (bi, 0, 0))
    return pl.pallas_call(
        _ctx_attn_kernel,
        grid=(b,),
        in_specs=[pl.BlockSpec((1, lc, P_W), lambda bi: (bi, 0, 0)),
                  small(DC), small(DC), small(DC), small(DC), small(HEAD_DIM), small(HEAD_DIM)],
        out_specs=[out(A_W), out(B_QW), out(C_W)],
        out_shape=[jax.ShapeDtypeStruct((b, lc, A_W), BF16), jax.ShapeDtypeStruct((b, lc, B_QW), BF16),
                   jax.ShapeDtypeStruct((b, lc, C_W), BF16)],
        compiler_params=_params(1),
        name="ctx_attn",
    )(pc, *lam_args, li, sw)


def _residual(aa_ref, ab_ref, ac_ref, g_ref, wa_ref, wb_ref, wc_ref, wo_ref, x_ref, mod_ref):
    d = wo_ref.shape[0]
    y = None
    for k, (a_ref, w_ref) in enumerate(((aa_ref, wa_ref), (ab_ref, wb_ref), (ac_ref, wc_ref))):
        t = g_ref[0, :, k * d:(k + 1) * d].astype(F32) * _dot(a_ref[0], w_ref[...])
        y = t if y is None else y + t
    gate = mod_ref[0, 2:3, :]
    return x_ref[0] + gate * _dot(y.astype(BF16), wo_ref[...])


def _outproj_next_kernel(*refs):
    nw_ref, modn_ref, xo_ref, ho_ref = refs[-4:]
    x = _residual(*refs[:-4])
    xo_ref[0] = x
    ho_ref[0] = _modulated_norm(x, nw_ref[...], modn_ref[0]).astype(ho_ref.dtype)


def _outproj_final_kernel(*refs):
    nw_ref, o_ref = refs[-2:]
    o_ref[0] = _rms(_residual(*refs[:-2])) * nw_ref[...]


def _outproj_call(branches, g, weights, layer, x, mod, nw, mod_next):
    b, length, d = x.shape
    tm = _row_tile(length, 256)
    rows = lambda w: pl.BlockSpec((1, tm, w), lambda bi, i: (bi, i, 0))
    row_spec = rows(d)
    in_specs = ([rows(a.shape[-1]) for a in branches] + [rows(g.shape[-1])]
                + [_layer_weight_spec(w, layer, 2) for w in weights]
                + [row_spec, _mod_spec(mod), pl.BlockSpec((1, d), lambda bi, i: (0, 0))])
    args = [*branches, g, *weights, x, mod, nw.reshape(1, d)]
    if mod_next is None:
        return pl.pallas_call(
            _outproj_final_kernel, grid=(b, length // tm), in_specs=in_specs, out_specs=row_spec,
            out_shape=jax.ShapeDtypeStruct((b, length, d), F32), compiler_params=_params(2), name="outproj_final",
        )(*args)
    return pl.pallas_call(
        _outproj_next_kernel, grid=(b, length // tm), in_specs=in_specs + [_mod_spec(mod_next)],
        out_specs=[row_spec, row_spec],
        out_shape=[jax.ShapeDtypeStruct((b, length, d), F32), jax.ShapeDtypeStruct((b, length, d), BF16)],
        compiler_params=_params(2), name="outproj",
    )(*args, mod_next)


def _rope_tables(length):
    pos = jnp.arange(length)
    row = (pos // GRID_W).astype(F32)[:, None]
    col = (pos % GRID_W).astype(F32)[:, None]

    def cos_sin(rot_dim):
        n = rot_dim // 4
        inv_freq = ROPE_THETA ** (-jnp.arange(n, dtype=F32) / n)
        ang = jnp.concatenate([row * inv_freq, col * inv_freq], axis=-1)
        return jnp.cos(ang), jnp.sin(ang)

    cb, sb = cos_sin(HEAD_DIM)
    cc, sc = cos_sin(DC)
    return (jnp.concatenate([cb, cb], axis=-1), jnp.concatenate([-sb, sb], axis=-1),
            jnp.concatenate([cc, cc, cc, cc], axis=-1), jnp.concatenate([-sc, sc, -sc, sc], axis=-1))


def _identity_tables(length):
    one = jnp.ones((length, HEAD_DIM), F32)
    zero = jnp.zeros((length, HEAD_DIM), F32)
    return one, zero, one, zero


def kernel(x, c, ctx, c_ctx, norm_w, w_ada, b_ada, w_in, b_gate, rpb, q_norm_w, k_norm_w, lam_q1, lam_k1, lam_q2,
           lam_k2, subln_w, w_bo_a, w_bo_b, w_bo_c, w_out, final_norm_w):
    b, length, d = x.shape
    lc = ctx.shape[1]
    depth = w_in.shape[0]
    rows = length // GRID_W

    n_cond = b + 1
    pad = (-n_cond) % 8
    cc = jnp.concatenate([c, c_ctx[None], jnp.zeros((pad, d), F32)], axis=0)
    mod_all = _ada_call(cc, w_ada, b_ada).reshape(depth, n_cond + pad, 3, d)

    tabs_lat = _rope_tables(length)
    tabs_ctx = _identity_tables(_row_tile(b * lc, INPROJ_TM))
    nbr_bias = _nbr_bias_tables(rpb, rows)

    w_p_bf = w_in[:, :, :P_W].astype(BF16)
    w_g_bf = w_in[:, :, P_W:].astype(BF16)
    wa_bf, wb_bf, wc_bf, wo_bf = (w.astype(BF16) for w in (w_bo_a, w_bo_b, w_bo_c, w_out))

    xl, xc = x, ctx
    h = _norm_call(xl, norm_w[0], mod_all[0, :b])
    hc = _norm_call(xc, norm_w[0], mod_all[0, b:b + 1])
    out = None
    for l in range(depth):
        last = l == depth - 1
        mod_l = mod_all[l, :b]
        mod_c = mod_all[l, b:b + 1]
        p, g = _inproj_call(h.reshape(b * length, d), w_p_bf, w_g_bf, l, b_gate[l], q_norm_w[l], k_norm_w[l],
                            tabs_lat)
        pc, gc = _inproj_call(hc.reshape(b * lc, d), w_p_bf, w_g_bf, l, b_gate[l], q_norm_w[l], k_norm_w[l],
                              tabs_ctx)
        p = p.reshape(b, length, P_W)
        pc = pc.reshape(b, lc, P_W)

        lam_init = 0.8 - 0.6 * float(np.exp(-0.3 * l))
        li = jnp.full((1, HEAD_DIM), lam_init, F32)
        lam_args = [v[l].reshape(1, DC) for v in (lam_q1, lam_k1, lam_q2, lam_k2)]
        sw = subln_w[l].reshape(1, HEAD_DIM)

        o_a = _nbr_call(p, pc, nbr_bias, l)
        o_b = _gqa_call(p, pc)
        o_c = _diff_call(p, pc, lam_args, li, sw)
        weights = (wa_bf, wb_bf, wc_bf, wo_bf)
        if last:
            out = _outproj_call((o_a, o_b, o_c), g.reshape(b, length, -1), weights, l, xl, mod_l, final_norm_w, None)
        else:
            xl, h = _outproj_call((o_a, o_b, o_c), g.reshape(b, length, -1), weights, l, xl, mod_l, norm_w[l + 1],
                                  mod_all[l + 1, :b])
            xc, hc = _outproj_call(_ctx_attn_call(pc, lam_args, li, sw), gc.reshape(b, lc, -1), weights, l, xc,
                                   mod_c, norm_w[l + 1], mod_all[l + 1, b:b + 1])
    return out
```

```python
import functools

import numpy as np
import jax
import jax.numpy as jnp
from jax import lax
from jax.experimental import pallas as pl
from jax.experimental.pallas import tpu as pltpu

F32 = jnp.float32
BF16 = jnp.bfloat16

GRID_W = 64
HEAD_DIM = 128
H_A = 4
WIN_H = 8
WIN_W = 16
H_B = 8
KV_B = 2
H_C = 4
DC = HEAD_DIM // 2
ROPE_THETA = 10000.0
EPS = 1e-6
NEG = -1e30
LOG2E = 1.4426950408889634

A_W = H_A * HEAD_DIM
B_QW = H_B * HEAD_DIM
B_KW = KV_B * HEAD_DIM
C_W = H_C * HEAD_DIM

OFF_AQ, OFF_AK, OFF_AV, OFF_AZ = 0, A_W, 2 * A_W, 3 * A_W
OFF_BQ = 4 * A_W
OFF_BK = OFF_BQ + B_QW
OFF_BV = OFF_BK + B_KW
OFF_BZ = OFF_BV + B_KW
OFF_CQ = OFF_BZ + B_QW
OFF_CK = OFF_CQ + C_W
OFF_CV = OFF_CK + C_W
OFF_CZ = OFF_CV + C_W
P_W = OFF_CZ + C_W

TN = 512
TN_GATES = 1024
GATES_TM = 256
INPROJ_TM = 256

NBR_R = 4
NBR_KR = 12

VMEM_LIMIT = 56 * 1024 * 1024


def _params(n_axes):
    return pltpu.CompilerParams(dimension_semantics=("arbitrary",) * n_axes, vmem_limit_bytes=VMEM_LIMIT)


def _sigmoid(x):
    return 0.5 * jnp.tanh(0.5 * x) + 0.5


def _dot(a, b):
    return jnp.dot(a, b, preferred_element_type=F32)


def _dot_nt(a, b):
    return lax.dot_general(a, b, (((1,), (1,)), ((), ())), preferred_element_type=F32)


def _head_rmsnorm(y, w):
    return y * lax.rsqrt(jnp.mean(y * y, axis=-1, keepdims=True) + EPS) * w


def _ada_kernel(c_ref, w_ref, b_ref, o_ref):
    a = c_ref[...]
    a = a * _sigmoid(a)
    o_ref[0] = _dot(a.astype(BF16), w_ref[0].astype(BF16)) + b_ref[0]


def _ada_call(cc, w_ada, b_ada):
    depth, d, n3 = w_ada.shape
    rows = cc.shape[0]
    tn = 1536
    return pl.pallas_call(
        _ada_kernel,
        grid=(depth, n3 // tn),
        in_specs=[
            pl.BlockSpec((rows, d), lambda l, j: (0, 0)),
            pl.BlockSpec((1, d, tn), lambda l, j: (l, 0, j)),
            pl.BlockSpec((1, 1, tn), lambda l, j: (l, 0, j)),
        ],
        out_specs=pl.BlockSpec((1, rows, tn), lambda l, j: (l, 0, j)),
        out_shape=jax.ShapeDtypeStruct((depth, rows, n3), F32),
        compiler_params=_params(2),
        name="ada",
    )(cc, w_ada, b_ada.reshape(depth, 1, n3))


def _rms(x):
    return x * lax.rsqrt(jnp.mean(x * x, axis=-1, keepdims=True) + EPS)


def _modulated_norm(x, w, mod):
    return _rms(x) * w * (1.0 + mod[1:2, :]) + mod[0:1, :]


def _norm_kernel(x_ref, w_ref, mod_ref, o_ref):
    o_ref[0] = _modulated_norm(x_ref[0], w_ref[...], mod_ref[0]).astype(o_ref.dtype)


def _row_tile(n, cap):
    t = min(n, cap)
    assert n % t == 0
    return t


def _mod_spec(mod):
    d = mod.shape[-1]
    return pl.BlockSpec((1, 3, d), (lambda bi, i: (bi, 0, 0)) if mod.shape[0] != 1 else (lambda bi, i: (0, 0, 0)))


def _norm_call(x, w, mod):
    b, length, d = x.shape
    tm = _row_tile(length, 512)
    row_spec = pl.BlockSpec((1, tm, d), lambda bi, i: (bi, i, 0))
    return pl.pallas_call(
        _norm_kernel,
        grid=(b, length // tm),
        in_specs=[row_spec, pl.BlockSpec((1, d), lambda bi, i: (0, 0)), _mod_spec(mod)],
        out_specs=row_spec,
        out_shape=jax.ShapeDtypeStruct((b, length, d), BF16),
        compiler_params=_params(2),
        name="norm",
    )(x, w.reshape(1, d), mod)


def _inproj_kernel(h_ref, w_ref, qw_ref, kw_ref, cb_ref, sb_ref, cc_ref, sc_ref, p_ref):
    h = h_ref[...]
    scale_a = HEAD_DIM ** -0.5 * LOG2E
    scale_b = HEAD_DIM ** -0.5 * LOG2E
    scale_c = DC ** -0.5 * LOG2E
    heads = TN // HEAD_DIM

    def rope_b(y):
        return y * cb_ref[...] + pltpu.roll(y, HEAD_DIM // 2, 1) * sb_ref[...]

    def rope_c(y):
        lane = lax.broadcasted_iota(jnp.int32, y.shape, 1)
        first = (lane & (DC // 2)) == 0
        rot = jnp.where(first, pltpu.roll(y, HEAD_DIM - DC // 2, 1), pltpu.roll(y, DC // 2, 1))
        return y * cc_ref[...] + rot * sc_ref[...]

    def finish(col, y):
        if OFF_AQ <= col < OFF_AK:
            return y * scale_a
        if OFF_BQ <= col < OFF_BK:
            return rope_b(_head_rmsnorm(y, qw_ref[...])) * scale_b
        if OFF_BK <= col < OFF_BV:
            return rope_b(_head_rmsnorm(y, kw_ref[...]))
        if OFF_CQ <= col < OFF_CK:
            return rope_c(y) * scale_c
        if OFF_CK <= col < OFF_CV:
            return rope_c(y)
        return y

    for c0 in range(0, p_ref.shape[1], TN):
        acc = _dot(h, w_ref[:, c0:c0 + TN])
        for hh in range(heads):
            col = c0 + hh * HEAD_DIM
            y = finish(col, acc[:, hh * HEAD_DIM:(hh + 1) * HEAD_DIM])
            p_ref[:, col:col + HEAD_DIM] = y.astype(p_ref.dtype)


def _gates_kernel(h_ref, w_ref, bg_ref, g_ref):
    h = h_ref[...]
    for c0 in range(0, g_ref.shape[1], TN_GATES):
        cols = slice(c0, c0 + TN_GATES)
        g_ref[:, cols] = _sigmoid(_dot(h, w_ref[:, cols]) + bg_ref[:, cols]).astype(g_ref.dtype)


def _layer_weight_spec(w, layer, n_grid_axes):
    index = (lambda i: (layer, 0, 0)) if n_grid_axes == 1 else (lambda bi, i: (layer, 0, 0))
    return pl.BlockSpec((None,) + w.shape[1:], index, pipeline_mode=pl.Buffered(1))


def _inproj_call(h, w_p, w_g, layer, b_gate, qw, kw, tabs):
    m, d = h.shape
    gate_w = w_g.shape[2]
    assert w_p.shape[2] == P_W and P_W % TN == 0 and gate_w % TN_GATES == 0
    ltab = tabs[0].shape[0]
    tm = _row_tile(ltab, INPROJ_TM)
    assert m % tm == 0
    nt = ltab // tm
    tab_spec = pl.BlockSpec((tm, HEAD_DIM), lambda i: (i % nt, 0))
    vec_spec = pl.BlockSpec((1, HEAD_DIM), lambda i: (0, 0))
    p = pl.pallas_call(
        _inproj_kernel,
        grid=(m // tm,),
        in_specs=[pl.BlockSpec((tm, d), lambda i: (i, 0)), _layer_weight_spec(w_p, layer, 1),
                  vec_spec, vec_spec, tab_spec, tab_spec, tab_spec, tab_spec],
        out_specs=pl.BlockSpec((tm, P_W), lambda i: (i, 0)),
        out_shape=jax.ShapeDtypeStruct((m, P_W), BF16),
        compiler_params=_params(1),
        name="inproj",
    )(h, w_p, qw.reshape(1, HEAD_DIM), kw.reshape(1, HEAD_DIM), *tabs)
    tg = _row_tile(m, GATES_TM)
    g = pl.pallas_call(
        _gates_kernel,
        grid=(m // tg,),
        in_specs=[pl.BlockSpec((tg, d), lambda i: (i, 0)), _layer_weight_spec(w_g, layer, 1),
                  pl.BlockSpec((1, gate_w), lambda i: (0, 0))],
        out_specs=pl.BlockSpec((tg, gate_w), lambda i: (i, 0)),
        out_shape=jax.ShapeDtypeStruct((m, gate_w), BF16),
        compiler_params=_params(1),
        name="gates",
    )(h, w_g, b_gate.reshape(1, gate_w))
    return p, g


def _softmax2(s_a, s_b):
    m = jnp.maximum(jnp.max(s_a, axis=-1, keepdims=True), jnp.max(s_b, axis=-1, keepdims=True))
    e_a = jnp.exp2(s_a - m)
    e_b = jnp.exp2(s_b - m)
    return e_a, e_b, jnp.sum(e_a, axis=-1, keepdims=True) + jnp.sum(e_b, axis=-1, keepdims=True)


def _softmax1(s):
    m = jnp.max(s, axis=-1, keepdims=True)
    e = jnp.exp2(s - m)
    return e, jnp.sum(e, axis=-1, keepdims=True)


def _silu_gate(o, z):
    z = z.astype(F32)
    return o * (z * _sigmoid(z))


def _split_maps(q):
    lane = lax.broadcasted_iota(jnp.int32, q.shape, 1)
    zero = jnp.zeros_like(q)
    return jnp.where(lane < DC, q, zero), jnp.where(lane >= DC, q, zero)


def _lam(lq1_ref, lk1_ref, lq2_ref, lk2_ref, li_ref):
    return (jnp.exp(jnp.sum(lq1_ref[...] * lk1_ref[...], axis=-1, keepdims=True))
            - jnp.exp(jnp.sum(lq2_ref[...] * lk2_ref[...], axis=-1, keepdims=True)) + li_ref[:, 0:1])


UNITS = 4
ATTN_TQ = 512
SOFTMAX_ROWS = 128
MAX_CHAINS = 2


def _scores_stage(q, kl, kc, s_ref):
    length = kl.shape[0]
    s_ref[:, :length] = _dot_nt(q, kl)
    s_ref[:, length:] = _dot_nt(q, kc)


def _softmax_pv_stage(s_ref, p_ref, v_ext):
    tq, lk = s_ref.shape
    blocks = [slice(k * HEAD_DIM, (k + 1) * HEAD_DIM) for k in range(lk // HEAD_DIM)]
    for r0 in range(0, tq, SOFTMAX_ROWS):
        rows = slice(r0, r0 + SOFTMAX_ROWS)
        parts = [s_ref[rows, blk] for blk in blocks[:MAX_CHAINS]]
        for k, blk in enumerate(blocks[MAX_CHAINS:]):
            parts[k % MAX_CHAINS] = jnp.maximum(parts[k % MAX_CHAINS], s_ref[rows, blk])
        m = functools.reduce(jnp.maximum, parts)
        m = jnp.broadcast_to(jnp.max(m, axis=-1, keepdims=True), (SOFTMAX_ROWS, HEAD_DIM))
        for blk in blocks:
            p_ref[rows, blk] = jnp.exp2(s_ref[rows, blk] - m).astype(p_ref.dtype)
    o_ext = _dot(p_ref[...], v_ext)
    return o_ext[:, :HEAD_DIM], o_ext[:, HEAD_DIM:HEAD_DIM + 1]


def _fill_v_ext(v_ext_ref, vl, vc):
    length = vl.shape[0]
    v_ext_ref[:length, :HEAD_DIM] = vl
    v_ext_ref[length:, :HEAD_DIM] = vc
    v_ext_ref[:, HEAD_DIM:] = jnp.ones((v_ext_ref.shape[0], HEAD_DIM), v_ext_ref.dtype)


def _pipelined_units(first_step, prepare, unit_q, next_tile_q, unit_k, unit_v, finish, scratch):
    s_bufs, p_bufs = scratch[0:2], scratch[2:4]

    @pl.when(first_step)
    def _():
        prepare()
        _scores_stage(unit_q(0), *unit_k(0), s_bufs[0])

    for r in range(UNITS):
        cur, nxt = r % 2, (r + 1) % 2
        if r + 1 < UNITS:
            _scores_stage(unit_q(r + 1), *unit_k(r + 1), s_bufs[nxt])
        else:
            _scores_stage(next_tile_q(), *unit_k(0), s_bufs[nxt])
        o, l = _softmax_pv_stage(s_bufs[cur], p_bufs[cur], unit_v(r))
        finish(r, o, l)


def _pipeline_scratch(tq, lk, n_v):
    return ([pltpu.VMEM((tq, lk), F32)] * 2 + [pltpu.VMEM((tq, lk), BF16)] * 2
            + [pltpu.VMEM((n_v, lk, 2 * HEAD_DIM), BF16)])


def _gqa_kernel(q_ref, qn_ref, kl_ref, vl_ref, kc_ref, vc_ref, z_ref, o_ref, *scratch):
    def cols(r):
        return slice(r * HEAD_DIM, (r + 1) * HEAD_DIM)

    def finish(r, o, l):
        o_ref[0, :, cols(r)] = _silu_gate(o / l, z_ref[0, :, cols(r)]).astype(o_ref.dtype)

    v_ext_ref = scratch[-1]
    _pipelined_units(
        pl.program_id(2) == 0,
        lambda: _fill_v_ext(v_ext_ref.at[0], vl_ref[0], vc_ref[0]),
        lambda r: q_ref[0, :, cols(r)],
        lambda: qn_ref[0],
        lambda r: (kl_ref[0], kc_ref[0]),
        lambda r: v_ext_ref[0],
        finish, scratch)


def _gqa_call(p, pc):
    b, length, _ = p.shape
    lc = pc.shape[1]
    tq = _row_tile(length, ATTN_TQ)
    n_i = length // tq
    assert H_B // KV_B == UNITS
    gw = UNITS * HEAD_DIM
    kv = lambda n, off: pl.BlockSpec((1, n, HEAD_DIM), lambda bi, g, i: (bi, 0, off // HEAD_DIM + g))
    return pl.pallas_call(
        _gqa_kernel,
        grid=(b, KV_B, n_i),
        in_specs=[
            pl.BlockSpec((1, tq, gw), lambda bi, g, i: (bi, i, OFF_BQ // gw + g)),
            pl.BlockSpec((1, tq, HEAD_DIM),
                         lambda bi, g, i: (bi, jnp.minimum(i + 1, n_i - 1), OFF_BQ // HEAD_DIM + UNITS * g)),
            kv(length, OFF_BK), kv(length, OFF_BV), kv(lc, OFF_BK), kv(lc, OFF_BV),
            pl.BlockSpec((1, tq, gw), lambda bi, g, i: (bi, i, OFF_BZ // gw + g)),
        ],
        out_specs=pl.BlockSpec((1, tq, gw), lambda bi, g, i: (bi, i, g)),
        out_shape=jax.ShapeDtypeStruct((b, length, B_QW), BF16),
        scratch_shapes=_pipeline_scratch(tq, length + lc, 1),
        compiler_params=_params(3),
        name="gqa",
    )(p, p, p, p, pc, pc, p)


def _diff_core(q, k, v, lam, sw, li):
    q1, q2 = _split_maps(q)
    e1, l1 = _softmax1(_dot_nt(q1, k))
    e2, l2 = _softmax1(_dot_nt(q2, k))
    a = (e1 * (1.0 / l1) - e2 * (lam / l2)).astype(BF16)
    o = _head_rmsnorm(_dot(a, v), sw)
    return o * (1.0 - li)


def _diff_kernel(q_ref, qn_ref, kl_ref, vl_ref, kc_ref, vc_ref, z_ref, lq1_ref, lk1_ref, lq2_ref, lk2_ref, li_ref,
                 sw_ref, o_ref, *scratch):
    lam = _lam(lq1_ref, lk1_ref, lq2_ref, lk2_ref, li_ref)
    pending = {}

    def cols(hh):
        return slice(hh * HEAD_DIM, (hh + 1) * HEAD_DIM)

    def finish(r, o, l):
        hh = r // 2
        if r % 2 == 0:
            pending[hh] = o / l
            return
        o = pending.pop(hh) - o * (lam / l)
        o = _head_rmsnorm(o, sw_ref[...]) * (1.0 - li_ref[...])
        o_ref[0, :, cols(hh)] = _silu_gate(o, z_ref[0, :, cols(hh)]).astype(o_ref.dtype)

    v_ext_ref = scratch[-1]

    def prepare():
        for hh in range(UNITS // 2):
            _fill_v_ext(v_ext_ref.at[hh], vl_ref[0, :, cols(hh)], vc_ref[0, :, cols(hh)])

    _pipelined_units(
        pl.program_id(2) == 0,
        prepare,
        lambda r: _split_maps(q_ref[0, :, cols(r // 2)])[r % 2],
        lambda: _split_maps(qn_ref[0])[0],
        lambda r: (kl_ref[0, :, cols(r // 2)], kc_ref[0, :, cols(r // 2)]),
        lambda r: v_ext_ref[r // 2],
        finish, scratch)


def _diff_call(p, pc, lam_args, li, sw):
    b, length, _ = p.shape
    lc = pc.shape[1]
    tq = _row_tile(length, ATTN_TQ)
    n_i = length // tq
    heads = UNITS // 2
    assert H_C % heads == 0
    pw = heads * HEAD_DIM
    small = lambda w: pl.BlockSpec((1, w), lambda bi, g, i: (0, 0))
    kv = lambda n, off: pl.BlockSpec((1, n, pw), lambda bi, g, i: (bi, 0, off // pw + g))
    return pl.pallas_call(
        _diff_kernel,
        grid=(b, H_C // heads, n_i),
        in_specs=[
            pl.BlockSpec((1, tq, pw), lambda bi, g, i: (bi, i, OFF_CQ // pw + g)),
            pl.BlockSpec((1, tq, HEAD_DIM),
                         lambda bi, g, i: (bi, jnp.minimum(i + 1, n_i - 1), OFF_CQ // HEAD_DIM + heads * g)),
            kv(length, OFF_CK), kv(length, OFF_CV), kv(lc, OFF_CK), kv(lc, OFF_CV),
            pl.BlockSpec((1, tq, pw), lambda bi, g, i: (bi, i, OFF_CZ // pw + g)),
            small(DC), small(DC), small(DC), small(DC), small(HEAD_DIM), small(HEAD_DIM),
        ],
        out_specs=pl.BlockSpec((1, tq, pw), lambda bi, g, i: (bi, i, g)),
        out_shape=jax.ShapeDtypeStruct((b, length, C_W), BF16),
        scratch_shapes=_pipeline_scratch(tq, length + lc, heads),
        compiler_params=_params(3),
        name="diff",
    )(p, p, p, p, pc, pc, p, *lam_args, li, sw)


def _nbr_key_row0(t, rows):
    return jnp.clip(t * NBR_R - WIN_H // 2, 0, rows - NBR_KR)


def _nbr_kernel(q_ref, kl_ref, vl_ref, kc_ref, vc_ref, z_ref, bias_ref, o_ref, *, rows):
    t = pl.program_id(1)
    span = NBR_KR * GRID_W
    k_off = pl.multiple_of(_nbr_key_row0(t, rows) * GRID_W, NBR_R * GRID_W)
    for hh in range(H_A):
        cols = slice(hh * HEAD_DIM, (hh + 1) * HEAD_DIM)
        q = q_ref[0, :, cols]
        s_win = _dot_nt(q, kl_ref[0, pl.ds(k_off, span), cols]) + bias_ref[hh, 0]
        s_ctx = _dot_nt(q, kc_ref[0, :, cols])
        e_w, e_c, l = _softmax2(s_win, s_ctx)
        o = _dot(e_w.astype(BF16), vl_ref[0, pl.ds(k_off, span), cols]) + _dot(e_c.astype(BF16), vc_ref[0, :, cols])
        o_ref[0, :, cols] = _silu_gate(o / l, z_ref[0, :, cols]).astype(o_ref.dtype)


def _nbr_bias_selectors(rows):
    nt = rows // NBR_R
    kh = min(WIN_H, rows)
    n_dr, n_dc = 2 * WIN_H - 1, 2 * WIN_W - 1
    qc = np.arange(GRID_W)[:, None]
    kc = np.arange(GRID_W)[None, :]
    dc = np.clip(kc - qc, -(WIN_W - 1), WIN_W - 1) + (WIN_W - 1)
    cs = np.clip(qc - WIN_W // 2, 0, GRID_W - WIN_W)
    in_win = (kc >= cs) & (kc < cs + WIN_W)
    sel_dc = (dc[None] == np.arange(n_dc)[:, None, None]).astype(np.float32)
    i = np.arange(NBR_R)[:, None]
    jr = np.arange(NBR_KR)[None, :]
    sel_dr, in_band = [], []
    for t in (0, 1, nt - 1):
        r0 = NBR_R * t
        k0 = int(np.clip(r0 - WIN_H // 2, 0, rows - NBR_KR))
        rq = r0 + i
        rk = k0 + jr
        rs = np.clip(rq - kh // 2, 0, rows - kh)
        band = (rk >= rs) & (rk < rs + kh)
        dr = np.clip(rk - rq + (WIN_H - 1), 0, n_dr - 1)
        sel_dr.append((dr[..., None] == np.arange(n_dr)) & band[..., None])
        in_band.append(band)
    sel_dr = np.stack(sel_dr).astype(np.float32)
    valid = np.stack(in_band)[:, :, None, :, None] & in_win[None, None, :, None, :]
    return sel_dc, sel_dr, valid


def _nbr_bias_tables(rpb, rows):
    sel_dc, sel_dr, valid = _nbr_bias_selectors(rows)
    by_col = jnp.einsum("lhrd,dqk->lhrqk", rpb, sel_dc, precision=lax.Precision.HIGHEST)
    bias = jnp.einsum("tijr,lhrqk->lhtiqjk", sel_dr, by_col, precision=lax.Precision.HIGHEST)
    bias = jnp.where(valid[None, None], bias * LOG2E, NEG)
    return bias.reshape(rpb.shape[0], H_A, 3, NBR_R * GRID_W, NBR_KR * GRID_W)


def _nbr_call(p, pc, bias, layer):
    b, length, _ = p.shape
    lc = pc.shape[1]
    rows = length // GRID_W
    assert rows % NBR_R == 0 and rows >= NBR_KR
    tq = NBR_R * GRID_W
    nt = rows // NBR_R
    blk = lambda n, col: pl.BlockSpec((1, n, A_W), lambda bi, t: (bi, 0, col))
    tile = lambda col: pl.BlockSpec((1, tq, A_W), lambda bi, t: (bi, t, col))
    bias_type = lambda bi, t: (layer, 0, jnp.where(t == 0, 0, jnp.where(t == nt - 1, 2, 1)), 0, 0)
    return pl.pallas_call(
        functools.partial(_nbr_kernel, rows=rows),
        grid=(b, nt),
        in_specs=[
            tile(OFF_AQ // A_W), blk(length, OFF_AK // A_W), blk(length, OFF_AV // A_W),
            blk(lc, OFF_AK // A_W), blk(lc, OFF_AV // A_W), tile(OFF_AZ // A_W),
            pl.BlockSpec((None, H_A, 1, tq, NBR_KR * GRID_W), bias_type),
        ],
        out_specs=pl.BlockSpec((1, tq, A_W), lambda bi, t: (bi, t, 0)),
        out_shape=jax.ShapeDtypeStruct((b, length, A_W), BF16),
        compiler_params=_params(2),
        name="nbr",
    )(p, p, p, pc, pc, p, bias)


def _ctx_attn_kernel(p_ref, lq1_ref, lk1_ref, lq2_ref, lk2_ref, li_ref, sw_ref, oa_ref, ob_ref, oc_ref):
    def col(off, hh):
        return p_ref[0, :, off + hh * HEAD_DIM: off + (hh + 1) * HEAD_DIM]

    def plain(q, k, v):
        e, l = _softmax1(_dot_nt(q, k))
        return _dot(e.astype(BF16), v) / l

    for hh in range(H_A):
        o = plain(col(OFF_AQ, hh), col(OFF_AK, hh), col(OFF_AV, hh))
        oa_ref[0, :, hh * HEAD_DIM:(hh + 1) * HEAD_DIM] = _silu_gate(o, col(OFF_AZ, hh)).astype(oa_ref.dtype)
    rep = H_B // KV_B
    for hh in range(H_B):
        o = plain(col(OFF_BQ, hh), col(OFF_BK, hh // rep), col(OFF_BV, hh // rep))
        ob_ref[0, :, hh * HEAD_DIM:(hh + 1) * HEAD_DIM] = _silu_gate(o, col(OFF_BZ, hh)).astype(ob_ref.dtype)
    lam = _lam(lq1_ref, lk1_ref, lq2_ref, lk2_ref, li_ref)
    for hh in range(H_C):
        o = _diff_core(col(OFF_CQ, hh), col(OFF_CK, hh), col(OFF_CV, hh), lam, sw_ref[...], li_ref[...])
        oc_ref[0, :, hh * HEAD_DIM:(hh + 1) * HEAD_DIM] = _silu_gate(o, col(OFF_CZ, hh)).astype(oc_ref.dtype)


def _ctx_attn_call(pc, lam_args, li, sw):
    b, lc, _ = pc.shape
    small = lambda w: pl.BlockSpec((1, w), lambda bi: (0, 0))
    out = lambda w: pl.BlockSpec((1, lc, w), lambda bi: (bi, 0, 0))
    return pl.pallas_call(
        _ctx_attn_kernel,
        grid=(b,),
        in_specs=[pl.BlockSpec((1, lc, P_W), lambda bi: (bi, 0, 0)),
                  small(DC), small(DC), small(DC), small(DC), small(HEAD_DIM), small(HEAD_DIM)],
        out_specs=[out(A_W), out(B_QW), out(C_W)],
        out_shape=[jax.ShapeDtypeStruct((b, lc, A_W), BF16), jax.ShapeDtypeStruct((b, lc, B_QW), BF16),
                   jax.ShapeDtypeStruct((b, lc, C_W), BF16)],
        compiler_params=_params(1),
        name="ctx_attn",
    )(pc, *lam_args, li, sw)


def _residual(aa_ref, ab_ref, ac_ref, g_ref, wa_ref, wb_ref, wc_ref, wo_ref, x_ref, mod_ref):
    d = wo_ref.shape[0]
    y = None
    for k, (a_ref, w_ref) in enumerate(((aa_ref, wa_ref), (ab_ref, wb_ref), (ac_ref, wc_ref))):
        t = g_ref[0, :, k * d:(k + 1) * d].astype(F32) * _dot(a_ref[0], w_ref[...])
        y = t if y is None else y + t
    gate = mod_ref[0, 2:3, :]
    return x_ref[0] + gate * _dot(y.astype(BF16), wo_ref[...])


def _outproj_next_kernel(*refs):
    nw_ref, modn_ref, xo_ref, ho_ref = refs[-4:]
    x = _residual(*refs[:-4])
    xo_ref[0] = x
    ho_ref[0] = _modulated_norm(x, nw_ref[...], modn_ref[0]).astype(ho_ref.dtype)


def _outproj_final_kernel(*refs):
    nw_ref, o_ref = refs[-2:]
    o_ref[0] = _rms(_residual(*refs[:-2])) * nw_ref[...]


def _outproj_call(branches, g, weights, layer, x, mod, nw, mod_next):
    b, length, d = x.shape
    tm = _row_tile(length, 256)
    rows = lambda w: pl.BlockSpec((1, tm, w), lambda bi, i: (bi, i, 0))
    row_spec = rows(d)
    in_specs = ([rows(a.shape[-1]) for a in branches] + [rows(g.shape[-1])]
                + [_layer_weight_spec(w, layer, 2) for w in weights]
                + [row_spec, _mod_spec(mod), pl.BlockSpec((1, d), lambda bi, i: (0, 0))])
    args = [*branches, g, *weights, x, mod, nw.reshape(1, d)]
    if mod_next is None:
        return pl.pallas_call(
            _outproj_final_kernel, grid=(b, length // tm), in_specs=in_specs, out_specs=row_spec,
            out_shape=jax.ShapeDtypeStruct((b, length, d), F32), compiler_params=_params(2), name="outproj_final",
        )(*args)
    return pl.pallas_call(
        _outproj_next_kernel, grid=(b, length // tm), in_specs=in_specs + [_mod_spec(mod_next)],
        out_specs=[row_spec, row_spec],
        out_shape=[jax.ShapeDtypeStruct((b, length, d), F32), jax.ShapeDtypeStruct((b, length, d), BF16)],
        compiler_params=_params(2), name="outproj",
    )(*args, mod_next)


def _rope_tables(length):
    pos = jnp.arange(length)
    row = (pos // GRID_W).astype(F32)[:, None]
    col = (pos % GRID_W).astype(F32)[:, None]

    def cos_sin(rot_dim):
        n = rot_dim // 4
        inv_freq = ROPE_THETA ** (-jnp.arange(n, dtype=F32) / n)
        ang = jnp.concatenate([row * inv_freq, col * inv_freq], axis=-1)
        return jnp.cos(ang), jnp.sin(ang)

    cb, sb = cos_sin(HEAD_DIM)
    cc, sc = cos_sin(DC)
    return (jnp.concatenate([cb, cb], axis=-1), jnp.concatenate([-sb, sb], axis=-1),
            jnp.concatenate([cc, cc, cc, cc], axis=-1), jnp.concatenate([-sc, sc, -sc, sc], axis=-1))


def _identity_tables(length):
    one = jnp.ones((length, HEAD_DIM), F32)
    zero = jnp.zeros((length, HEAD_DIM), F32)
    return one, zero, one, zero


def kernel(x, c, ctx, c_ctx, norm_w, w_ada, b_ada, w_in, b_gate, rpb, q_norm_w, k_norm_w, lam_q1, lam_k1, lam_q2,
           lam_k2, subln_w, w_bo_a, w_bo_b, w_bo_c, w_out, final_norm_w):
    b, length, d = x.shape
    lc = ctx.shape[1]
    depth = w_in.shape[0]
    rows = length // GRID_W

    n_cond = b + 1
    pad = (-n_cond) % 8
    cc = jnp.concatenate([c, c_ctx[None], jnp.zeros((pad, d), F32)], axis=0)
    mod_all = _ada_call(cc, w_ada, b_ada).reshape(depth, n_cond + pad, 3, d)

    tabs_lat = _rope_tables(length)
    tabs_ctx = _identity_tables(_row_tile(b * lc, INPROJ_TM))
    nbr_bias = _nbr_bias_tables(rpb, rows)

    w_p_bf = w_in[:, :, :P_W].astype(BF16)
    w_g_bf = w_in[:, :, P_W:].astype(BF16)
    wa_bf, wb_bf, wc_bf, wo_bf = (w.astype(BF16) for w in (w_bo_a, w_bo_b, w_bo_c, w_out))

    xl, xc = x, ctx
    h = _norm_call(xl, norm_w[0], mod_all[0, :b])
    hc = _norm_call(xc, norm_w[0], mod_all[0, b:b + 1])
    out = None
    for l in range(depth):
        last = l == depth - 1
        mod_l = mod_all[l, :b]
        mod_c = mod_all[l, b:b + 1]
        p, g = _inproj_call(h.reshape(b * length, d), w_p_bf, w_g_bf, l, b_gate[l], q_norm_w[l], k_norm_w[l],
                            tabs_lat)
        pc, gc = _inproj_call(hc.reshape(b * lc, d), w_p_bf, w_g_bf, l, b_gate[l], q_norm_w[l], k_norm_w[l],
                              tabs_ctx)
        p = p.reshape(b, length, P_W)
        pc = pc.reshape(b, lc, P_W)

        lam_init = 0.8 - 0.6 * float(np.exp(-0.3 * l))
        li = jnp.full((1, HEAD_DIM), lam_init, F32)
        lam_args = [v[l].reshape(1, DC) for v in (lam_q1, lam_k1, lam_q2, lam_k2)]
        sw = subln_w[l].reshape(1, HEAD_DIM)

        o_a = _nbr_call(p, pc, nbr_bias, l)
        o_b = _gqa_call(p, pc)
        o_c = _diff_call(p, pc, lam_args, li, sw)
        weights = (wa_bf, wb_bf, wc_bf, wo_bf)
        if last:
            out = _outproj_call((o_a, o_b, o_c), g.reshape(b, length, -1), weights, l, xl, mod_l, final_norm_w, None)
        else:
            xl, h = _outproj_call((o_a, o_b, o_c), g.reshape(b, length, -1), weights, l, xl, mod_l, norm_w[l + 1],
                                  mod_all[l + 1, :b])
            xc, hc = _outproj_call(_ctx_attn_call(pc, lam_args, li, sw), gc.reshape(b, lc, -1), weights, l, xc,
                                   mod_c, norm_w[l + 1], mod_all[l + 1, b:b + 1])
    return out
```

```python
import functools

import numpy as np
import jax
import jax.numpy as jnp
from jax import lax
from jax.experimental import pallas as pl
from jax.experimental.pallas import tpu as pltpu

F32 = jnp.float32
BF16 = jnp.bfloat16

GRID_W = 64
HEAD_DIM = 128
H_A = 4
WIN_H = 8
WIN_W = 16
H_B = 8
KV_B = 2
H_C = 4
DC = HEAD_DIM // 2
ROPE_THETA = 10000.0
EPS = 1e-6
NEG = -1e30
LOG2E = 1.4426950408889634

A_W = H_A * HEAD_DIM
B_QW = H_B * HEAD_DIM
B_KW = KV_B * HEAD_DIM
C_W = H_C * HEAD_DIM

OFF_AQ, OFF_AK, OFF_AV, OFF_AZ = 0, A_W, 2 * A_W, 3 * A_W
OFF_BQ = 4 * A_W
OFF_BK = OFF_BQ + B_QW
OFF_BV = OFF_BK + B_KW
OFF_BZ = OFF_BV + B_KW
OFF_CQ = OFF_BZ + B_QW
OFF_CK = OFF_CQ + C_W
OFF_CV = OFF_CK + C_W
OFF_CZ = OFF_CV + C_W
P_W = OFF_CZ + C_W

TN = 512
TN_GATES = 1024
GATES_TM = 256
INPROJ_TM = 512

NBR_R = 4
NBR_KR = 12

VMEM_LIMIT = 56 * 1024 * 1024


def _params(n_axes):
    return pltpu.CompilerParams(dimension_semantics=("arbitrary",) * n_axes, vmem_limit_bytes=VMEM_LIMIT)


def _sigmoid(x):
    return 0.5 * jnp.tanh(0.5 * x) + 0.5


def _dot(a, b):
    return jnp.dot(a, b, preferred_element_type=F32)


def _dot_nt(a, b):
    return lax.dot_general(a, b, (((1,), (1,)), ((), ())), preferred_element_type=F32)


def _head_rmsnorm(y, w):
    return y * lax.rsqrt(jnp.mean(y * y, axis=-1, keepdims=True) + EPS) * w


def _ada_kernel(c_ref, w_ref, b_ref, o_ref):
    a = c_ref[...]
    a = a * _sigmoid(a)
    o_ref[0] = _dot(a.astype(BF16), w_ref[0].astype(BF16)) + b_ref[0]


def _ada_call(cc, w_ada, b_ada):
    depth, d, n3 = w_ada.shape
    rows = cc.shape[0]
    tn = 1536
    return pl.pallas_call(
        _ada_kernel,
        grid=(depth, n3 // tn),
        in_specs=[
            pl.BlockSpec((rows, d), lambda l, j: (0, 0)),
            pl.BlockSpec((1, d, tn), lambda l, j: (l, 0, j)),
            pl.BlockSpec((1, 1, tn), lambda l, j: (l, 0, j)),
        ],
        out_specs=pl.BlockSpec((1, rows, tn), lambda l, j: (l, 0, j)),
        out_shape=jax.ShapeDtypeStruct((depth, rows, n3), F32),
        compiler_params=_params(2),
        name="ada",
    )(cc, w_ada, b_ada.reshape(depth, 1, n3))


def _rms(x):
    return x * lax.rsqrt(jnp.mean(x * x, axis=-1, keepdims=True) + EPS)


def _modulated_norm(x, w, mod):
    return _rms(x) * w * (1.0 + mod[1:2, :]) + mod[0:1, :]


def _norm_kernel(x_ref, w_ref, mod_ref, o_ref):
    o_ref[0] = _modulated_norm(x_ref[0], w_ref[...], mod_ref[0]).astype(o_ref.dtype)


def _row_tile(n, cap):
    t = min(n, cap)
    assert n % t == 0
    return t


def _mod_spec(mod):
    d = mod.shape[-1]
    return pl.BlockSpec((1, 3, d), (lambda bi, i: (bi, 0, 0)) if mod.shape[0] != 1 else (lambda bi, i: (0, 0, 0)))


def _norm_call(x, w, mod):
    b, length, d = x.shape
    tm = _row_tile(length, 512)
    row_spec = pl.BlockSpec((1, tm, d), lambda bi, i: (bi, i, 0))
    return pl.pallas_call(
        _norm_kernel,
        grid=(b, length // tm),
        in_specs=[row_spec, pl.BlockSpec((1, d), lambda bi, i: (0, 0)), _mod_spec(mod)],
        out_specs=row_spec,
        out_shape=jax.ShapeDtypeStruct((b, length, d), BF16),
        compiler_params=_params(2),
        name="norm",
    )(x, w.reshape(1, d), mod)


def _inproj_kernel(h_ref, w_ref, qw_ref, kw_ref, cb_ref, sb_ref, cc_ref, sc_ref, p_ref):
    h = h_ref[...]
    scale_a = HEAD_DIM ** -0.5 * LOG2E
    scale_b = HEAD_DIM ** -0.5 * LOG2E
    scale_c = DC ** -0.5 * LOG2E
    heads = TN // HEAD_DIM

    def rope_b(y):
        return y * cb_ref[...] + pltpu.roll(y, HEAD_DIM // 2, 1) * sb_ref[...]

    def rope_c(y):
        lane = lax.broadcasted_iota(jnp.int32, y.shape, 1)
        first = (lane & (DC // 2)) == 0
        rot = jnp.where(first, pltpu.roll(y, HEAD_DIM - DC // 2, 1), pltpu.roll(y, DC // 2, 1))
        return y * cc_ref[...] + rot * sc_ref[...]

    def finish(col, y):
        if OFF_AQ <= col < OFF_AK:
            return y * scale_a
        if OFF_BQ <= col < OFF_BK:
            return rope_b(_head_rmsnorm(y, qw_ref[...])) * scale_b
        if OFF_BK <= col < OFF_BV:
            return rope_b(_head_rmsnorm(y, kw_ref[...]))
        if OFF_CQ <= col < OFF_CK:
            return rope_c(y) * scale_c
        if OFF_CK <= col < OFF_CV:
            return rope_c(y)
        return y

    for c0 in range(0, p_ref.shape[1], TN):
        acc = _dot(h, w_ref[:, c0:c0 + TN])
        for hh in range(heads):
            col = c0 + hh * HEAD_DIM
            y = finish(col, acc[:, hh * HEAD_DIM:(hh + 1) * HEAD_DIM])
            p_ref[:, col:col + HEAD_DIM] = y.astype(p_ref.dtype)


def _gates_kernel(h_ref, w_ref, bg_ref, g_ref):
    h = h_ref[...]
    for c0 in range(0, g_ref.shape[1], TN_GATES):
        cols = slice(c0, c0 + TN_GATES)
        g_ref[:, cols] = _sigmoid(_dot(h, w_ref[:, cols]) + bg_ref[:, cols]).astype(g_ref.dtype)


def _layer_weight_spec(w, layer, n_grid_axes):
    index = (lambda i: (layer, 0, 0)) if n_grid_axes == 1 else (lambda bi, i: (layer, 0, 0))
    return pl.BlockSpec((None,) + w.shape[1:], index, pipeline_mode=pl.Buffered(1))


def _inproj_call(h, w_p, w_g, layer, b_gate, qw, kw, tabs):
    m, d = h.shape
    gate_w = w_g.shape[2]
    assert w_p.shape[2] == P_W and P_W % TN == 0 and gate_w % TN_GATES == 0
    ltab = tabs[0].shape[0]
    tm = _row_tile(ltab, INPROJ_TM)
    assert m % tm == 0
    nt = ltab // tm
    tab_spec = pl.BlockSpec((tm, HEAD_DIM), lambda i: (i % nt, 0))
    vec_spec = pl.BlockSpec((1, HEAD_DIM), lambda i: (0, 0))
    p = pl.pallas_call(
        _inproj_kernel,
        grid=(m // tm,),
        in_specs=[pl.BlockSpec((tm, d), lambda i: (i, 0)), _layer_weight_spec(w_p, layer, 1),
                  vec_spec, vec_spec, tab_spec, tab_spec, tab_spec, tab_spec],
        out_specs=pl.BlockSpec((tm, P_W), lambda i: (i, 0)),
        out_shape=jax.ShapeDtypeStruct((m, P_W), BF16),
        compiler_params=_params(1),
        name="inproj",
    )(h, w_p, qw.reshape(1, HEAD_DIM), kw.reshape(1, HEAD_DIM), *tabs)
    tg = _row_tile(m, GATES_TM)
    g = pl.pallas_call(
        _gates_kernel,
        grid=(m // tg,),
        in_specs=[pl.BlockSpec((tg, d), lambda i: (i, 0)), _layer_weight_spec(w_g, layer, 1),
                  pl.BlockSpec((1, gate_w), lambda i: (0, 0))],
        out_specs=pl.BlockSpec((tg, gate_w), lambda i: (i, 0)),
        out_shape=jax.ShapeDtypeStruct((m, gate_w), BF16),
        compiler_params=_params(1),
        name="gates",
    )(h, w_g, b_gate.reshape(1, gate_w))
    return p, g


def _softmax2(s_a, s_b):
    m = jnp.maximum(jnp.max(s_a, axis=-1, keepdims=True), jnp.max(s_b, axis=-1, keepdims=True))
    e_a = jnp.exp2(s_a - m)
    e_b = jnp.exp2(s_b - m)
    return e_a, e_b, jnp.sum(e_a, axis=-1, keepdims=True) + jnp.sum(e_b, axis=-1, keepdims=True)


def _softmax1(s):
    m = jnp.max(s, axis=-1, keepdims=True)
    e = jnp.exp2(s - m)
    return e, jnp.sum(e, axis=-1, keepdims=True)


def _silu_gate(o, z):
    z = z.astype(F32)
    return o * (z * _sigmoid(z))


def _split_maps(q):
    lane = lax.broadcasted_iota(jnp.int32, q.shape, 1)
    zero = jnp.zeros_like(q)
    return jnp.where(lane < DC, q, zero), jnp.where(lane >= DC, q, zero)


def _lam(lq1_ref, lk1_ref, lq2_ref, lk2_ref, li_ref):
    return (jnp.exp(jnp.sum(lq1_ref[...] * lk1_ref[...], axis=-1, keepdims=True))
            - jnp.exp(jnp.sum(lq2_ref[...] * lk2_ref[...], axis=-1, keepdims=True)) + li_ref[:, 0:1])


UNITS = 4
ATTN_TQ = 512
SOFTMAX_ROWS = 128
MAX_CHAINS = 2


def _scores_stage(q, kl, kc, s_ref):
    length = kl.shape[0]
    s_ref[:, :length] = _dot_nt(q, kl)
    s_ref[:, length:] = _dot_nt(q, kc)


def _softmax_pv_stage(s_ref, p_ref, v_ext):
    tq, lk = s_ref.shape
    blocks = [slice(k * HEAD_DIM, (k + 1) * HEAD_DIM) for k in range(lk // HEAD_DIM)]
    for r0 in range(0, tq, SOFTMAX_ROWS):
        rows = slice(r0, r0 + SOFTMAX_ROWS)
        parts = [s_ref[rows, blk] for blk in blocks[:MAX_CHAINS]]
        for k, blk in enumerate(blocks[MAX_CHAINS:]):
            parts[k % MAX_CHAINS] = jnp.maximum(parts[k % MAX_CHAINS], s_ref[rows, blk])
        m = functools.reduce(jnp.maximum, parts)
        m = jnp.broadcast_to(jnp.max(m, axis=-1, keepdims=True), (SOFTMAX_ROWS, HEAD_DIM))
        for blk in blocks:
            p_ref[rows, blk] = jnp.exp2(s_ref[rows, blk] - m).astype(p_ref.dtype)
    o_ext = _dot(p_ref[...], v_ext)
    return o_ext[:, :HEAD_DIM], o_ext[:, HEAD_DIM:HEAD_DIM + 1]


def _fill_v_ext(v_ext_ref, vl, vc):
    length = vl.shape[0]
    v_ext_ref[:length, :HEAD_DIM] = vl
    v_ext_ref[length:, :HEAD_DIM] = vc
    v_ext_ref[:, HEAD_DIM:] = jnp.ones((v_ext_ref.shape[0], HEAD_DIM), v_ext_ref.dtype)


def _pipelined_units(first_step, prepare, unit_q, next_tile_q, unit_k, unit_v, finish, scratch):
    s_bufs, p_bufs = scratch[0:2], scratch[2:4]

    @pl.when(first_step)
    def _():
        prepare()
        _scores_stage(unit_q(0), *unit_k(0), s_bufs[0])

    for r in range(UNITS):
        cur, nxt = r % 2, (r + 1) % 2
        if r + 1 < UNITS:
            _scores_stage(unit_q(r + 1), *unit_k(r + 1), s_bufs[nxt])
        else:
            _scores_stage(next_tile_q(), *unit_k(0), s_bufs[nxt])
        o, l = _softmax_pv_stage(s_bufs[cur], p_bufs[cur], unit_v(r))
        finish(r, o, l)


def _pipeline_scratch(tq, lk, n_v):
    return ([pltpu.VMEM((tq, lk), F32)] * 2 + [pltpu.VMEM((tq, lk), BF16)] * 2
            + [pltpu.VMEM((n_v, lk, 2 * HEAD_DIM), BF16)])


def _gqa_kernel(q_ref, qn_ref, kl_ref, vl_ref, kc_ref, vc_ref, z_ref, o_ref, *scratch):
    def cols(r):
        return slice(r * HEAD_DIM, (r + 1) * HEAD_DIM)

    def finish(r, o, l):
        o_ref[0, :, cols(r)] = _silu_gate(o / l, z_ref[0, :, cols(r)]).astype(o_ref.dtype)

    v_ext_ref = scratch[-1]
    _pipelined_units(
        pl.program_id(2) == 0,
        lambda: _fill_v_ext(v_ext_ref.at[0], vl_ref[0], vc_ref[0]),
        lambda r: q_ref[0, :, cols(r)],
        lambda: qn_ref[0],
        lambda r: (kl_ref[0], kc_ref[0]),
        lambda r: v_ext_ref[0],
        finish, scratch)


def _gqa_call(p, pc):
    b, length, _ = p.shape
    lc = pc.shape[1]
    tq = _row_tile(length, ATTN_TQ)
    n_i = length // tq
    assert H_B // KV_B == UNITS
    gw = UNITS * HEAD_DIM
    kv = lambda n, off: pl.BlockSpec((1, n, HEAD_DIM), lambda bi, g, i: (bi, 0, off // HEAD_DIM + g))
    return pl.pallas_call(
        _gqa_kernel,
        grid=(b, KV_B, n_i),
        in_specs=[
            pl.BlockSpec((1, tq, gw), lambda bi, g, i: (bi, i, OFF_BQ // gw + g)),
            pl.BlockSpec((1, tq, HEAD_DIM),
                         lambda bi, g, i: (bi, jnp.minimum(i + 1, n_i - 1), OFF_BQ // HEAD_DIM + UNITS * g)),
            kv(length, OFF_BK), kv(length, OFF_BV), kv(lc, OFF_BK), kv(lc, OFF_BV),
            pl.BlockSpec((1, tq, gw), lambda bi, g, i: (bi, i, OFF_BZ // gw + g)),
        ],
        out_specs=pl.BlockSpec((1, tq, gw), lambda bi, g, i: (bi, i, g)),
        out_shape=jax.ShapeDtypeStruct((b, length, B_QW), BF16),
        scratch_shapes=_pipeline_scratch(tq, length + lc, 1),
        compiler_params=_params(3),
        name="gqa",
    )(p, p, p, p, pc, pc, p)


def _diff_core(q, k, v, lam, sw, li):
    q1, q2 = _split_maps(q)
    e1, l1 = _softmax1(_dot_nt(q1, k))
    e2, l2 = _softmax1(_dot_nt(q2, k))
    a = (e1 * (1.0 / l1) - e2 * (lam / l2)).astype(BF16)
    o = _head_rmsnorm(_dot(a, v), sw)
    return o * (1.0 - li)


def _diff_kernel(q_ref, qn_ref, kl_ref, vl_ref, kc_ref, vc_ref, z_ref, lq1_ref, lk1_ref, lq2_ref, lk2_ref, li_ref,
                 sw_ref, o_ref, *scratch):
    lam = _lam(lq1_ref, lk1_ref, lq2_ref, lk2_ref, li_ref)
    pending = {}

    def cols(hh):
        return slice(hh * HEAD_DIM, (hh + 1) * HEAD_DIM)

    def finish(r, o, l):
        hh = r // 2
        if r % 2 == 0:
            pending[hh] = o / l
            return
        o = pending.pop(hh) - o * (lam / l)
        o = _head_rmsnorm(o, sw_ref[...]) * (1.0 - li_ref[...])
        o_ref[0, :, cols(hh)] = _silu_gate(o, z_ref[0, :, cols(hh)]).astype(o_ref.dtype)

    v_ext_ref = scratch[-1]

    def prepare():
        for hh in range(UNITS // 2):
            _fill_v_ext(v_ext_ref.at[hh], vl_ref[0, :, cols(hh)], vc_ref[0, :, cols(hh)])

    _pipelined_units(
        pl.program_id(2) == 0,
        prepare,
        lambda r: _split_maps(q_ref[0, :, cols(r // 2)])[r % 2],
        lambda: _split_maps(qn_ref[0])[0],
        lambda r: (kl_ref[0, :, cols(r // 2)], kc_ref[0, :, cols(r // 2)]),
        lambda r: v_ext_ref[r // 2],
        finish, scratch)


def _diff_call(p, pc, lam_args, li, sw):
    b, length, _ = p.shape
    lc = pc.shape[1]
    tq = _row_tile(length, ATTN_TQ)
    n_i = length // tq
    heads = UNITS // 2
    assert H_C % heads == 0
    pw = heads * HEAD_DIM
    small = lambda w: pl.BlockSpec((1, w), lambda bi, g, i: (0, 0))
    kv = lambda n, off: pl.BlockSpec((1, n, pw), lambda bi, g, i: (bi, 0, off // pw + g))
    return pl.pallas_call(
        _diff_kernel,
        grid=(b, H_C // heads, n_i),
        in_specs=[
            pl.BlockSpec((1, tq, pw), lambda bi, g, i: (bi, i, OFF_CQ // pw + g)),
            pl.BlockSpec((1, tq, HEAD_DIM),
                         lambda bi, g, i: (bi, jnp.minimum(i + 1, n_i - 1), OFF_CQ // HEAD_DIM + heads * g)),
            kv(length, OFF_CK), kv(length, OFF_CV), kv(lc, OFF_CK), kv(lc, OFF_CV),
            pl.BlockSpec((1, tq, pw), lambda bi, g, i: (bi, i, OFF_CZ // pw + g)),
            small(DC), small(DC), small(DC), small(DC), small(HEAD_DIM), small(HEAD_DIM),
        ],
        out_specs=pl.BlockSpec((1, tq, pw), lambda bi, g, i: (bi, i, g)),
        out_shape=jax.ShapeDtypeStruct((b, length, C_W), BF16),
        scratch_shapes=_pipeline_scratch(tq, length + lc, heads),
        compiler_params=_params(3),
        name="diff",
    )(p, p, p, p, pc, pc, p, *lam_args, li, sw)


def _nbr_key_row0(t, rows):
    return jnp.clip(t * NBR_R - WIN_H // 2, 0, rows - NBR_KR)


def _nbr_kernel(q_ref, kl_ref, vl_ref, kc_ref, vc_ref, z_ref, bias_ref, o_ref, *, rows):
    t = pl.program_id(1)
    span = NBR_KR * GRID_W
    k_off = pl.multiple_of(_nbr_key_row0(t, rows) * GRID_W, NBR_R * GRID_W)
    for hh in range(H_A):
        cols = slice(hh * HEAD_DIM, (hh + 1) * HEAD_DIM)
        q = q_ref[0, :, cols]
        s_win = _dot_nt(q, kl_ref[0, pl.ds(k_off, span), cols]) + bias_ref[hh, 0]
        s_ctx = _dot_nt(q, kc_ref[0, :, cols])
        e_w, e_c, l = _softmax2(s_win, s_ctx)
        o = _dot(e_w.astype(BF16), vl_ref[0, pl.ds(k_off, span), cols]) + _dot(e_c.astype(BF16), vc_ref[0, :, cols])
        o_ref[0, :, cols] = _silu_gate(o / l, z_ref[0, :, cols]).astype(o_ref.dtype)


def _nbr_bias_selectors(rows):
    nt = rows // NBR_R
    kh = min(WIN_H, rows)
    n_dr, n_dc = 2 * WIN_H - 1, 2 * WIN_W - 1
    qc = np.arange(GRID_W)[:, None]
    kc = np.arange(GRID_W)[None, :]
    dc = np.clip(kc - qc, -(WIN_W - 1), WIN_W - 1) + (WIN_W - 1)
    cs = np.clip(qc - WIN_W // 2, 0, GRID_W - WIN_W)
    in_win = (kc >= cs) & (kc < cs + WIN_W)
    sel_dc = (dc[None] == np.arange(n_dc)[:, None, None]).astype(np.float32)
    i = np.arange(NBR_R)[:, None]
    jr = np.arange(NBR_KR)[None, :]
    sel_dr, in_band = [], []
    for t in (0, 1, nt - 1):
        r0 = NBR_R * t
        k0 = int(np.clip(r0 - WIN_H // 2, 0, rows - NBR_KR))
        rq = r0 + i
        rk = k0 + jr
        rs = np.clip(rq - kh // 2, 0, rows - kh)
        band = (rk >= rs) & (rk < rs + kh)
        dr = np.clip(rk - rq + (WIN_H - 1), 0, n_dr - 1)
        sel_dr.append((dr[..., None] == np.arange(n_dr)) & band[..., None])
        in_band.append(band)
    sel_dr = np.stack(sel_dr).astype(np.float32)
    valid = np.stack(in_band)[:, :, None, :, None] & in_win[None, None, :, None, :]
    return sel_dc, sel_dr, valid


def _nbr_bias_tables(rpb, rows):
    sel_dc, sel_dr, valid = _nbr_bias_selectors(rows)
    by_col = jnp.einsum("lhrd,dqk->lhrqk", rpb, sel_dc, precision=lax.Precision.HIGHEST)
    bias = jnp.einsum("tijr,lhrqk->lhtiqjk", sel_dr, by_col, precision=lax.Precision.HIGHEST)
    bias = jnp.where(valid[None, None], bias * LOG2E, NEG)
    return bias.reshape(rpb.shape[0], H_A, 3, NBR_R * GRID_W, NBR_KR * GRID_W)


def _nbr_call(p, pc, bias, layer):
    b, length, _ = p.shape
    lc = pc.shape[1]
    rows = length // GRID_W
    assert rows % NBR_R == 0 and rows >= NBR_KR
    tq = NBR_R * GRID_W
    nt = rows // NBR_R
    blk = lambda n, col: pl.BlockSpec((1, n, A_W), lambda bi, t: (bi, 0, col))
    tile = lambda col: pl.BlockSpec((1, tq, A_W), lambda bi, t: (bi, t, col))
    bias_type = lambda bi, t: (layer, 0, jnp.where(t == 0, 0, jnp.where(t == nt - 1, 2, 1)), 0, 0)
    return pl.pallas_call(
        functools.partial(_nbr_kernel, rows=rows),
        grid=(b, nt),
        in_specs=[
            tile(OFF_AQ // A_W), blk(length, OFF_AK // A_W), blk(length, OFF_AV // A_W),
            blk(lc, OFF_AK // A_W), blk(lc, OFF_AV // A_W), tile(OFF_AZ // A_W),
            pl.BlockSpec((None, H_A, 1, tq, NBR_KR * GRID_W), bias_type),
        ],
        out_specs=pl.BlockSpec((1, tq, A_W), lambda bi, t: (bi, t, 0)),
        out_shape=jax.ShapeDtypeStruct((b, length, A_W), BF16),
        compiler_params=_params(2),
        name="nbr",
    )(p, p, p, pc, pc, p, bias)


def _ctx_attn_kernel(p_ref, lq1_ref, lk1_ref, lq2_ref, lk2_ref, li_ref, sw_ref, oa_ref, ob_ref, oc_ref):
    def col(off, hh):
        return p_ref[0, :, off + hh * HEAD_DIM: off + (hh + 1) * HEAD_DIM]

    def plain(q, k, v):
        e, l = _softmax1(_dot_nt(q, k))
        return _dot(e.astype(BF16), v) / l

    for hh in range(H_A):
        o = plain(col(OFF_AQ, hh), col(OFF_AK, hh), col(OFF_AV, hh))
        oa_ref[0, :, hh * HEAD_DIM:(hh + 1) * HEAD_DIM] = _silu_gate(o, col(OFF_AZ, hh)).astype(oa_ref.dtype)
    rep = H_B // KV_B
    for hh in range(H_B):
        o = plain(col(OFF_BQ, hh), col(OFF_BK, hh // rep), col(OFF_BV, hh // rep))
        ob_ref[0, :, hh * HEAD_DIM:(hh + 1) * HEAD_DIM] = _silu_gate(o, col(OFF_BZ, hh)).astype(ob_ref.dtype)
    lam = _lam(lq1_ref, lk1_ref, lq2_ref, lk2_ref, li_ref)
    for hh in range(H_C):
        o = _diff_core(col(OFF_CQ, hh), col(OFF_CK, hh), col(OFF_CV, hh), lam, sw_ref[...], li_ref[...])
        oc_ref[0, :, hh * HEAD_DIM:(hh + 1) * HEAD_DIM] = _silu_gate(o, col(OFF_CZ, hh)).astype(oc_ref.dtype)


def _ctx_attn_call(pc, lam_args, li, sw):
    b, lc, _ = pc.shape
    small = lambda w: pl.BlockSpec((1, w), lambda bi: (0, 0))
    out = lambda w: pl.BlockSpec((1, lc, w), lambda bi: (bi, 0, 0))
    return pl.pallas_call(
        _ctx_attn_kernel,
        grid=(b,),
        in_specs=[pl.BlockSpec((1, lc, P_W), lambda bi: (bi, 0, 0)),
                  small(DC), small(DC), small(DC), small(DC), small(HEAD_DIM), small(HEAD_DIM)],
        out_specs=[out(A_W), out(B_QW), out(C_W)],
        out_shape=[jax.ShapeDtypeStruct((b, lc, A_W), BF16), jax.ShapeDtypeStruct((b, lc, B_QW), BF16),
                   jax.ShapeDtypeStruct((b, lc, C_W), BF16)],
        compiler_params=_params(1),
        name="ctx_attn",
    )(pc, *lam_args, li, sw)


def _residual(aa_ref, ab_ref, ac_ref, g_ref, wa_ref, wb_ref, wc_ref, wo_ref, x_ref, mod_ref):
    d = wo_ref.shape[0]
    y = None
    for k, (a_ref, w_ref) in enumerate(((aa_ref, wa_ref), (ab_ref, wb_ref), (ac_ref, wc_ref))):
        t = g_ref[0, :, k * d:(k + 1) * d].astype(F32) * _dot(a_ref[0], w_ref[...])
        y = t if y is None else y + t
    gate = mod_ref[0, 2:3, :]
    return x_ref[0] + gate * _dot(y.astype(BF16), wo_ref[...])


def _outproj_next_kernel(*refs):
    nw_ref, modn_ref, xo_ref, ho_ref = refs[-4:]
    x = _residual(*refs[:-4])
    xo_ref[0] = x
    ho_ref[0] = _modulated_norm(x, nw_ref[...], modn_ref[0]).astype(ho_ref.dtype)


def _outproj_final_kernel(*refs):
    nw_ref, o_ref = refs[-2:]
    o_ref[0] = _rms(_residual(*refs[:-2])) * nw_ref[...]


def _outproj_call(branches, g, weights, layer, x, mod, nw, mod_next):
    b, length, d = x.shape
    tm = _row_tile(length, 256)
    rows = lambda w: pl.BlockSpec((1, tm, w), lambda bi, i: (bi, i, 0))
    row_spec = rows(d)
    in_specs = ([rows(a.shape[-1]) for a in branches] + [rows(g.shape[-1])]
                + [_layer_weight_spec(w, layer, 2) for w in weights]
                + [row_spec, _mod_spec(mod), pl.BlockSpec((1, d), lambda bi, i: (0, 0))])
    args = [*branches, g, *weights, x, mod, nw.reshape(1, d)]
    if mod_next is None:
        return pl.pallas_call(
            _outproj_final_kernel, grid=(b, length // tm), in_specs=in_specs, out_specs=row_spec,
            out_shape=jax.ShapeDtypeStruct((b, length, d), F32), compiler_params=_params(2), name="outproj_final",
        )(*args)
    return pl.pallas_call(
        _outproj_next_kernel, grid=(b, length // tm), in_specs=in_specs + [_mod_spec(mod_next)],
        out_specs=[row_spec, row_spec],
        out_shape=[jax.ShapeDtypeStruct((b, length, d), F32), jax.ShapeDtypeStruct((b, length, d), BF16)],
        compiler_params=_params(2), name="outproj",
    )(*args, mod_next)


def _rope_tables(length):
    pos = jnp.arange(length)
    row = (pos // GRID_W).astype(F32)[:, None]
    col = (pos % GRID_W).astype(F32)[:, None]

    def cos_sin(rot_dim):
        n = rot_dim // 4
        inv_freq = ROPE_THETA ** (-jnp.arange(n, dtype=F32) / n)
        ang = jnp.concatenate([row * inv_freq, col * inv_freq], axis=-1)
        return jnp.cos(ang), jnp.sin(ang)

    cb, sb = cos_sin(HEAD_DIM)
    cc, sc = cos_sin(DC)
    return (jnp.concatenate([cb, cb], axis=-1), jnp.concatenate([-sb, sb], axis=-1),
            jnp.concatenate([cc, cc, cc, cc], axis=-1), jnp.concatenate([-sc, sc, -sc, sc], axis=-1))


def _identity_tables(length):
    one = jnp.ones((length, HEAD_DIM), F32)
    zero = jnp.zeros((length, HEAD_DIM), F32)
    return one, zero, one, zero


def kernel(x, c, ctx, c_ctx, norm_w, w_ada, b_ada, w_in, b_gate, rpb, q_norm_w, k_norm_w, lam_q1, lam_k1, lam_q2,
           lam_k2, subln_w, w_bo_a, w_bo_b, w_bo_c, w_out, final_norm_w):
    b, length, d = x.shape
    lc = ctx.shape[1]
    depth = w_in.shape[0]
    rows = length // GRID_W

    n_cond = b + 1
    pad = (-n_cond) % 8
    cc = jnp.concatenate([c, c_ctx[None], jnp.zeros((pad, d), F32)], axis=0)
    mod_all = _ada_call(cc, w_ada, b_ada).reshape(depth, n_cond + pad, 3, d)

    tabs_lat = _rope_tables(length)
    tabs_ctx = _identity_tables(_row_tile(b * lc, INPROJ_TM))
    nbr_bias = _nbr_bias_tables(rpb, rows)

    w_p_bf = w_in[:, :, :P_W].astype(BF16)
    w_g_bf = w_in[:, :, P_W:].astype(BF16)
    wa_bf, wb_bf, wc_bf, wo_bf = (w.astype(BF16) for w in (w_bo_a, w_bo_b, w_bo_c, w_out))

    xl, xc = x, ctx
    h = _norm_call(xl, norm_w[0], mod_all[0, :b])
    hc = _norm_call(xc, norm_w[0], mod_all[0, b:b + 1])
    out = None
    for l in range(depth):
        last = l == depth - 1
        mod_l = mod_all[l, :b]
        mod_c = mod_all[l, b:b + 1]
        p, g = _inproj_call(h.reshape(b * length, d), w_p_bf, w_g_bf, l, b_gate[l], q_norm_w[l], k_norm_w[l],
                            tabs_lat)
        pc, gc = _inproj_call(hc.reshape(b * lc, d), w_p_bf, w_g_bf, l, b_gate[l], q_norm_w[l], k_norm_w[l],
                              tabs_ctx)
        p = p.reshape(b, length, P_W)
        pc = pc.reshape(b, lc, P_W)

        lam_init = 0.8 - 0.6 * float(np.exp(-0.3 * l))
        li = jnp.full((1, HEAD_DIM), lam_init, F32)
        lam_args = [v[l].reshape(1, DC) for v in (lam_q1, lam_k1, lam_q2, lam_k2)]
        sw = subln_w[l].reshape(1, HEAD_DIM)

        o_a = _nbr_call(p, pc, nbr_bias, l)
        o_b = _gqa_call(p, pc)
        o_c = _diff_call(p, pc, lam_args, li, sw)
        weights = (wa_bf, wb_bf, wc_bf, wo_bf)
        if last:
            out = _outproj_call((o_a, o_b, o_c), g.reshape(b, length, -1), weights, l, xl, mod_l, final_norm_w, None)
        else:
            xl, h = _outproj_call((o_a, o_b, o_c), g.reshape(b, length, -1), weights, l, xl, mod_l, norm_w[l + 1],
                                  mod_all[l + 1, :b])
            xc, hc = _outproj_call(_ctx_attn_call(pc, lam_args, li, sw), gc.reshape(b, lc, -1), weights, l, xc,
                                   mod_c, norm_w[l + 1], mod_all[l + 1, b:b + 1])
    return out
```
